```python
import math
import jax, jax.numpy as jnp
from jax import lax
import numpy as np

D_MODEL = 1024
BATCH = 2
SEQ = 8192
DEPTH = 2

GRID_W = 64
CTX_LEN = 256
N_MIXERS = 2
N_SUB = 3
MACARON_WEIGHT = 0.5
FFN_HIDDEN = 2816
RMS_EPS = 1e-6
S5_GROUP = 16
S5_GROUPS = D_MODEL // S5_GROUP
S5_STATE = 64
S5_MIN_NEG_RE = -1e-4
NA_HEADS = 16
NA_HEAD_DIM = D_MODEL // NA_HEADS
WIN_H = 8
WIN_W = 16
N_SSM_LAYERS = (DEPTH + 1) // 2
N_NA_LAYERS = DEPTH // 2

kernel_name = "hybrid_s5_natten_macaron_dit"

F32 = jnp.float32


def _rms_norm(x, g):
    xf = x.astype(F32)
    y = xf * lax.rsqrt(jnp.mean(xf * xf, axis=-1, keepdims=True) + RMS_EPS)
    return (y * g.astype(F32)).astype(x.dtype)


def _modulate(x, g, shift, scale):
    return _rms_norm(x, g) * (1 + scale) + shift


def _swiglu(h, w_in, w_out):
    gate, up = jnp.split(h @ w_in, 2, axis=-1)
    return (jax.nn.silu(gate) * up) @ w_out


def _cmul(ar, ai, br, bi):
    return ar * br - ai * bi, ar * bi + ai * br


def _zoh(lam_re, lam_im, log_step, b_re, b_im):
    lr = jnp.minimum(lam_re.astype(F32), S5_MIN_NEG_RE)
    li = lam_im.astype(F32)
    dt = jnp.exp(log_step.astype(F32))[:, None]
    mag = jnp.exp(lr * dt)
    ar = mag * jnp.cos(li * dt)
    ai = mag * jnp.sin(li * dt)
    den = lr * lr + li * li
    cr = ((ar - 1) * lr + ai * li) / den
    ci = (ai * lr - (ar - 1) * li) / den
    bbr, bbi = _cmul(cr[..., None], ci[..., None], b_re.astype(F32), b_im.astype(F32))
    return ar, ai, bbr, bbi


def _scan_op(e1, e2):
    a1r, a1i, b1r, b1i = e1
    a2r, a2i, b2r, b2i = e2
    ar, ai = _cmul(a2r, a2i, a1r, a1i)
    tr, ti = _cmul(a2r, a2i, b1r, b1i)
    return ar, ai, tr + b2r, ti + b2i


def _ssm_scan(u, ar, ai, bbr, bbi, h0, reverse):
    bur = jnp.einsum('blgh,gph->blgp', u, bbr)
    bui = jnp.einsum('blgh,gph->blgp', u, bbi)
    if h0 is not None:
        sr, si = _cmul(ar, ai, h0[0], h0[1])
        t0 = -1 if reverse else 0
        bur = bur.at[:, t0].add(sr)
        bui = bui.at[:, t0].add(si)
    a_r = jnp.broadcast_to(ar, bur.shape)
    a_i = jnp.broadcast_to(ai, bur.shape)
    _, _, xr, xi = lax.associative_scan(_scan_op, (a_r, a_i, bur, bui), reverse=reverse, axis=1)
    return xr, xi


def _ssm_readout(xr, xi, c_re, c_im):
    return jnp.einsum('blgp,ghp->blgh', xr, c_re) - jnp.einsum('blgp,ghp->blgh', xi, c_im)


def _glu(y, w_glu):
    g = jax.nn.gelu(y)
    a, b = jnp.split(g @ w_glu, 2, axis=-1)
    return a * jax.nn.sigmoid(b)


def _s5_mixer(h, hc, w_in, lam_re, lam_im, log_step, b_re, b_im, c_re, c_im, d_skip, w_glu, need_ctx_out):
    bsz, seq_len, _ = h.shape
    clen = hc.shape[1]
    u = (h @ w_in).astype(F32).reshape(bsz, seq_len, S5_GROUPS, S5_GROUP)
    uc = (hc @ w_in).astype(F32).reshape(bsz, clen, S5_GROUPS, S5_GROUP)
    d = d_skip.astype(F32).reshape(S5_GROUPS, S5_GROUP)
    y = u * d
    yc = uc * d if need_ctx_out else None
    for dirn, rev in enumerate((False, True)):
        ar, ai, bbr, bbi = _zoh(lam_re[dirn], lam_im[dirn], log_step[dirn], b_re[dirn], b_im[dirn])
        cr = c_re[dirn].astype(F32)
        ci = c_im[dirn].astype(F32)
        xcr, xci = _ssm_scan(uc, ar, ai, bbr, bbi, None, rev)
        last = 0 if rev else -1
        xr, xi = _ssm_scan(u, ar, ai, bbr, bbi, (xcr[:, last], xci[:, last]), rev)
        y = y + _ssm_readout(xr, xi, cr, ci)
        if need_ctx_out:
            yc = yc + _ssm_readout(xcr, xci, cr, ci)
    out = _glu(y.reshape(bsz, seq_len, D_MODEL).astype(h.dtype), w_glu)
    out_c = _glu(yc.reshape(bsz, clen, D_MODEL).astype(hc.dtype), w_glu) if need_ctx_out else None
    return out, out_c


def _na_mixer(h, hc, w_qkv, q_g, k_g, rpb, w_o, need_ctx_out):
    bsz, seq_len, _ = h.shape
    clen = hc.shape[1]
    rows = seq_len // GRID_W
    kh = min(WIN_H, rows)
    scale = NA_HEAD_DIM ** -0.5
    qkv = (h @ w_qkv).reshape(bsz, seq_len, 3, NA_HEADS, NA_HEAD_DIM)
    q = _rms_norm(qkv[:, :, 0], q_g)
    k = _rms_norm(qkv[:, :, 1], k_g)
    v = qkv[:, :, 2]
    qkv_c = (hc @ w_qkv).reshape(bsz, clen, 3, NA_HEADS, NA_HEAD_DIM)
    kc = _rms_norm(qkv_c[:, :, 1], k_g)
    vc = qkv_c[:, :, 2]

    grid = (bsz, rows, GRID_W, NA_HEADS, NA_HEAD_DIM)
    qg, kg, vg = q.reshape(grid), k.reshape(grid), v.reshape(grid)
    r = jnp.arange(rows)
    rs = jnp.clip(r - kh // 2, 0, rows - kh)
    row_idx = rs[:, None] + jnp.arange(kh)[None, :]
    k_band = kg[:, row_idx]
    v_band = vg[:, row_idx]
    s_loc = jnp.einsum('brqhd,brikhd->bhrqik', qg, k_band, preferred_element_type=F32) * scale

    wcol = jnp.arange(GRID_W)
    cs = jnp.clip(wcol - WIN_W // 2, 0, GRID_W - WIN_W)
    col_valid = (wcol[None, :] >= cs[:, None]) & (wcol[None, :] < cs[:, None] + WIN_W)
    row_bias_idx = row_idx - r[:, None] + WIN_H - 1
    col_bias_idx = jnp.clip(wcol[None, :] - wcol[:, None] + WIN_W - 1, 0, 2 * WIN_W - 2)
    bias = rpb[:, row_bias_idx[:, None, :, None], col_bias_idx[None, :, None, :]]
    s_loc = jnp.where(col_valid[:, None, :], s_loc + bias.astype(F32), -jnp.inf)

    s_ctx = jnp.einsum('brqhd,bchd->bhrqc', qg, kc, preferred_element_type=F32) * scale
    n_loc = kh * GRID_W
    scores = jnp.concatenate([s_loc.reshape(bsz, NA_HEADS, rows, GRID_W, n_loc), s_ctx], axis=-1)
    p = jax.nn.softmax(scores, axis=-1)
    p_loc = p[..., :n_loc].reshape(bsz, NA_HEADS, rows, GRID_W, kh, GRID_W).astype(v.dtype)
    p_ctx = p[..., n_loc:].astype(v.dtype)
    o = (jnp.einsum('bhrqik,brikhd->brqhd', p_loc, v_band)
         + jnp.einsum('bhrqc,bchd->brqhd', p_ctx, vc))
    out = o.reshape(bsz, seq_len, D_MODEL) @ w_o

    out_c = None
    if need_ctx_out:
        qc = _rms_norm(qkv_c[:, :, 0], q_g)
        s_cc = jnp.einsum('bqhd,bkhd->bhqk', qc, kc, preferred_element_type=F32) * scale
        p_cc = jax.nn.softmax(s_cc, axis=-1).astype(vc.dtype)
        oc = jnp.einsum('bhqk,bkhd->bqhd', p_cc, vc)
        out_c = oc.reshape(bsz, clen, D_MODEL) @ w_o
    return out, out_c


def setup_inputs(seed: int = 0) -> dict:
    key = jax.random.key(seed)
    ks = jax.random.split(key, 24)
    D = D_MODEL
    F = FFN_HIDDEN
    G, P, GS = S5_GROUPS, S5_STATE, S5_GROUP

    def nrm(k, shape, s):
        return jax.random.normal(k, shape, F32) * s

    n = jnp.arange(P, dtype=F32)
    return {
        "x": nrm(ks[0], (BATCH, SEQ, D), 1.0),
        "c": nrm(ks[1], (BATCH, D), 1.0),
        "ctx": nrm(ks[2], (BATCH, CTX_LEN, D), 1.0),
        "c_ctx": nrm(ks[3], (D,), 1.0),
        "norm_g": 1.0 + nrm(ks[4], (DEPTH, N_SUB, D), 0.02),
        "ada_w": nrm(ks[5], (DEPTH, D, N_SUB * 3 * D), 0.5 * D ** -0.5),
        "ada_b": nrm(ks[6], (DEPTH, N_SUB * 3 * D), 0.02),
        "ffn_w_in": nrm(ks[7], (DEPTH, 2, D, 2 * F), D ** -0.5),
        "ffn_w_out": nrm(ks[8], (DEPTH, 2, F, D), F ** -0.5),
        "ssm_w_in": nrm(ks[9], (N_SSM_LAYERS, D, D), D ** -0.5),
        "ssm_lambda_re": -0.5 + nrm(ks[10], (N_SSM_LAYERS, 2, G, P), 0.01),
        "ssm_lambda_im": math.pi * n + nrm(ks[11], (N_SSM_LAYERS, 2, G, P), 0.01),
        "ssm_log_step": jax.random.uniform(ks[12], (N_SSM_LAYERS, 2, G), F32, math.log(1e-3), math.log(1e-1)),
        "ssm_b_re": nrm(ks[13], (N_SSM_LAYERS, 2, G, P, GS), (2 * GS) ** -0.5),
        "ssm_b_im": nrm(ks[14], (N_SSM_LAYERS, 2, G, P, GS), (2 * GS) ** -0.5),
        "ssm_c_re": nrm(ks[15], (N_SSM_LAYERS, 2, G, GS, P), P ** -0.5),
        "ssm_c_im": nrm(ks[16], (N_SSM_LAYERS, 2, G, GS, P), P ** -0.5),
        "ssm_d": nrm(ks[17], (N_SSM_LAYERS, D), 1.0),
        "ssm_w_glu": nrm(ks[18], (N_SSM_LAYERS, D, 2 * D), D ** -0.5),
        "na_w_qkv": nrm(ks[19], (N_NA_LAYERS, D, 3 * D), D ** -0.5),
        "na_q_norm": 1.0 + nrm(ks[20], (N_NA_LAYERS, NA_HEAD_DIM), 0.02),
        "na_k_norm": 1.0 + nrm(ks[21], (N_NA_LAYERS, NA_HEAD_DIM), 0.02),
        "na_rpb": nrm(ks[22], (N_NA_LAYERS, NA_HEADS, 2 * WIN_H - 1, 2 * WIN_W - 1), 0.02),
        "na_w_o": nrm(ks[23], (N_NA_LAYERS, D, D), D ** -0.5),
    }


def reference(x, c, ctx, c_ctx, norm_g, ada_w, ada_b, ffn_w_in, ffn_w_out,
              ssm_w_in, ssm_lambda_re, ssm_lambda_im, ssm_log_step, ssm_b_re, ssm_b_im,
              ssm_c_re, ssm_c_im, ssm_d, ssm_w_glu,
              na_w_qkv, na_q_norm, na_k_norm, na_rpb, na_w_o):
    D = D_MODEL
    sc = jax.nn.silu(c)
    scc = jax.nn.silu(c_ctx)
    for i in range(DEPTH):
        last = i == DEPTH - 1
        m = (sc @ ada_w[i] + ada_b[i]).reshape(-1, N_SUB, 3, D)[:, :, :, None, :]
        mc = (scc @ ada_w[i] + ada_b[i]).reshape(N_SUB, 3, D)

        h = _modulate(x, norm_g[i, 0], m[:, 0, 0], m[:, 0, 1])
        x = x + MACARON_WEIGHT * m[:, 0, 2] * _swiglu(h, ffn_w_in[i, 0], ffn_w_out[i, 0])
        hc = _modulate(ctx, norm_g[i, 0], mc[0, 0], mc[0, 1])
        ctx = ctx + MACARON_WEIGHT * mc[0, 2] * _swiglu(hc, ffn_w_in[i, 0], ffn_w_out[i, 0])

        h = _modulate(x, norm_g[i, 1], m[:, 1, 0], m[:, 1, 1])
        hc = _modulate(ctx, norm_g[i, 1], mc[1, 0], mc[1, 1])
        j = i // N_MIXERS
        if i % N_MIXERS == 0:
            y, yc = _s5_mixer(h, hc, ssm_w_in[j], ssm_lambda_re[j], ssm_lambda_im[j], ssm_log_step[j],
                              ssm_b_re[j], ssm_b_im[j], ssm_c_re[j], ssm_c_im[j], ssm_d[j], ssm_w_glu[j],
                              not last)
        else:
            y, yc = _na_mixer(h, hc, na_w_qkv[j], na_q_norm[j], na_k_norm[j], na_rpb[j], na_w_o[j],
                              not last)
        x = x + m[:, 1, 2] * y

        h = _modulate(x, norm_g[i, 2], m[:, 2, 0], m[:, 2, 1])
        x = x + MACARON_WEIGHT * m[:, 2, 2] * _swiglu(h, ffn_w_in[i, 1], ffn_w_out[i, 1])
        if not last:
            ctx = ctx + mc[1, 2] * yc
            hc = _modulate(ctx, norm_g[i, 2], mc[2, 0], mc[2, 1])
            ctx = ctx + MACARON_WEIGHT * mc[2, 2] * _swiglu(hc, ffn_w_in[i, 1], ffn_w_out[i, 1])
    return x
```

```python
import functools
import math

import jax
import jax.numpy as jnp
from jax import lax
from jax.experimental import pallas as pl
from jax.experimental.pallas import tpu as pltpu

F32 = jnp.float32
BF16 = jnp.bfloat16

GRID_W = 64
N_SUB = 3
MACARON_WEIGHT = 0.5
RMS_EPS = 1e-6
S5_GROUP = 16
S5_STATE = 64
S5_MIN_NEG_RE = -1e-4
NA_HEADS = 16
WIN_H = 8
WIN_W = 16

LANES = 128
VMEM_LIMIT = 56 * 1024 * 1024

TOKEN_TILE = 256
S5_CHUNK = 32
S5_GROUPS_PER_STEP = 4
NA_Q_ROWS = 8
NA_K_ROWS = 16


def _dot(a, b):
    return jnp.dot(a, b, preferred_element_type=F32)


def _dot_nt(a, b):
    return lax.dot_general(a, b, (((1,), (1,)), ((), ())), preferred_element_type=F32)


def _dot_nt_f32(a, b):
    return lax.dot_general(a, b, (((1,), (1,)), ((), ())), preferred_element_type=F32,
                           precision=lax.Precision.HIGHEST)


def _params(*sem):
    return pltpu.CompilerParams(dimension_semantics=sem, vmem_limit_bytes=VMEM_LIMIT)


def _resident(shape, index_map):
    return pl.BlockSpec(shape, index_map, pipeline_mode=pl.Buffered(1))


def _ada_kernel(c_ref, w_ref, b_ref, o_ref):
    c = c_ref[...]
    s = c * jax.nn.sigmoid(c)
    o_ref[0] = jnp.dot(s, w_ref[0], preferred_element_type=F32,
                       precision=lax.Precision.HIGHEST) + b_ref[0]


def _ada_modulation(cvec, ada_w, ada_b):
    depth, d, n = ada_w.shape
    tn = d
    return pl.pallas_call(
        _ada_kernel,
        grid=(depth, n // tn),
        in_specs=[pl.BlockSpec((8, d), lambda l, j: (0, 0)),
                  pl.BlockSpec((1, d, tn), lambda l, j: (l, 0, j)),
                  pl.BlockSpec((1, 1, tn), lambda l, j: (l, 0, j))],
        out_specs=pl.BlockSpec((1, 8, tn), lambda l, j: (l, 0, j)),
        out_shape=jax.ShapeDtypeStruct((depth, 8, n), F32),
        compiler_params=_params("parallel", "parallel"),
        name="ada_modulation",
    )(cvec, ada_w, ada_b.reshape(depth, 1, n))


def _modulated_norm(x, g, mod, first_row, ctx_rows):
    tm = x.shape[0]
    y = x * lax.rsqrt(jnp.mean(x * x, axis=-1, keepdims=True) + RMS_EPS) * g
    if ctx_rows == 0:
        shift, scale, gate = mod[3:4], mod[4:5], mod[5:6]
    else:
        is_ctx = (first_row + lax.broadcasted_iota(jnp.int32, (tm, 1), 0)) < ctx_rows
        shift = jnp.where(is_ctx, mod[0:1], mod[3:4])
        scale = jnp.where(is_ctx, mod[1:2], mod[4:5])
        gate = jnp.where(is_ctx, mod[2:3], mod[5:6])
    return y * (1.0 + scale) + shift, gate


def _ffn_kernel(x_ref, mod_ref, g_ref, win_ref, wout_ref, o_ref, *, ctx_rows, hidden):
    tm = x_ref.shape[1]
    x = x_ref[0]
    h, gate = _modulated_norm(x, g_ref[...], mod_ref[0], pl.program_id(1) * tm, ctx_rows)
    hb = h.astype(BF16)
    gt = _dot(hb, win_ref[:, :hidden])
    up = _dot(hb, win_ref[:, hidden:])
    act = (gt * jax.nn.sigmoid(gt) * up).astype(BF16)
    o_ref[0] = x + (MACARON_WEIGHT * gate) * _dot(act, wout_ref[...])


def _ffn(xs, mod, g, w_in, w_out, ctx_rows):
    b, s, d = xs.shape
    hidden = w_out.shape[0]
    tm = TOKEN_TILE
    return pl.pallas_call(
        functools.partial(_ffn_kernel, ctx_rows=ctx_rows, hidden=hidden),
        grid=(b, s // tm),
        in_specs=[pl.BlockSpec((1, tm, d), lambda i, t: (i, t, 0)),
                  pl.BlockSpec((1, 6, d), lambda i, t: (i, 0, 0)),
                  pl.BlockSpec((1, d), lambda i, t: (0, 0)),
                  _resident((d, 2 * hidden), lambda i, t: (0, 0)),
                  _resident((hidden, d), lambda i, t: (0, 0))],
        out_specs=pl.BlockSpec((1, tm, d), lambda i, t: (i, t, 0)),
        out_shape=jax.ShapeDtypeStruct((b, s, d), F32),
        compiler_params=_params("parallel", "parallel"),
        name="ffn",
    )(xs, mod, g, w_in, w_out)


def _s5_in_kernel(x_ref, mod_ref, g_ref, w_ref, u_ref, *, ctx_rows):
    tm = x_ref.shape[1]
    h, _ = _modulated_norm(x_ref[0], g_ref[...], mod_ref[0], pl.program_id(1) * tm, ctx_rows)
    u_ref[0] = _dot(h.astype(BF16), w_ref[...])


def _s5_in(xs, mod, g, w_in, ctx_rows):
    b, s, d = xs.shape
    tm = TOKEN_TILE
    return pl.pallas_call(
        functools.partial(_s5_in_kernel, ctx_rows=ctx_rows),
        grid=(b, s // tm),
        in_specs=[pl.BlockSpec((1, tm, d), lambda i, t: (i, t, 0)),
                  pl.BlockSpec((1, 6, d), lambda i, t: (i, 0, 0)),
                  pl.BlockSpec((1, d), lambda i, t: (0, 0)),
                  _resident((d, d), lambda i, t: (0, 0))],
        out_specs=pl.BlockSpec((1, tm, d), lambda i, t: (i, t, 0)),
        out_shape=jax.ShapeDtypeStruct((b, s, d), F32),
        compiler_params=_params("parallel", "parallel"),
        name="s5_in",
    )(xs, mod, g, w_in)


def _s5_prep_kernel(lr_ref, li_ref, ls_ref, btr_ref, bti_ref, cr_ref, ci_ref,
                    tp_ref, wre_ref, wim_ref, pr_ref, pi_ref, at_ref,
                    prs, pis, *, chunk):
    gs = S5_GROUP
    tk = chunk * gs
    lr = jnp.minimum(lr_ref[0, 0], S5_MIN_NEG_RE)
    li = li_ref[0, 0]
    dt = jnp.exp(ls_ref[0, 0])
    mag = jnp.exp(lr * dt)
    ar = mag * jnp.cos(li * dt)
    ai = mag * jnp.sin(li * dt)
    den = lr * lr + li * li
    zr = ((ar - 1.0) * lr + ai * li) / den
    zi = (ai * lr - (ar - 1.0) * li) / den
    btr = btr_ref[0, 0]
    bti = bti_ref[0, 0]
    bbr = zr * btr - zi * bti
    bbi = zr * bti + zi * btr
    cr = cr_ref[0, 0]
    ci = ci_ref[0, 0]

    nk = chunk + 8
    kk = lax.broadcasted_iota(jnp.int32, (nk, 1), 0).astype(F32)
    emag = jnp.exp(kk * (lr * dt))
    er = emag * jnp.cos(kk * (li * dt))
    ei = emag * jnp.sin(kk * (li * dt))

    for k in range(chunk + 1):
        erk, eik = er[k:k + 1], ei[k:k + 1]
        prs[k * gs:(k + 1) * gs, :] = cr * erk - ci * eik
        pis[k * gs:(k + 1) * gs, :] = cr * eik + ci * erk
        if k < chunk:
            s = chunk - 1 - k
            wre_ref[0, 0, s * gs:(s + 1) * gs, :] = (bbr * erk - bbi * eik).astype(BF16)
            wim_ref[0, 0, s * gs:(s + 1) * gs, :] = (bbr * eik + bbi * erk).astype(BF16)
    pr_ref[0, 0] = prs[gs:tk + gs, :].astype(BF16)
    pi_ref[0, 0] = pis[gs:tk + gs, :].astype(BF16)
    at_ref[0, 0, 0:1, :] = er[chunk:chunk + 1]
    at_ref[0, 0, 1:2, :] = ei[chunk:chunk + 1]

    kt = _dot_nt_f32(bbr, prs[0:tk, :]) - _dot_nt_f32(bbi, pis[0:tk, :])
    lane = lax.broadcasted_iota(jnp.int32, (gs, tk), 1)
    for s in range(chunk):
        blk = kt if s == 0 else jnp.where(lane >= s * gs, pltpu.roll(kt, s * gs, 1), 0.0)
        tp_ref[0, 0, s * gs:(s + 1) * gs, :] = blk.astype(BF16)


def _s5_prep(lam_re, lam_im, log_step, b_re, b_im, c_re, c_im, chunk):
    nd, g, p = lam_re.shape
    gs = S5_GROUP
    tk = chunk * gs
    pad = LANES - p

    def row(v, fill):
        return jnp.pad(v, ((0, 0), (0, 0), (0, pad)), constant_values=fill).reshape(nd, g, 1, LANES)

    def mat(v):
        return jnp.pad(v, ((0, 0), (0, 0), (0, 0), (0, pad)))

    args = (row(lam_re, -1.0), row(lam_im, 0.0), log_step.reshape(nd, g, 1, 1),
            mat(jnp.swapaxes(b_re, 2, 3)), mat(jnp.swapaxes(b_im, 2, 3)), mat(c_re), mat(c_im))
    vec = pl.BlockSpec((1, 1, 1, LANES), lambda d, i: (d, i, 0, 0))
    m16 = pl.BlockSpec((1, 1, gs, LANES), lambda d, i: (d, i, 0, 0))
    wide = pl.BlockSpec((1, 1, tk, LANES), lambda d, i: (d, i, 0, 0))
    return pl.pallas_call(
        functools.partial(_s5_prep_kernel, chunk=chunk),
        grid=(nd, g),
        in_specs=[vec, vec, pl.BlockSpec((1, 1, 1, 1), lambda d, i: (d, i, 0, 0)),
                  m16, m16, m16, m16],
        out_specs=[pl.BlockSpec((1, 1, tk, tk), lambda d, i: (d, i, 0, 0)),
                   wide, wide, wide, wide,
                   pl.BlockSpec((1, 1, 2, LANES), lambda d, i: (d, i, 0, 0))],
        out_shape=[jax.ShapeDtypeStruct((nd, g, tk, tk), BF16),
                   jax.ShapeDtypeStruct((nd, g, tk, LANES), BF16),
                   jax.ShapeDtypeStruct((nd, g, tk, LANES), BF16),
                   jax.ShapeDtypeStruct((nd, g, tk, LANES), BF16),
                   jax.ShapeDtypeStruct((nd, g, tk, LANES), BF16),
                   jax.ShapeDtypeStruct((nd, g, 2, LANES), F32)],
        scratch_shapes=[pltpu.VMEM(((chunk + 1) * gs, LANES), F32),
                        pltpu.VMEM(((chunk + 1) * gs, LANES), F32)],
        compiler_params=_params("parallel", "parallel"),
        name="s5_prep",
    )(*args)


def _split_bf16(x):
    hi = x.astype(BF16)
    return hi, (x - hi.astype(F32)).astype(BF16)


def _s5_chunk_kernel(u_ref, tp_ref, wre_ref, wim_ref, pr_ref, pi_ref, at_ref, y_ref,
                     sre, sim, hre, him, *, n_batch, n_chunks):
    n_groups = u_ref.shape[1]
    for j in range(n_groups):
        u = u_ref[0, j]
        sre[j] = _dot(u, wre_ref[0, j])
        sim[j] = _dot(u, wim_ref[0, j])

    ar = [at_ref[0, j, 0:1, :] for j in range(n_groups)]
    ai = [at_ref[0, j, 1:2, :] for j in range(n_groups)]

    def step(c, carry):
        nxt = []
        for j in range(n_groups):
            for b in range(n_batch):
                hr, hi = carry[2 * (j * n_batch + b)], carry[2 * (j * n_batch + b) + 1]
                row = b * n_chunks + c
                hre[j, pl.ds(row, 1), :] = hr
                him[j, pl.ds(row, 1), :] = hi
                sr = sre[j, pl.ds(row, 1), :]
                si = sim[j, pl.ds(row, 1), :]
                nxt.append(ar[j] * hr - ai[j] * hi + sr)
                nxt.append(ar[j] * hi + ai[j] * hr + si)
        return tuple(nxt)

    zero = jnp.zeros((1, LANES), F32)
    lax.fori_loop(0, n_chunks, step, (zero,) * (2 * n_groups * n_batch))

    for j in range(n_groups):
        hr_hi, hr_lo = _split_bf16(hre[j])
        hi_hi, hi_lo = _split_bf16(him[j])
        pr, pi = pr_ref[0, j], pi_ref[0, j]
        y = _dot(u_ref[0, j], tp_ref[0, j])
        y = y + (_dot_nt(hr_hi, pr) + _dot_nt(hr_lo, pr))
        y = y - (_dot_nt(hi_hi, pi) + _dot_nt(hi_lo, pi))
        y_ref[0, j] = y


def _s5_chunk_scan(ucat, mats, n_batch):
    nd, g, r, tk = ucat.shape
    gb = S5_GROUPS_PER_STEP
    tp, wre, wim, pr, pi, at = mats
    n_chunks = r // n_batch

    def blk(*tail):
        return pl.BlockSpec((1, gb) + tail, lambda d, i: (d, i, 0, 0))

    return pl.pallas_call(
        functools.partial(_s5_chunk_kernel, n_batch=n_batch, n_chunks=n_chunks),
        grid=(nd, g // gb),
        in_specs=[blk(r, tk), blk(tk, tk), blk(tk, LANES), blk(tk, LANES),
                  blk(tk, LANES), blk(tk, LANES), blk(2, LANES)],
        out_specs=blk(r, tk),
        out_shape=jax.ShapeDtypeStruct((nd, g, r, tk), F32),
        scratch_shapes=[pltpu.VMEM((gb, r, LANES), F32) for _ in range(4)],
        compiler_params=_params("parallel", "parallel"),
        name="s5_chunk_scan",
    )(ucat, tp, wre, wim, pr, pi, at)


def _gelu_tanh(y):
    return 0.5 * y * (1.0 + jnp.tanh(math.sqrt(2.0 / math.pi) * (y + 0.044715 * (y * y * y))))


def _s5_out_kernel(x_ref, u_ref, yf_ref, yb_ref, mod_ref, d_ref, w_ref, o_ref, *, ctx_rows):
    tm = x_ref.shape[1]
    d = x_ref.shape[2]
    first_row = pl.program_id(1) * tm
    mod = mod_ref[0]
    if ctx_rows == 0:
        gate = mod[5:6]
    else:
        is_ctx = (first_row + lax.broadcasted_iota(jnp.int32, (tm, 1), 0)) < ctx_rows
        gate = jnp.where(is_ctx, mod[2:3], mod[5:6])
    y = u_ref[0] * d_ref[...] + yf_ref[0] + yb_ref[0]
    z = _dot(_gelu_tanh(y).astype(BF16), w_ref[...])
    o_ref[0] = x_ref[0] + gate * (z[:, :d] * jax.nn.sigmoid(z[:, d:]))


def _s5_out(xs, u, yf, yb, mod, d_skip, w_glu, ctx_rows):
    b, s, d = xs.shape
    tm = TOKEN_TILE
    tok = pl.BlockSpec((1, tm, d), lambda i, t: (i, t, 0))
    return pl.pallas_call(
        functools.partial(_s5_out_kernel, ctx_rows=ctx_rows),
        grid=(b, s // tm),
        in_specs=[tok, tok, tok, tok,
                  pl.BlockSpec((1, 6, d), lambda i, t: (i, 0, 0)),
                  pl.BlockSpec((1, d), lambda i, t: (0, 0)),
                  _resident((d, 2 * d), lambda i, t: (0, 0))],
        out_specs=tok,
        out_shape=jax.ShapeDtypeStruct((b, s, d), F32),
        compiler_params=_params("parallel", "parallel"),
        name="s5_out",
    )(xs, u, yf, yb, mod, d_skip, w_glu)


def _to_chunks(u, chunk):
    b, s, d = u.shape
    g = d // S5_GROUP
    v = u.reshape(b, s // chunk, chunk, g, S5_GROUP)
    return jnp.transpose(v, (3, 0, 1, 2, 4)).reshape(g, b * (s // chunk), chunk * S5_GROUP)


def _from_chunks(y, b, chunk):
    g, r, tk = y.shape
    v = y.reshape(g, b, r // b, chunk, S5_GROUP)
    return jnp.transpose(v, (1, 2, 3, 0, 4)).reshape(b, (r // b) * chunk, g * S5_GROUP)


def _reverse_segments(u, ctx_rows):
    return jnp.concatenate([u[:, :ctx_rows][:, ::-1], u[:, ctx_rows:][:, ::-1]], axis=1)


def _s5_mixer(xs, mod, g, w_in, mats, d_skip, w_glu, ctx_rows):
    b = xs.shape[0]
    chunk = S5_CHUNK
    u = _s5_in(xs, mod, g, w_in, ctx_rows)
    ub = u.astype(BF16)
    ucat = jnp.stack([_to_chunks(ub, chunk), _to_chunks(_reverse_segments(ub, ctx_rows), chunk)])
    ycat = _s5_chunk_scan(ucat, mats, b)
    yf = _from_chunks(ycat[0], b, chunk)
    yb = _reverse_segments(_from_chunks(ycat[1], b, chunk), ctx_rows)
    return _s5_out(xs, u, yf, yb, mod, d_skip, w_glu, ctx_rows)


def _na_qkv_kernel(x_ref, mod_ref, g_ref, w_ref, ones_ref, qg_ref, kg_ref,
                   q_ref, k_ref, v_ref, *, ctx_rows, head_dim):
    tm, d = x_ref.shape[1], x_ref.shape[2]
    h, _ = _modulated_norm(x_ref[0], g_ref[...], mod_ref[0], pl.program_id(1) * tm, ctx_rows)
    qkv = _dot(h.astype(BF16), w_ref[...])

    def head_norm(z, gain):
        ms = _dot((z * z).astype(BF16), ones_ref[...]) * (1.0 / head_dim)
        return z * lax.rsqrt(ms + RMS_EPS) * gain

    q = head_norm(qkv[:, :d], qg_ref[...]) * (head_dim ** -0.5)
    k = head_norm(qkv[:, d:2 * d], kg_ref[...])
    v = qkv[:, 2 * d:]
    for hp in range(d // LANES):
        sl = slice(hp * LANES, (hp + 1) * LANES)
        q_ref[0, hp] = q[:, sl].astype(BF16)
        k_ref[0, hp] = k[:, sl].astype(BF16)
        v_ref[0, hp] = v[:, sl].astype(BF16)


def _na_qkv(xs, mod, g, w_qkv, q_gain, k_gain, ctx_rows):
    b, s, d = xs.shape
    tm = TOKEN_TILE
    head_dim = d // NA_HEADS
    hid = jnp.arange(d) // head_dim
    ones_bd = (hid[:, None] == hid[None, :]).astype(BF16)
    qg = jnp.tile(q_gain, NA_HEADS).reshape(1, d)
    kg = jnp.tile(k_gain, NA_HEADS).reshape(1, d)
    hp = d // LANES
    out = jax.ShapeDtypeStruct((b, hp, s, LANES), BF16)
    ospec = pl.BlockSpec((1, hp, tm, LANES), lambda i, t: (i, 0, t, 0))
    vec = pl.BlockSpec((1, d), lambda i, t: (0, 0))
    return pl.pallas_call(
        functools.partial(_na_qkv_kernel, ctx_rows=ctx_rows, head_dim=head_dim),
        grid=(b, s // tm),
        in_specs=[pl.BlockSpec((1, tm, d), lambda i, t: (i, t, 0)),
                  pl.BlockSpec((1, 6, d), lambda i, t: (i, 0, 0)),
                  vec,
                  _resident((d, 3 * d), lambda i, t: (0, 0)),
                  _resident((d, d), lambda i, t: (0, 0)),
                  vec, vec],
        out_specs=[ospec, ospec, ospec],
        out_shape=[out, out, out],
        compiler_params=_params("parallel", "parallel"),
        name="na_qkv",
    )(xs, mod, g, w_qkv, ones_bd, qg, kg)


def _window_start(r, rows):
    kh = min(WIN_H, rows)
    return min(max(r - kh // 2, 0), rows - kh)


def _na_tile_geometry(kind, rows):
    n_tiles = rows // NA_Q_ROWS
    tile = {0: 0, 1: 1, 2: n_tiles - 1}[kind]
    q0 = tile * NA_Q_ROWS
    k0 = min(max(q0 - (NA_K_ROWS - NA_Q_ROWS) // 2, 0), rows - NA_K_ROWS)
    return q0, k0


def _na_bias_kernel(rpb_ref, o_ref, *, rows):
    h = pl.program_id(0)
    w = GRID_W
    kh = min(WIN_H, rows)
    ncol = 2 * WIN_W - 1
    nrow = 2 * WIN_H - 1
    cq = lax.broadcasted_iota(jnp.int32, (w, LANES), 0)
    lane = lax.broadcasted_iota(jnp.int32, (w, LANES), 1)
    ck = lane % w
    left = lane < w
    cs = jnp.clip(cq - WIN_W // 2, 0, w - WIN_W)
    col_ok = (ck >= cs) & (ck < cs + WIN_W)
    dc = jnp.clip(ck - cq + WIN_W - 1, 0, ncol - 1)
    neg = jnp.full((w, LANES), -jnp.inf, F32)

    def pair_table(d_left, d_right):
        t = jnp.zeros((w, LANES), F32)
        for j in range(ncol):
            vl = rpb_ref[(h * nrow + d_left) * ncol + j] if d_left is not None else 0.0
            vr = rpb_ref[(h * nrow + d_right) * ncol + j] if d_right is not None else 0.0
            t = jnp.where(dc == j, jnp.where(left, vl, vr), t)
        ok = col_ok
        if d_left is None:
            ok = ok & jnp.logical_not(left)
        if d_right is None:
            ok = ok & left
        return jnp.where(ok, t, neg)

    cache = {}
    for kind in range(3):
        q0, k0 = _na_tile_geometry(kind, rows)
        for rq in range(NA_Q_ROWS):
            r = q0 + rq
            rs = _window_start(r, rows)
            for m in range(NA_K_ROWS // 2):
                ds = []
                for kr in (k0 + 2 * m, k0 + 2 * m + 1):
                    ds.append(kr - r + WIN_H - 1 if rs <= kr < rs + kh else None)
                key = tuple(ds)
                if key == (None, None):
                    blk = neg
                else:
                    if key not in cache:
                        cache[key] = pair_table(*key)
                    blk = cache[key]
                o_ref[kind, 0, rq * w:(rq + 1) * w, m * LANES:(m + 1) * LANES] = blk


def _na_bias(rpb, rows):
    nh = rpb.shape[0]
    nq, nk = NA_Q_ROWS * GRID_W, NA_K_ROWS * GRID_W
    return pl.pallas_call(
        functools.partial(_na_bias_kernel, rows=rows),
        grid=(nh,),
        in_specs=[pl.BlockSpec(memory_space=pltpu.SMEM)],
        out_specs=pl.BlockSpec((3, 1, nq, nk), lambda h: (0, h, 0, 0)),
        out_shape=jax.ShapeDtypeStruct((3, nh, nq, nk), F32),
        compiler_params=_params("parallel"),
        name="na_bias",
    )(rpb.reshape(-1))


def _na_attn_kernel(*refs, n_q, n_k):
    q_refs = refs[:n_q]
    k_refs = refs[n_q:n_q + n_k]
    v_refs = refs[n_q + n_k:n_q + 2 * n_k]
    kc_ref, vc_ref, bias_ref, o_ref = refs[n_q + 2 * n_k:]
    q = jnp.concatenate([r[0, 0] for r in q_refs], axis=0)
    k = jnp.concatenate([r[0, 0] for r in k_refs], axis=0)
    v = jnp.concatenate([r[0, 0] for r in v_refs], axis=0)
    kc, vc = kc_ref[0, 0], vc_ref[0, 0]
    half = LANES // 2
    first_head = lax.broadcasted_iota(jnp.int32, q.shape, 1) < half
    outs = []
    for hh in range(2):
        qh = jnp.where(first_head if hh == 0 else jnp.logical_not(first_head), q, jnp.zeros_like(q))
        s = _dot_nt(qh, k) + bias_ref[0, hh]
        sc = _dot_nt(qh, kc)
        m = jnp.maximum(jnp.max(s, axis=-1, keepdims=True), jnp.max(sc, axis=-1, keepdims=True))
        p = jnp.exp(s - m)
        pc = jnp.exp(sc - m)
        denom = jnp.sum(p, axis=-1, keepdims=True) + jnp.sum(pc, axis=-1, keepdims=True)
        o = _dot(p.astype(BF16), v) + _dot(pc.astype(BF16), vc)
        outs.append(o / denom)
    o_ref[0, 0] = jnp.where(first_head, outs[0], outs[1]).astype(BF16)


def _na_attention(q, k, v, bias, ctx_rows):
    b, hp, s, _ = q.shape
    tb = TOKEN_TILE
    assert ctx_rows == tb
    seq = s - ctx_rows
    rows = seq // GRID_W
    n_tiles = rows // NA_Q_ROWS
    n_q = NA_Q_ROWS * GRID_W // tb
    n_k = NA_K_ROWS * GRID_W // tb
    last_k0 = (rows - NA_K_ROWS) * GRID_W // tb

    def q_map(i):
        return lambda p, bi, t: (bi, p, 1 + n_q * t + i, 0)

    def k_map(i):
        def f(p, bi, t):
            first = jnp.clip(n_q * t - (n_k - n_q) // 2, 0, last_k0)
            return (bi, p, 1 + first + i, 0)
        return f

    def bias_map(p, bi, t):
        kind = jnp.where(t == 0, 0, jnp.where(t == n_tiles - 1, 2, 1))
        return (kind, p, 0, 0)

    tok = lambda f: pl.BlockSpec((1, 1, tb, LANES), f)
    ctx_spec = tok(lambda p, bi, t: (bi, p, 0, 0))
    in_specs = ([tok(q_map(i)) for i in range(n_q)] + [tok(k_map(i)) for i in range(n_k)]
                + [tok(k_map(i)) for i in range(n_k)] + [ctx_spec, ctx_spec]
                + [pl.BlockSpec((1, 2) + bias.shape[2:], bias_map)])
    args = [q] * n_q + [k] * n_k + [v] * n_k + [k, v, bias]
    nq = NA_Q_ROWS * GRID_W
    return pl.pallas_call(
        functools.partial(_na_attn_kernel, n_q=n_q, n_k=n_k),
        grid=(hp, b, n_tiles),
        in_specs=in_specs,
        out_specs=pl.BlockSpec((1, 1, nq, LANES), lambda p, bi, t: (bi, p, t, 0)),
        out_shape=jax.ShapeDtypeStruct((b, hp, seq, LANES), BF16),
        compiler_params=_params("parallel", "parallel", "parallel"),
        name="na_attention",
    )(*args)


def _na_out_kernel(x_ref, o_ref, mod_ref, w_ref, out_ref):
    o = jnp.concatenate([o_ref[0, p] for p in range(o_ref.shape[1])], axis=-1)
    out_ref[0] = x_ref[0] + mod_ref[0][5:6] * _dot(o, w_ref[...])


def _na_out(xs, o, mod, w_o, ctx_rows):
    b, s, d = xs.shape
    hp = o.shape[1]
    seq = s - ctx_rows
    tm = TOKEN_TILE
    skip = ctx_rows // tm
    return pl.pallas_call(
        _na_out_kernel,
        grid=(b, seq // tm),
        in_specs=[pl.BlockSpec((1, tm, d), lambda i, t: (i, t + skip, 0)),
                  pl.BlockSpec((1, hp, tm, LANES), lambda i, t: (i, 0, t, 0)),
                  pl.BlockSpec((1, 6, d), lambda i, t: (i, 0, 0)),
                  _resident((d, d), lambda i, t: (0, 0))],
        out_specs=pl.BlockSpec((1, tm, d), lambda i, t: (i, t, 0)),
        out_shape=jax.ShapeDtypeStruct((b, seq, d), F32),
        compiler_params=_params("parallel", "parallel"),
        name="na_out",
    )(xs, o, mod, w_o)


def _na_mixer(xs, mod, g, w_qkv, q_gain, k_gain, bias, w_o, ctx_rows):
    q, k, v = _na_qkv(xs, mod, g, w_qkv, q_gain, k_gain, ctx_rows)
    o = _na_attention(q, k, v, bias, ctx_rows)
    return _na_out(xs, o, mod, w_o, ctx_rows)


def kernel(x, c, ctx, c_ctx, norm_g, ada_w, ada_b, ffn_w_in, ffn_w_out, ssm_w_in, ssm_lambda_re, ssm_lambda_im, ssm_log_step, ssm_b_re, ssm_b_im, ssm_c_re, ssm_c_im, ssm_d, ssm_w_glu, na_w_qkv, na_q_norm, na_k_norm, na_rpb, na_w_o):
    b, seq, d = x.shape
    ctx_rows = ctx.shape[1]
    depth = norm_g.shape[0]
    assert b + 1 <= 8 and ctx_rows == TOKEN_TILE and depth == 2

    cvec = jnp.zeros((8, d), F32).at[:b].set(c).at[b].set(c_ctx)
    m = _ada_modulation(cvec, ada_w, ada_b).reshape(depth, 8, N_SUB, 3, d)

    def mod_rows(layer, sub):
        lat = m[layer, :b, sub]
        cx = jnp.broadcast_to(m[layer, b, sub], (b, 3, d))
        return jnp.concatenate([cx, lat], axis=1)

    def gain(layer, sub):
        return norm_g[layer, sub].reshape(1, d)

    w_in = ffn_w_in.astype(BF16)
    w_out = ffn_w_out.astype(BF16)
    xs = jnp.concatenate([ctx, x], axis=1)

    xs = _ffn(xs, mod_rows(0, 0), gain(0, 0), w_in[0, 0], w_out[0, 0], ctx_rows)
    mats = _s5_prep(ssm_lambda_re[0], ssm_lambda_im[0], ssm_log_step[0], ssm_b_re[0], ssm_b_im[0],
                    ssm_c_re[0], ssm_c_im[0], S5_CHUNK)
    xs = _s5_mixer(xs, mod_rows(0, 1), gain(0, 1), ssm_w_in[0].astype(BF16), mats,
                   ssm_d[0].reshape(1, d), ssm_w_glu[0].astype(BF16), ctx_rows)
    xs = _ffn(xs, mod_rows(0, 2), gain(0, 2), w_in[0, 1], w_out[0, 1], ctx_rows)

    xs = _ffn(xs, mod_rows(1, 0), gain(1, 0), w_in[1, 0], w_out[1, 0], ctx_rows)
    bias = _na_bias(na_rpb[0], seq // GRID_W)
    xl = _na_mixer(xs, mod_rows(1, 1), gain(1, 1), na_w_qkv[0].astype(BF16), na_q_norm[0],
                   na_k_norm[0], bias, na_w_o[0].astype(BF16), ctx_rows)
    return _ffn(xl, mod_rows(1, 2), gain(1, 2), w_in[1, 1], w_out[1, 1], 0)
```

```python
import functools
import math

import jax
import jax.numpy as jnp
from jax import lax
from jax.experimental import pallas as pl
from jax.experimental.pallas import tpu as pltpu

F32 = jnp.float32
BF16 = jnp.bfloat16

GRID_W = 64
N_SUB = 3
MACARON_WEIGHT = 0.5
RMS_EPS = 1e-6
S5_GROUP = 16
S5_STATE = 64
S5_MIN_NEG_RE = -1e-4
NA_HEADS = 16
WIN_H = 8
WIN_W = 16

LANES = 128
SUBLANES = 8
VMEM_LIMIT = 56 * 1024 * 1024

TOKEN_TILE = 256
S5_CHUNK = 8
S5_LANE_GROUPS = LANES // S5_GROUP
NA_Q_ROWS = 8
NA_K_ROWS = 16


def _dot(a, b):
    return jnp.dot(a, b, preferred_element_type=F32)


def _dot_nt(a, b):
    return lax.dot_general(a, b, (((1,), (1,)), ((), ())), preferred_element_type=F32)


def _dot_nt_f32(a, b):
    return lax.dot_general(a, b, (((1,), (1,)), ((), ())), preferred_element_type=F32,
                           precision=lax.Precision.HIGHEST)


def _params(*sem):
    return pltpu.CompilerParams(dimension_semantics=sem, vmem_limit_bytes=VMEM_LIMIT)


def _resident(shape, index_map):
    return pl.BlockSpec(shape, index_map, pipeline_mode=pl.Buffered(1))


def _ada_kernel(c_ref, w_ref, b_ref, o_ref):
    c = c_ref[...]
    s = c * jax.nn.sigmoid(c)
    o_ref[0] = jnp.dot(s, w_ref[0], preferred_element_type=F32,
                       precision=lax.Precision.HIGHEST) + b_ref[0]


def _ada_modulation(cvec, ada_w, ada_b):
    depth, d, n = ada_w.shape
    tn = d
    return pl.pallas_call(
        _ada_kernel,
        grid=(depth, n // tn),
        in_specs=[pl.BlockSpec((8, d), lambda l, j: (0, 0)),
                  pl.BlockSpec((1, d, tn), lambda l, j: (l, 0, j)),
                  pl.BlockSpec((1, 1, tn), lambda l, j: (l, 0, j))],
        out_specs=pl.BlockSpec((1, 8, tn), lambda l, j: (l, 0, j)),
        out_shape=jax.ShapeDtypeStruct((depth, 8, n), F32),
        compiler_params=_params("parallel", "parallel"),
        name="ada_modulation",
    )(cvec, ada_w, ada_b.reshape(depth, 1, n))


def _row_is_context(tm, first_row, ctx_rows):
    return (first_row + lax.broadcasted_iota(jnp.int32, (tm, 1), 0)) < ctx_rows


def _modulated_norm(x, g, mod, first_row, ctx_rows):
    tm = x.shape[0]
    y = x * lax.rsqrt(jnp.mean(x * x, axis=-1, keepdims=True) + RMS_EPS) * g
    if ctx_rows == 0:
        shift, scale, gate = mod[3:4], mod[4:5], mod[5:6]
    else:
        is_ctx = _row_is_context(tm, first_row, ctx_rows)
        shift = jnp.where(is_ctx, mod[0:1], mod[3:4])
        scale = jnp.where(is_ctx, mod[1:2], mod[4:5])
        gate = jnp.where(is_ctx, mod[2:3], mod[5:6])
    return y * (1.0 + scale) + shift, gate


def _ffn_kernel(x_ref, mod_ref, g_ref, win_ref, wout_ref, o_ref, *, ctx_rows, hidden):
    tm = x_ref.shape[1]
    x = x_ref[0]
    h, gate = _modulated_norm(x, g_ref[...], mod_ref[0], pl.program_id(1) * tm, ctx_rows)
    hb = h.astype(BF16)
    gt = _dot(hb, win_ref[:, :hidden])
    up = _dot(hb, win_ref[:, hidden:])
    act = (gt * jax.nn.sigmoid(gt) * up).astype(BF16)
    o_ref[0] = x + (MACARON_WEIGHT * gate) * _dot(act, wout_ref[...])


def _ffn(xs, mod, g, w_in, w_out, ctx_rows):
    b, s, d = xs.shape
    hidden = w_out.shape[0]
    tm = TOKEN_TILE
    return pl.pallas_call(
        functools.partial(_ffn_kernel, ctx_rows=ctx_rows, hidden=hidden),
        grid=(b, s // tm),
        in_specs=[pl.BlockSpec((1, tm, d), lambda i, t: (i, t, 0)),
                  pl.BlockSpec((1, 6, d), lambda i, t: (i, 0, 0)),
                  pl.BlockSpec((1, d), lambda i, t: (0, 0)),
                  _resident((d, 2 * hidden), lambda i, t: (0, 0)),
                  _resident((hidden, d), lambda i, t: (0, 0))],
        out_specs=pl.BlockSpec((1, tm, d), lambda i, t: (i, t, 0)),
        out_shape=jax.ShapeDtypeStruct((b, s, d), F32),
        compiler_params=_params("parallel", "parallel"),
        name="ffn",
    )(xs, mod, g, w_in, w_out)


def _s5_in_kernel(x_ref, mod_ref, g_ref, w_ref, u_ref, *, ctx_rows):
    tm = x_ref.shape[1]
    h, _ = _modulated_norm(x_ref[0], g_ref[...], mod_ref[0], pl.program_id(1) * tm, ctx_rows)
    u_ref[0] = _dot(h.astype(BF16), w_ref[...])


def _s5_in(xs, mod, g, w_in, ctx_rows):
    b, s, d = xs.shape
    tm = TOKEN_TILE
    return pl.pallas_call(
        functools.partial(_s5_in_kernel, ctx_rows=ctx_rows),
        grid=(b, s // tm),
        in_specs=[pl.BlockSpec((1, tm, d), lambda i, t: (i, t, 0)),
                  pl.BlockSpec((1, 6, d), lambda i, t: (i, 0, 0)),
                  pl.BlockSpec((1, d), lambda i, t: (0, 0)),
                  _resident((d, d), lambda i, t: (0, 0))],
        out_specs=pl.BlockSpec((1, tm, d), lambda i, t: (i, t, 0)),
        out_shape=jax.ShapeDtypeStruct((b, s, d), F32),
        compiler_params=_params("parallel", "parallel"),
        name="s5_in",
    )(xs, mod, g, w_in)


def _s5_prep_kernel(lr_ref, li_ref, ls_ref, btr_ref, bti_ref, cr_ref, ci_ref,
                    tpd_ref, wst_ref, prdt_ref, avec_ref, *, chunk, seg_chunks):
    n = LANES
    ng = S5_LANE_GROUPS
    half = n // 2
    wide = ng * half
    lr = jnp.minimum(lr_ref[0], S5_MIN_NEG_RE)
    li = li_ref[0]
    dt = jnp.exp(ls_ref[0])
    mag = jnp.exp(lr * dt)
    ar = mag * jnp.cos(li * dt)
    ai = mag * jnp.sin(li * dt)
    den = lr * lr + li * li
    zr = ((ar - 1.0) * lr + ai * li) / den
    zi = (ai * lr - (ar - 1.0) * li) / den
    btr, bti = btr_ref[0], bti_ref[0]
    bbr = zr * btr - zi * bti
    bbi = zr * bti + zi * btr
    cr, ci = cr_ref[0], ci_ref[0]

    lane = lax.broadcasted_iota(jnp.int32, (n, n), 1)
    row = lax.broadcasted_iota(jnp.int32, (n, n), 0)
    first_half = lane < half
    same_group = (row // S5_GROUP) == (lane // S5_GROUP)
    row_w = lax.broadcasted_iota(jnp.int32, (n, wide), 0)
    lane_w = lax.broadcasted_iota(jnp.int32, (n, wide), 1)
    own_states = (row_w // S5_GROUP) == (lane_w // half)

    def spread(v):
        return jnp.where(own_states, jnp.concatenate([v] * (wide // n), axis=1), 0.0).astype(BF16)

    powers = [(jnp.ones((n, n), F32), jnp.zeros((n, n), F32))]
    for _ in range(chunk):
        er, ei = powers[-1]
        powers.append((er * ar - ei * ai, er * ai + ei * ar))

    def input_map(k):
        er, ei = powers[k]
        return bbr * er - bbi * ei, bbr * ei + bbi * er

    def output_map(k):
        er, ei = powers[k]
        return cr * er - ci * ei, cr * ei + ci * er

    c2 = jnp.where(first_half, cr, -ci)
    lag_blocks = []
    for k in range(chunk):
        wr, wi = input_map(k)
        kd = _dot_nt_f32(jnp.where(first_half, wr, wi), c2)
        lag_blocks.append(jnp.where(same_group, kd, 0.0).astype(BF16))
    zero_block = jnp.zeros((n, n), BF16)

    def emit(reverse):
        for s in range(chunk):
            wr, wi = input_map(s if reverse else chunk - 1 - s)
            wst_ref[0, 0, s * n:(s + 1) * n, 0:wide] = spread(wr)
            wst_ref[0, 0, s * n:(s + 1) * n, wide:2 * wide] = spread(wi)
        for t in range(chunk):
            pr, pi = output_map(chunk - t if reverse else t + 1)
            prdt_ref[0, 0, t * n:(t + 1) * n, 0:wide] = spread(pr)
            prdt_ref[0, 0, t * n:(t + 1) * n, wide:2 * wide] = spread(-pi)
        for s in range(chunk):
            for t in range(chunk):
                lag = s - t if reverse else t - s
                tpd_ref[0, 0, s * n:(s + 1) * n, t * n:(t + 1) * n] = (
                    lag_blocks[lag] if lag >= 0 else zero_block)

    @pl.when(pl.program_id(0) == 0)
    def _():
        emit(False)

    @pl.when(pl.program_id(0) == 1)
    def _():
        emit(True)

    er, ei = powers[chunk]
    sr, si = jnp.ones((n, n), F32), jnp.zeros((n, n), F32)
    pr_, pi_ = er, ei
    e = seg_chunks
    while e:
        if e & 1:
            sr, si = sr * pr_ - si * pi_, sr * pi_ + si * pr_
        pr_, pi_ = pr_ * pr_ - pi_ * pi_, 2.0 * pr_ * pi_
        e >>= 1
    for r, v in enumerate((er, ei, sr, si)):
        for j in range(ng // 2):
            g0 = 2 * j * S5_GROUP
            g1 = g0 + S5_GROUP
            avec_ref[0, 0, r:r + 1, j * n:(j + 1) * n] = jnp.where(
                first_half[0:1], v[g0:g0 + 1], v[g1:g1 + 1])
    avec_ref[0, 0, 4:8, :] = jnp.zeros((4, wide), F32)


def _s5_prep(lam_re, lam_im, log_step, b_re, b_im, c_re, c_im, chunk, seg_chunks):
    nd, g, p = lam_re.shape
    gs = S5_GROUP
    assert 2 * p == LANES and g % S5_LANE_GROUPS == 0
    n_blocks = g // S5_LANE_GROUPS
    tk = chunk * LANES
    wide = S5_LANE_GROUPS * p

    def rows(v):
        v = v.reshape(nd, g * gs, p)
        return jnp.concatenate([v, v], axis=-1)

    def per_group(v):
        return rows(jnp.broadcast_to(v[:, :, None, :], (nd, g, gs, p)))

    args = (per_group(lam_re), per_group(lam_im),
            per_group(jnp.broadcast_to(log_step[:, :, None], (nd, g, p))),
            rows(jnp.swapaxes(b_re, 2, 3)), rows(jnp.swapaxes(b_im, 2, 3)), rows(c_re), rows(c_im))
    sq = pl.BlockSpec((1, LANES, LANES), lambda d, i: (d, i, 0))
    big = pl.BlockSpec((1, 1, tk, tk), lambda d, i: (d, i, 0, 0))
    assert tk == 2 * wide
    return pl.pallas_call(
        functools.partial(_s5_prep_kernel, chunk=chunk, seg_chunks=seg_chunks),
        grid=(nd, n_blocks),
        in_specs=[sq] * 7,
        out_specs=[big, big, big, pl.BlockSpec((1, 1, 8, wide), lambda d, i: (d, i, 0, 0))],
        out_shape=[jax.ShapeDtypeStruct((nd, n_blocks, tk, tk), BF16)] * 3
                  + [jax.ShapeDtypeStruct((nd, n_blocks, 8, wide), F32)],
        compiler_params=_params("parallel", "parallel"),
        name="s5_prep",
    )(*args)


def _s5_scan_kernel(u_ref, dsk_ref, tpd_ref, wst_ref, prdt_ref, avec_ref, y_ref,
                    xcat, sv, *, chunk, n_ctx):
    n = LANES
    n_c = xcat.shape[0]
    nb = sv.shape[0] // 2
    n_seg = SUBLANES
    seg = n_c // n_seg
    n_lat = n_c - n_ctx
    direction = pl.program_id(2)

    @pl.when(direction == 0)
    def _():
        for s in range(chunk):
            xcat[:, s * n:(s + 1) * n] = u_ref[0, pl.ds(s, n_c, stride=chunk), :].astype(BF16)

    def run(reverse):
        wst = wst_ref[0, 0]

        def summarise(dst, src):
            s = _dot(xcat[src, :], wst)
            for j in range(2 * nb):
                sv[j, dst, :] = s[:, j * n:(j + 1) * n]

        if reverse:
            summarise(slice(0, n_lat), slice(n_ctx, n_c))
            summarise(slice(n_lat, n_c), slice(0, n_ctx))
        else:
            summarise(slice(0, n_c), slice(0, n_c))

        def lane_blocks(r):
            return [avec_ref[0, 0, r:r + 1, j * n:(j + 1) * n] for j in range(nb)]

        ar = [jnp.broadcast_to(v, (n_seg, n)) for v in lane_blocks(0)]
        ai = [jnp.broadcast_to(v, (n_seg, n)) for v in lane_blocks(1)]

        def rows_at(i):
            return pl.ds(seg - 1 - i if reverse else i, n_seg, stride=seg)

        def advance(h, rows):
            out_r, out_i = [], []
            for j in range(nb):
                hr, hi = h[j], h[nb + j]
                out_r.append(ar[j] * hr - ai[j] * hi + sv[j, rows, :])
                out_i.append(ar[j] * hi + ai[j] * hr + sv[nb + j, rows, :])
            return tuple(out_r + out_i)

        zero = jnp.zeros((n_seg, n), F32)
        fin = lax.fori_loop(0, seg, lambda i, h: advance(h, rows_at(i)), (zero,) * (2 * nb))

        asr, asi = lane_blocks(2), lane_blocks(3)
        order = range(n_seg - 1, -1, -1) if reverse else range(n_seg)
        h0 = []
        for j in range(nb):
            rows_r, rows_i = [None] * n_seg, [None] * n_seg
            pr = pi = jnp.zeros((1, n), F32)
            prev = None
            for sgm in order:
                if prev is not None:
                    pr, pi = (asr[j] * pr - asi[j] * pi + fin[j][prev:prev + 1],
                              asr[j] * pi + asi[j] * pr + fin[nb + j][prev:prev + 1])
                rows_r[sgm], rows_i[sgm] = pr, pi
                prev = sgm
            h0.append((jnp.concatenate(rows_r, axis=0), jnp.concatenate(rows_i, axis=0)))
        h0 = tuple(v[0] for v in h0) + tuple(v[1] for v in h0)

        def step(i, h):
            rows = rows_at(i)
            nxt = advance(h, rows)
            for j in range(2 * nb):
                sv[j, rows, :] = h[j]
            return nxt

        lax.fori_loop(0, seg, step, h0)

        def entry_states(rows):
            return jnp.concatenate([sv[j, rows, :] for j in range(2 * nb)], axis=1).astype(BF16)

        if reverse:
            h_all = jnp.concatenate([entry_states(slice(n_lat, n_c)), entry_states(slice(0, n_lat))],
                                    axis=0)
        else:
            h_all = entry_states(slice(0, n_c))
        return _dot(xcat[...], tpd_ref[0, 0]) + _dot_nt(h_all, prdt_ref[0, 0])

    @pl.when(direction == 0)
    def _():
        y = run(False)
        for t in range(chunk):
            rows = pl.ds(t, n_c, stride=chunk)
            y_ref[0, rows, :] = y[:, t * n:(t + 1) * n] + u_ref[0, rows, :] * dsk_ref[...]

    @pl.when(direction == 1)
    def _():
        y = run(True)
        for t in range(chunk):
            rows = pl.ds(t, n_c, stride=chunk)
            y_ref[0, rows, :] = y_ref[0, rows, :] + y[:, t * n:(t + 1) * n]


def _s5_scan(u, d_skip, mats, chunk, ctx_rows):
    b, s, d = u.shape
    tpd, wst, prdt, avec = mats
    nd, n_blocks, tk, _ = tpd.shape
    wide = avec.shape[-1]
    n_c = s // chunk

    def mat(shape):
        return pl.BlockSpec((1, 1) + shape, lambda o, i, dr: (dr, o, 0, 0))

    tok = pl.BlockSpec((1, s, LANES), lambda o, i, dr: (i, 0, o))
    return pl.pallas_call(
        functools.partial(_s5_scan_kernel, chunk=chunk, n_ctx=ctx_rows // chunk),
        grid=(n_blocks, b, nd),
        in_specs=[tok, pl.BlockSpec((1, LANES), lambda o, i, dr: (0, o)),
                  mat((tk, tk)), mat((tk, tk)), mat((tk, tk)), mat((8, wide))],
        out_specs=tok,
        out_shape=jax.ShapeDtypeStruct((b, s, d), F32),
        scratch_shapes=[pltpu.VMEM((n_c, tk), BF16),
                        pltpu.VMEM((2 * wide // LANES, n_c, LANES), F32)],
        compiler_params=_params("parallel", "parallel", "arbitrary"),
        name="s5_scan",
    )(u, d_skip, tpd, wst, prdt, avec)


def _gelu_tanh(y):
    return 0.5 * y * (1.0 + jnp.tanh(math.sqrt(2.0 / math.pi) * (y + 0.044715 * (y * y * y))))


def _s5_out_kernel(x_ref, y_ref, mod_ref, w_ref, o_ref, *, ctx_rows):
    tm = x_ref.shape[1]
    d = x_ref.shape[2]
    mod = mod_ref[0]
    if ctx_rows == 0:
        gate = mod[5:6]
    else:
        gate = jnp.where(_row_is_context(tm, pl.program_id(1) * tm, ctx_rows), mod[2:3], mod[5:6])
    z = _dot(_gelu_tanh(y_ref[0]).astype(BF16), w_ref[...])
    o_ref[0] = x_ref[0] + gate * (z[:, :d] * jax.nn.sigmoid(z[:, d:]))


def _s5_out(xs, y, mod, w_glu, ctx_rows):
    b, s, d = xs.shape
    tm = TOKEN_TILE
    tok = pl.BlockSpec((1, tm, d), lambda i, t: (i, t, 0))
    return pl.pallas_call(
        functools.partial(_s5_out_kernel, ctx_rows=ctx_rows),
        grid=(b, s // tm),
        in_specs=[tok, tok,
                  pl.BlockSpec((1, 6, d), lambda i, t: (i, 0, 0)),
                  _resident((d, 2 * d), lambda i, t: (0, 0))],
        out_specs=tok,
        out_shape=jax.ShapeDtypeStruct((b, s, d), F32),
        compiler_params=_params("parallel", "parallel"),
        name="s5_out",
    )(xs, y, mod, w_glu)


def _s5_mixer(xs, mod, g, w_in, mats, d_skip, w_glu, ctx_rows):
    u = _s5_in(xs, mod, g, w_in, ctx_rows)
    y = _s5_scan(u, d_skip, mats, S5_CHUNK, ctx_rows)
    return _s5_out(xs, y, mod, w_glu, ctx_rows)


def _na_qkv_kernel(x_ref, mod_ref, g_ref, w_ref, ones_ref, qg_ref, kg_ref,
                   q_ref, k_ref, v_ref, *, ctx_rows, head_dim):
    tm, d = x_ref.shape[1], x_ref.shape[2]
    h, _ = _modulated_norm(x_ref[0], g_ref[...], mod_ref[0], pl.program_id(1) * tm, ctx_rows)
    qkv = _dot(h.astype(BF16), w_ref[...])

    def head_norm(z, gain):
        ms = _dot((z * z).astype(BF16), ones_ref[...]) * (1.0 / head_dim)
        return z * lax.rsqrt(ms + RMS_EPS) * gain

    q = head_norm(qkv[:, :d], qg_ref[...]) * (head_dim ** -0.5)
    k = head_norm(qkv[:, d:2 * d], kg_ref[...])
    v = qkv[:, 2 * d:]
    for hp in range(d // LANES):
        sl = slice(hp * LANES, (hp + 1) * LANES)
        q_ref[0, hp] = q[:, sl].astype(BF16)
        k_ref[0, hp] = k[:, sl].astype(BF16)
        v_ref[0, hp] = v[:, sl].astype(BF16)


def _na_qkv(xs, mod, g, w_qkv, q_gain, k_gain, ctx_rows):
    b, s, d = xs.shape
    tm = TOKEN_TILE
    head_dim = d // NA_HEADS
    hid = jnp.arange(d) // head_dim
    ones_bd = (hid[:, None] == hid[None, :]).astype(BF16)
    qg = jnp.tile(q_gain, NA_HEADS).reshape(1, d)
    kg = jnp.tile(k_gain, NA_HEADS).reshape(1, d)
    hp = d // LANES
    out = jax.ShapeDtypeStruct((b, hp, s, LANES), BF16)
    ospec = pl.BlockSpec((1, hp, tm, LANES), lambda i, t: (i, 0, t, 0))
    vec = pl.BlockSpec((1, d), lambda i, t: (0, 0))
    return pl.pallas_call(
        functools.partial(_na_qkv_kernel, ctx_rows=ctx_rows, head_dim=head_dim),
        grid=(b, s // tm),
        in_specs=[pl.BlockSpec((1, tm, d), lambda i, t: (i, t, 0)),
                  pl.BlockSpec((1, 6, d), lambda i, t: (i, 0, 0)),
                  vec,
                  _resident((d, 3 * d), lambda i, t: (0, 0)),
                  _resident((d, d), lambda i, t: (0, 0)),
                  vec, vec],
        out_specs=[ospec, ospec, ospec],
        out_shape=[out, out, out],
        compiler_params=_params("parallel", "parallel"),
        name="na_qkv",
    )(xs, mod, g, w_qkv, ones_bd, qg, kg)


def _window_start(r, rows):
    kh = min(WIN_H, rows)
    return min(max(r - kh // 2, 0), rows - kh)


def _na_tile_geometry(kind, rows):
    n_tiles = rows // NA_Q_ROWS
    tile = {0: 0, 1: 1, 2: n_tiles - 1}[kind]
    q0 = tile * NA_Q_ROWS
    k0 = min(max(q0 - (NA_K_ROWS - NA_Q_ROWS) // 2, 0), rows - NA_K_ROWS)
    return q0, k0


def _na_bias_kernel(rpb_ref, o_ref, *, rows):
    h = pl.program_id(0)
    w = GRID_W
    kh = min(WIN_H, rows)
    ncol = 2 * WIN_W - 1
    nrow = 2 * WIN_H - 1
    cq = lax.broadcasted_iota(jnp.int32, (w, LANES), 0)
    lane = lax.broadcasted_iota(jnp.int32, (w, LANES), 1)
    ck = lane % w
    left = lane < w
    cs = jnp.clip(cq - WIN_W // 2, 0, w - WIN_W)
    col_ok = (ck >= cs) & (ck < cs + WIN_W)
    dc = jnp.clip(ck - cq + WIN_W - 1, 0, ncol - 1)
    neg = jnp.full((w, LANES), -jnp.inf, F32)

    def pair_table(d_left, d_right):
        t = jnp.zeros((w, LANES), F32)
        for j in range(ncol):
            vl = rpb_ref[(h * nrow + d_left) * ncol + j] if d_left is not None else 0.0
            vr = rpb_ref[(h * nrow + d_right) * ncol + j] if d_right is not None else 0.0
            t = jnp.where(dc == j, jnp.where(left, vl, vr), t)
        ok = col_ok
        if d_left is None:
            ok = ok & jnp.logical_not(left)
        if d_right is None:
            ok = ok & left
        return jnp.where(ok, t, neg)

    cache = {}
    for kind in range(3):
        q0, k0 = _na_tile_geometry(kind, rows)
        for rq in range(NA_Q_ROWS):
            r = q0 + rq
            rs = _window_start(r, rows)
            for m in range(NA_K_ROWS // 2):
                ds = []
                for kr in (k0 + 2 * m, k0 + 2 * m + 1):
                    ds.append(kr - r + WIN_H - 1 if rs <= kr < rs + kh else None)
                key = tuple(ds)
                if key == (None, None):
                    blk = neg
                else:
                    if key not in cache:
                        cache[key] = pair_table(*key)
                    blk = cache[key]
                o_ref[kind, 0, rq * w:(rq + 1) * w, m * LANES:(m + 1) * LANES] = blk


def _na_bias(rpb, rows):
    nh = rpb.shape[0]
    nq, nk = NA_Q_ROWS * GRID_W, NA_K_ROWS * GRID_W
    return pl.pallas_call(
        functools.partial(_na_bias_kernel, rows=rows),
        grid=(nh,),
        in_specs=[pl.BlockSpec(memory_space=pltpu.SMEM)],
        out_specs=pl.BlockSpec((3, 1, nq, nk), lambda h: (0, h, 0, 0)),
        out_shape=jax.ShapeDtypeStruct((3, nh, nq, nk), F32),
        compiler_params=_params("parallel"),
        name="na_bias",
    )(rpb.reshape(-1))


def _na_attn_kernel(*refs, n_q, n_k):
    q_refs = refs[:n_q]
    k_refs = refs[n_q:n_q + n_k]
    v_refs = refs[n_q + n_k:n_q + 2 * n_k]
    kc_ref, vc_ref, bias_ref, o_ref = refs[n_q + 2 * n_k:]
    q = jnp.concatenate([r[0, 0] for r in q_refs], axis=0)
    k = jnp.concatenate([r[0, 0] for r in k_refs], axis=0)
    v = jnp.concatenate([r[0, 0] for r in v_refs], axis=0)
    kc, vc = kc_ref[0, 0], vc_ref[0, 0]
    half = LANES // 2
    first_head = lax.broadcasted_iota(jnp.int32, q.shape, 1) < half
    outs = []
    for hh in range(2):
        qh = jnp.where(first_head if hh == 0 else jnp.logical_not(first_head), q, jnp.zeros_like(q))
        s = _dot_nt(qh, k) + bias_ref[0, hh]
        sc = _dot_nt(qh, kc)
        m = jnp.maximum(jnp.max(s, axis=-1, keepdims=True), jnp.max(sc, axis=-1, keepdims=True))
        p = jnp.exp(s - m)
        pc = jnp.exp(sc - m)
        denom = jnp.sum(p, axis=-1, keepdims=True) + jnp.sum(pc, axis=-1, keepdims=True)
        o = _dot(p.astype(BF16), v) + _dot(pc.astype(BF16), vc)
        outs.append(o / denom)
    o_ref[0, 0] = jnp.where(first_head, outs[0], outs[1]).astype(BF16)


def _na_attention(q, k, v, bias, ctx_rows):
    b, hp, s, _ = q.shape
    tb = TOKEN_TILE
    assert ctx_rows == tb
    seq = s - ctx_rows
    rows = seq // GRID_W
    n_tiles = rows // NA_Q_ROWS
    n_q = NA_Q_ROWS * GRID_W // tb
    n_k = NA_K_ROWS * GRID_W // tb
    last_k0 = (rows - NA_K_ROWS) * GRID_W // tb

    def q_map(i):
        return lambda p, bi, t: (bi, p, 1 + n_q * t + i, 0)

    def k_map(i):
        def f(p, bi, t):
            first = jnp.clip(n_q * t - (n_k - n_q) // 2, 0, last_k0)
            return (bi, p, 1 + first + i, 0)
        return f

    def bias_map(p, bi, t):
        kind = jnp.where(t == 0, 0, jnp.where(t == n_tiles - 1, 2, 1))
        return (kind, p, 0, 0)

    tok = lambda f: pl.BlockSpec((1, 1, tb, LANES), f)
    ctx_spec = tok(lambda p, bi, t: (bi, p, 0, 0))
    in_specs = ([tok(q_map(i)) for i in range(n_q)] + [tok(k_map(i)) for i in range(n_k)]
                + [tok(k_map(i)) for i in range(n_k)] + [ctx_spec, ctx_spec]
                + [pl.BlockSpec((1, 2) + bias.shape[2:], bias_map)])
    args = [q] * n_q + [k] * n_k + [v] * n_k + [k, v, bias]
    nq = NA_Q_ROWS * GRID_W
    return pl.pallas_call(
        functools.partial(_na_attn_kernel, n_q=n_q, n_k=n_k),
        grid=(hp, b, n_tiles),
        in_specs=in_specs,
        out_specs=pl.BlockSpec((1, 1, nq, LANES), lambda p, bi, t: (bi, p, t, 0)),
        out_shape=jax.ShapeDtypeStruct((b, hp, seq, LANES), BF16),
        compiler_params=_params("parallel", "parallel", "parallel"),
        name="na_attention",
    )(*args)


def _na_out_kernel(x_ref, o_ref, mod_ref, w_ref, out_ref):
    o = jnp.concatenate([o_ref[0, p] for p in range(o_ref.shape[1])], axis=-1)
    out_ref[0] = x_ref[0] + mod_ref[0][5:6] * _dot(o, w_ref[...])


def _na_out(xs, o, mod, w_o, ctx_rows):
    b, s, d = xs.shape
    hp = o.shape[1]
    seq = s - ctx_rows
    tm = TOKEN_TILE
    skip = ctx_rows // tm
    return pl.pallas_call(
        _na_out_kernel,
        grid=(b, seq // tm),
        in_specs=[pl.BlockSpec((1, tm, d), lambda i, t: (i, t + skip, 0)),
                  pl.BlockSpec((1, hp, tm, LANES), lambda i, t: (i, 0, t, 0)),
                  pl.BlockSpec((1, 6, d), lambda i, t: (i, 0, 0)),
                  _resident((d, d), lambda i, t: (0, 0))],
        out_specs=pl.BlockSpec((1, tm, d), lambda i, t: (i, t, 0)),
        out_shape=jax.ShapeDtypeStruct((b, seq, d), F32),
        compiler_params=_params("parallel", "parallel"),
        name="na_out",
    )(xs, o, mod, w_o)


def _na_mixer(xs, mod, g, w_qkv, q_gain, k_gain, bias, w_o, ctx_rows):
    q, k, v = _na_qkv(xs, mod, g, w_qkv, q_gain, k_gain, ctx_rows)
    o = _na_attention(q, k, v, bias, ctx_rows)
    return _na_out(xs, o, mod, w_o, ctx_rows)


def kernel(x, c, ctx, c_ctx, norm_g, ada_w, ada_b, ffn_w_in, ffn_w_out, ssm_w_in, ssm_lambda_re, ssm_lambda_im, ssm_log_step, ssm_b_re, ssm_b_im, ssm_c_re, ssm_c_im, ssm_d, ssm_w_glu, na_w_qkv, na_q_norm, na_k_norm, na_rpb, na_w_o):
    b, seq, d = x.shape
    ctx_rows = ctx.shape[1]
    depth = norm_g.shape[0]
    assert b + 1 <= 8 and ctx_rows == TOKEN_TILE and depth == 2

    cvec = jnp.zeros((8, d), F32).at[:b].set(c).at[b].set(c_ctx)
    m = _ada_modulation(cvec, ada_w, ada_b).reshape(depth, 8, N_SUB, 3, d)

    def mod_rows(layer, sub):
        lat = m[layer, :b, sub]
        cx = jnp.broadcast_to(m[layer, b, sub], (b, 3, d))
        return jnp.concatenate([cx, lat], axis=1)

    def gain(layer, sub):
        return norm_g[layer, sub].reshape(1, d)

    w_in = ffn_w_in.astype(BF16)
    w_out = ffn_w_out.astype(BF16)
    xs = jnp.concatenate([ctx, x], axis=1)

    xs = _ffn(xs, mod_rows(0, 0), gain(0, 0), w_in[0, 0], w_out[0, 0], ctx_rows)
    n_chunks = (ctx_rows + seq) // S5_CHUNK
    assert n_chunks % SUBLANES == 0 and ctx_rows % S5_CHUNK == 0
    mats = _s5_prep(ssm_lambda_re[0], ssm_lambda_im[0], ssm_log_step[0], ssm_b_re[0], ssm_b_im[0],
                    ssm_c_re[0], ssm_c_im[0], S5_CHUNK, n_chunks // SUBLANES)
    xs = _s5_mixer(xs, mod_rows(0, 1), gain(0, 1), ssm_w_in[0].astype(BF16), mats,
                   ssm_d[0].reshape(1, d), ssm_w_glu[0].astype(BF16), ctx_rows)
    xs = _ffn(xs, mod_rows(0, 2), gain(0, 2), w_in[0, 1], w_out[0, 1], ctx_rows)

    xs = _ffn(xs, mod_rows(1, 0), gain(1, 0), w_in[1, 0], w_out[1, 0], ctx_rows)
    bias = _na_bias(na_rpb[0], seq // GRID_W)
    xl = _na_mixer(xs, mod_rows(1, 1), gain(1, 1), na_w_qkv[0].astype(BF16), na_q_norm[0],
                   na_k_norm[0], bias, na_w_o[0].astype(BF16), ctx_rows)
    return _ffn(xl, mod_rows(1, 2), gain(1, 2), w_in[1, 1], w_out[1, 1], 0)
```

```python
import functools
import math

import jax
import jax.numpy as jnp
from jax import lax
from jax.experimental import pallas as pl
from jax.experimental.pallas import tpu as pltpu

F32 = jnp.float32
BF16 = jnp.bfloat16

GRID_W = 64
N_SUB = 3
MACARON_WEIGHT = 0.5
RMS_EPS = 1e-6
S5_GROUP = 16
S5_STATE = 64
S5_MIN_NEG_RE = -1e-4
NA_HEADS = 16
WIN_H = 8
WIN_W = 16

LANES = 128
SUBLANES = 8
VMEM_LIMIT = 56 * 1024 * 1024

TOKEN_TILE = 256
S5_CHUNK = 8
S5_LANE_GROUPS = LANES // S5_GROUP
NA_Q_ROWS = 8
NA_K_ROWS = 16
LOG2E = math.log2(math.e)


def _dot(a, b):
    return jnp.dot(a, b, preferred_element_type=F32)


def _dot_nt(a, b):
    return lax.dot_general(a, b, (((1,), (1,)), ((), ())), preferred_element_type=F32)


def _dot_nt_f32(a, b):
    return lax.dot_general(a, b, (((1,), (1,)), ((), ())), preferred_element_type=F32,
                           precision=lax.Precision.HIGHEST)


def _params(*sem):
    return pltpu.CompilerParams(dimension_semantics=sem, vmem_limit_bytes=VMEM_LIMIT)


def _resident(shape, index_map):
    return pl.BlockSpec(shape, index_map, pipeline_mode=pl.Buffered(1))


def _ada_kernel(c_ref, w_ref, b_ref, o_ref):
    c = c_ref[...]
    s = c * jax.nn.sigmoid(c)
    o_ref[0] = jnp.dot(s, w_ref[0], preferred_element_type=F32,
                       precision=lax.Precision.HIGHEST) + b_ref[0]


def _ada_modulation(cvec, ada_w, ada_b):
    depth, d, n = ada_w.shape
    tn = d
    return pl.pallas_call(
        _ada_kernel,
        grid=(depth, n // tn),
        in_specs=[pl.BlockSpec((8, d), lambda l, j: (0, 0)),
                  pl.BlockSpec((1, d, tn), lambda l, j: (l, 0, j)),
                  pl.BlockSpec((1, 1, tn), lambda l, j: (l, 0, j))],
        out_specs=pl.BlockSpec((1, 8, tn), lambda l, j: (l, 0, j)),
        out_shape=jax.ShapeDtypeStruct((depth, 8, n), F32),
        compiler_params=_params("parallel", "parallel"),
        name="ada_modulation",
    )(cvec, ada_w, ada_b.reshape(depth, 1, n))


def _row_is_context(tm, first_row, ctx_rows):
    return (first_row + lax.broadcasted_iota(jnp.int32, (tm, 1), 0)) < ctx_rows


def _modulated_norm(x, g, mod, first_row, ctx_rows):
    tm = x.shape[0]
    y = x * lax.rsqrt(jnp.mean(x * x, axis=-1, keepdims=True) + RMS_EPS) * g
    if ctx_rows == 0:
        shift, scale, gate = mod[3:4], mod[4:5], mod[5:6]
    else:
        is_ctx = _row_is_context(tm, first_row, ctx_rows)
        shift = jnp.where(is_ctx, mod[0:1], mod[3:4])
        scale = jnp.where(is_ctx, mod[1:2], mod[4:5])
        gate = jnp.where(is_ctx, mod[2:3], mod[5:6])
    return y * (1.0 + scale) + shift, gate


def _gelu_tanh(y):
    return 0.5 * y * (1.0 + jnp.tanh(math.sqrt(2.0 / math.pi) * (y + 0.044715 * (y * y * y))))


def _stage_kernel(*refs, pre, post, ctx_rows, hidden, head_dim):
    it = iter(refs)
    x_ref = next(it)
    tm, d = x_ref.shape[1], x_ref.shape[2]
    first_row = pl.program_id(1) * tm
    x = x_ref[0]

    def gate_of(mod):
        if ctx_rows == 0:
            return mod[5:6]
        return jnp.where(_row_is_context(tm, first_row, ctx_rows), mod[2:3], mod[5:6])

    if pre == "s5":
        y_ref, pmod_ref, wglu_ref = next(it), next(it), next(it)
        z = _dot(_gelu_tanh(y_ref[0]).astype(BF16), wglu_ref[...])
        x = x + gate_of(pmod_ref[0]) * (z[:, :d] * jax.nn.sigmoid(z[:, d:]))
    elif pre == "na":
        a_ref, pmod_ref, wo_ref = next(it), next(it), next(it)
        a = jnp.concatenate([a_ref[0, p] for p in range(a_ref.shape[1])], axis=-1)
        x = x + gate_of(pmod_ref[0]) * _dot(a, wo_ref[...])

    mod_ref, g_ref, win_ref, wout_ref = next(it), next(it), next(it), next(it)
    h, gate = _modulated_norm(x, g_ref[...], mod_ref[0], first_row, ctx_rows)
    hb = h.astype(BF16)
    gt = _dot(hb, win_ref[:, :hidden])
    up = _dot(hb, win_ref[:, hidden:])
    act = (gt * jax.nn.sigmoid(gt) * up).astype(BF16)
    x = x + (MACARON_WEIGHT * gate) * _dot(act, wout_ref[...])

    if post == "s5":
        qmod_ref, qg_ref, w_ref = next(it), next(it), next(it)
    elif post == "qkv":
        qmod_ref, qg_ref, w_ref, ones_ref, qgain_ref, kgain_ref = (next(it) for _ in range(6))
    o_ref = next(it)
    o_ref[0] = x
    if post is None:
        return
    h2, _ = _modulated_norm(x, qg_ref[...], qmod_ref[0], first_row, ctx_rows)
    proj = _dot(h2.astype(BF16), w_ref[...])
    if post == "s5":
        next(it)[0] = proj
        return

    def head_norm(z, gain):
        ms = _dot((z * z).astype(BF16), ones_ref[...]) * (1.0 / head_dim)
        return z * lax.rsqrt(ms + RMS_EPS) * gain

    q = head_norm(proj[:, :d], qgain_ref[...]) * (head_dim ** -0.5 * LOG2E)
    k = head_norm(proj[:, d:2 * d], kgain_ref[...])
    v = proj[:, 2 * d:]
    q_ref, k_ref, v_ref = next(it), next(it), next(it)
    for hp in range(d // LANES):
        sl = slice(hp * LANES, (hp + 1) * LANES)
        q_ref[0, hp] = q[:, sl].astype(BF16)
        k_ref[0, hp] = k[:, sl].astype(BF16)
        v_ref[0, hp] = v[:, sl].astype(BF16)


def _stage(xs, ffn, *, pre=None, post=None, ctx_rows, name):
    b, s, d = xs.shape
    tm = TOKEN_TILE
    mod, gain, w_in, w_out = ffn
    hidden = w_out.shape[0]
    head_dim = d // NA_HEADS
    skip = 0
    if pre is not None and pre[0] == "na":
        skip = ctx_rows // tm
        s, ctx_rows = s - ctx_rows, 0
    tok = pl.BlockSpec((1, tm, d), lambda i, t: (i, t, 0))
    modspec = pl.BlockSpec((1, 6, d), lambda i, t: (i, 0, 0))
    vec = pl.BlockSpec((1, d), lambda i, t: (0, 0))

    def weight(w):
        return _resident(w.shape, lambda i, t: (0, 0))

    args = [xs]
    in_specs = [pl.BlockSpec((1, tm, d), lambda i, t: (i, t + skip, 0))]
    if pre is not None and pre[0] == "s5":
        _, y, pmod, w_glu = pre
        args += [y, pmod, w_glu]
        in_specs += [tok, modspec, weight(w_glu)]
    elif pre is not None:
        _, attn, pmod, w_o = pre
        args += [attn, pmod, w_o]
        in_specs += [pl.BlockSpec((1, attn.shape[1], tm, LANES), lambda i, t: (i, 0, t, 0)),
                     modspec, weight(w_o)]
    args += [mod, gain, w_in, w_out]
    in_specs += [modspec, vec, weight(w_in), weight(w_out)]
    out_shape = [jax.ShapeDtypeStruct((b, s, d), F32)]
    out_specs = [tok]
    if post is not None and post[0] == "s5":
        _, qmod, qgain, w = post
        args += [qmod, qgain, w]
        in_specs += [modspec, vec, weight(w)]
        out_shape.append(jax.ShapeDtypeStruct((b, s, d), F32))
        out_specs.append(tok)
    elif post is not None:
        _, qmod, qgain, w, q_gain, k_gain = post
        hid = jnp.arange(d) // head_dim
        ones_bd = (hid[:, None] == hid[None, :]).astype(BF16)
        args += [qmod, qgain, w, ones_bd,
                 jnp.tile(q_gain, NA_HEADS).reshape(1, d), jnp.tile(k_gain, NA_HEADS).reshape(1, d)]
        in_specs += [modspec, vec, weight(w), weight(ones_bd), vec, vec]
        hp = d // LANES
        out_shape += [jax.ShapeDtypeStruct((b, hp, s, LANES), BF16)] * 3
        out_specs += [pl.BlockSpec((1, hp, tm, LANES), lambda i, t: (i, 0, t, 0))] * 3
    return pl.pallas_call(
        functools.partial(_stage_kernel, pre=None if pre is None else pre[0],
                          post=None if post is None else post[0],
                          ctx_rows=ctx_rows, hidden=hidden, head_dim=head_dim),
        grid=(b, s // tm),
        in_specs=in_specs,
        out_specs=out_specs,
        out_shape=out_shape,
        compiler_params=_params("parallel", "parallel"),
        name=name,
    )(*args)


def _s5_prep_kernel(lr_ref, li_ref, ls_ref, btr_ref, bti_ref, cr_ref, ci_ref,
                    tpd_ref, wst_ref, prdt_ref, avec_ref, *, chunk, seg_chunks):
    n = LANES
    ng = S5_LANE_GROUPS
    half = n // 2
    wide = ng * half
    lr = jnp.minimum(lr_ref[0], S5_MIN_NEG_RE)
    li = li_ref[0]
    dt = jnp.exp(ls_ref[0])
    mag = jnp.exp(lr * dt)
    ar = mag * jnp.cos(li * dt)
    ai = mag * jnp.sin(li * dt)
    den = lr * lr + li * li
    zr = ((ar - 1.0) * lr + ai * li) / den
    zi = (ai * lr - (ar - 1.0) * li) / den
    btr, bti = btr_ref[0], bti_ref[0]
    bbr = zr * btr - zi * bti
    bbi = zr * bti + zi * btr
    cr, ci = cr_ref[0], ci_ref[0]

    lane = lax.broadcasted_iota(jnp.int32, (n, n), 1)
    row = lax.broadcasted_iota(jnp.int32, (n, n), 0)
    first_half = lane < half
    same_group = (row // S5_GROUP) == (lane // S5_GROUP)
    row_w = lax.broadcasted_iota(jnp.int32, (n, wide), 0)
    lane_w = lax.broadcasted_iota(jnp.int32, (n, wide), 1)
    own_states = (row_w // S5_GROUP) == (lane_w // half)

    def spread(v):
        return jnp.where(own_states, jnp.concatenate([v] * (wide // n), axis=1), 0.0).astype(BF16)

    powers = [(jnp.ones((n, n), F32), jnp.zeros((n, n), F32))]
    for _ in range(chunk):
        er, ei = powers[-1]
        powers.append((er * ar - ei * ai, er * ai + ei * ar))

    def input_map(k):
        er, ei = powers[k]
        return bbr * er - bbi * ei, bbr * ei + bbi * er

    def output_map(k):
        er, ei = powers[k]
        return cr * er - ci * ei, cr * ei + ci * er

    c2 = jnp.where(first_half, cr, -ci)
    lag_blocks = []
    for k in range(chunk):
        wr, wi = input_map(k)
        kd = _dot_nt_f32(jnp.where(first_half, wr, wi), c2)
        lag_blocks.append(jnp.where(same_group, kd, 0.0).astype(BF16))
    zero_block = jnp.zeros((n, n), BF16)

    def emit(reverse):
        for s in range(chunk):
            wr, wi = input_map(s if reverse else chunk - 1 - s)
            wst_ref[0, 0, s * n:(s + 1) * n, 0:wide] = spread(wr)
            wst_ref[0, 0, s * n:(s + 1) * n, wide:2 * wide] = spread(wi)
        for t in range(chunk):
            pr, pi = output_map(chunk - t if reverse else t + 1)
            prdt_ref[0, 0, t * n:(t + 1) * n, 0:wide] = spread(pr)
            prdt_ref[0, 0, t * n:(t + 1) * n, wide:2 * wide] = spread(-pi)
        for s in range(chunk):
            for t in range(chunk):
                lag = s - t if reverse else t - s
                tpd_ref[0, 0, s * n:(s + 1) * n, t * n:(t + 1) * n] = (
                    lag_blocks[lag] if lag >= 0 else zero_block)

    @pl.when(pl.program_id(0) == 0)
    def _():
        emit(False)

    @pl.when(pl.program_id(0) == 1)
    def _():
        emit(True)

    er, ei = powers[chunk]
    sr, si = jnp.ones((n, n), F32), jnp.zeros((n, n), F32)
    pr_, pi_ = er, ei
    e = seg_chunks
    while e:
        if e & 1:
            sr, si = sr * pr_ - si * pi_, sr * pi_ + si * pr_
        pr_, pi_ = pr_ * pr_ - pi_ * pi_, 2.0 * pr_ * pi_
        e >>= 1
    for r, v in enumerate((er, ei, sr, si)):
        for j in range(ng // 2):
            g0 = 2 * j * S5_GROUP
            g1 = g0 + S5_GROUP
            avec_ref[0, 0, r:r + 1, j * n:(j + 1) * n] = jnp.where(
                first_half[0:1], v[g0:g0 + 1], v[g1:g1 + 1])
    avec_ref[0, 0, 4:8, :] = jnp.zeros((4, wide), F32)


def _s5_prep(lam_re, lam_im, log_step, b_re, b_im, c_re, c_im, chunk, seg_chunks):
    nd, g, p = lam_re.shape
    gs = S5_GROUP
    assert 2 * p == LANES and g % S5_LANE_GROUPS == 0
    n_blocks = g // S5_LANE_GROUPS
    tk = chunk * LANES
    wide = S5_LANE_GROUPS * p

    def rows(v):
        v = v.reshape(nd, g * gs, p)
        return jnp.concatenate([v, v], axis=-1)

    def per_group(v):
        return rows(jnp.broadcast_to(v[:, :, None, :], (nd, g, gs, p)))

    args = (per_group(lam_re), per_group(lam_im),
            per_group(jnp.broadcast_to(log_step[:, :, None], (nd, g, p))),
            rows(jnp.swapaxes(b_re, 2, 3)), rows(jnp.swapaxes(b_im, 2, 3)), rows(c_re), rows(c_im))
    sq = pl.BlockSpec((1, LANES, LANES), lambda d, i: (d, i, 0))
    big = pl.BlockSpec((1, 1, tk, tk), lambda d, i: (d, i, 0, 0))
    assert tk == 2 * wide
    return pl.pallas_call(
        functools.partial(_s5_prep_kernel, chunk=chunk, seg_chunks=seg_chunks),
        grid=(nd, n_blocks),
        in_specs=[sq] * 7,
        out_specs=[big, big, big, pl.BlockSpec((1, 1, 8, wide), lambda d, i: (d, i, 0, 0))],
        out_shape=[jax.ShapeDtypeStruct((nd, n_blocks, tk, tk), BF16)] * 3
                  + [jax.ShapeDtypeStruct((nd, n_blocks, 8, wide), F32)],
        compiler_params=_params("parallel", "parallel"),
        name="s5_prep",
    )(*args)


def _s5_scan_kernel(u_ref, dsk_ref, tpd_ref, wst_ref, prdt_ref, avec_ref, y_ref,
                    xcat, sv, *, chunk, n_ctx):
    n = LANES
    n_c = xcat.shape[0]
    nb = sv.shape[0] // 2
    n_seg = SUBLANES
    seg = n_c // n_seg
    n_lat = n_c - n_ctx
    direction = pl.program_id(2)

    @pl.when(direction == 0)
    def _():
        for s in range(chunk):
            xcat[:, s * n:(s + 1) * n] = u_ref[0, pl.ds(s, n_c, stride=chunk), :].astype(BF16)

    def run(reverse):
        wst = wst_ref[0, 0]

        def summarise(dst, src):
            s = _dot(xcat[src, :], wst)
            for j in range(2 * nb):
                sv[j, dst, :] = s[:, j * n:(j + 1) * n]

        if reverse:
            summarise(slice(0, n_lat), slice(n_ctx, n_c))
            summarise(slice(n_lat, n_c), slice(0, n_ctx))
        else:
            summarise(slice(0, n_c), slice(0, n_c))

        def lane_blocks(r):
            return [avec_ref[0, 0, r:r + 1, j * n:(j + 1) * n] for j in range(nb)]

        ar = [jnp.broadcast_to(v, (n_seg, n)) for v in lane_blocks(0)]
        ai = [jnp.broadcast_to(v, (n_seg, n)) for v in lane_blocks(1)]

        def rows_at(i):
            return pl.ds(seg - 1 - i if reverse else i, n_seg, stride=seg)

        def advance(h, rows):
            out_r, out_i = [], []
            for j in range(nb):
                hr, hi = h[j], h[nb + j]
                out_r.append(ar[j] * hr - ai[j] * hi + sv[j, rows, :])
                out_i.append(ar[j] * hi + ai[j] * hr + sv[nb + j, rows, :])
            return tuple(out_r + out_i)

        zero = jnp.zeros((n_seg, n), F32)
        fin = lax.fori_loop(0, seg, lambda i, h: advance(h, rows_at(i)), (zero,) * (2 * nb))

        asr, asi = lane_blocks(2), lane_blocks(3)
        order = range(n_seg - 1, -1, -1) if reverse else range(n_seg)
        h0 = []
        for j in range(nb):
            rows_r, rows_i = [None] * n_seg, [None] * n_seg
            pr = pi = jnp.zeros((1, n), F32)
            prev = None
            for sgm in order:
                if prev is not None:
                    pr, pi = (asr[j] * pr - asi[j] * pi + fin[j][prev:prev + 1],
                              asr[j] * pi + asi[j] * pr + fin[nb + j][prev:prev + 1])
                rows_r[sgm], rows_i[sgm] = pr, pi
                prev = sgm
            h0.append((jnp.concatenate(rows_r, axis=0), jnp.concatenate(rows_i, axis=0)))
        h0 = tuple(v[0] for v in h0) + tuple(v[1] for v in h0)

        def step(i, h):
            rows = rows_at(i)
            nxt = advance(h, rows)
            for j in range(2 * nb):
                sv[j, rows, :] = h[j]
            return nxt

        lax.fori_loop(0, seg, step, h0)

        def entry_states(rows):
            return jnp.concatenate([sv[j, rows, :] for j in range(2 * nb)], axis=1).astype(BF16)

        if reverse:
            h_all = jnp.concatenate([entry_states(slice(n_lat, n_c)), entry_states(slice(0, n_lat))],
                                    axis=0)
        else:
            h_all = entry_states(slice(0, n_c))
        return _dot(xcat[...], tpd_ref[0, 0]) + _dot_nt(h_all, prdt_ref[0, 0])

    @pl.when(direction == 0)
    def _():
        y = run(False)
        for t in range(chunk):
            rows = pl.ds(t, n_c, stride=chunk)
            y_ref[0, rows, :] = y[:, t * n:(t + 1) * n] + u_ref[0, rows, :] * dsk_ref[...]

    @pl.when(direction == 1)
    def _():
        y = run(True)
        for t in range(chunk):
            rows = pl.ds(t, n_c, stride=chunk)
            y_ref[0, rows, :] = y_ref[0, rows, :] + y[:, t * n:(t + 1) * n]


def _s5_scan(u, d_skip, mats, chunk, ctx_rows):
    b, s, d = u.shape
    tpd, wst, prdt, avec = mats
    nd, n_blocks, tk, _ = tpd.shape
    wide = avec.shape[-1]
    n_c = s // chunk

    def mat(shape):
        return pl.BlockSpec((1, 1) + shape, lambda o, i, dr: (dr, o, 0, 0))

    tok = pl.BlockSpec((1, s, LANES), lambda o, i, dr: (i, 0, o))
    return pl.pallas_call(
        functools.partial(_s5_scan_kernel, chunk=chunk, n_ctx=ctx_rows // chunk),
        grid=(n_blocks, b, nd),
        in_specs=[tok, pl.BlockSpec((1, LANES), lambda o, i, dr: (0, o)),
                  mat((tk, tk)), mat((tk, tk)), mat((tk, tk)), mat((8, wide))],
        out_specs=tok,
        out_shape=jax.ShapeDtypeStruct((b, s, d), F32),
        scratch_shapes=[pltpu.VMEM((n_c, tk), BF16),
                        pltpu.VMEM((2 * wide // LANES, n_c, LANES), F32)],
        compiler_params=_params("parallel", "parallel", "arbitrary"),
        name="s5_scan",
    )(u, d_skip, tpd, wst, prdt, avec)


def _window_start(r, rows):
    kh = min(WIN_H, rows)
    return min(max(r - kh // 2, 0), rows - kh)


def _na_tile_geometry(kind, rows):
    n_tiles = rows // NA_Q_ROWS
    tile = {0: 0, 1: 1, 2: n_tiles - 1}[kind]
    q0 = tile * NA_Q_ROWS
    k0 = min(max(q0 - (NA_K_ROWS - NA_Q_ROWS) // 2, 0), rows - NA_K_ROWS)
    return q0, k0


def _na_bias_kernel(rpb_ref, o_ref, *, rows):
    h = pl.program_id(0)
    w = GRID_W
    kh = min(WIN_H, rows)
    ncol = 2 * WIN_W - 1
    nrow = 2 * WIN_H - 1
    cq = lax.broadcasted_iota(jnp.int32, (w, LANES), 0)
    lane = lax.broadcasted_iota(jnp.int32, (w, LANES), 1)
    ck = lane % w
    left = lane < w
    cs = jnp.clip(cq - WIN_W // 2, 0, w - WIN_W)
    col_ok = (ck >= cs) & (ck < cs + WIN_W)
    dc = jnp.clip(ck - cq + WIN_W - 1, 0, ncol - 1)
    neg = jnp.full((w, LANES), -jnp.inf, F32)

    def pair_table(d_left, d_right):
        t = jnp.zeros((w, LANES), F32)
        for j in range(ncol):
            vl = rpb_ref[(h * nrow + d_left) * ncol + j] if d_left is not None else 0.0
            vr = rpb_ref[(h * nrow + d_right) * ncol + j] if d_right is not None else 0.0
            t = jnp.where(dc == j, jnp.where(left, vl, vr), t)
        t = t * LOG2E
        ok = col_ok
        if d_left is None:
            ok = ok & jnp.logical_not(left)
        if d_right is None:
            ok = ok & left
        return jnp.where(ok, t, neg)

    cache = {}
    for kind in range(3):
        q0, k0 = _na_tile_geometry(kind, rows)
        for rq in range(NA_Q_ROWS):
            r = q0 + rq
            rs = _window_start(r, rows)
            for m in range(NA_K_ROWS // 2):
                ds = []
                for kr in (k0 + 2 * m, k0 + 2 * m + 1):
                    ds.append(kr - r + WIN_H - 1 if rs <= kr < rs + kh else None)
                key = tuple(ds)
                if key == (None, None):
                    blk = neg
                else:
                    if key not in cache:
                        cache[key] = pair_table(*key)
                    blk = cache[key]
                o_ref[kind, 0, rq * w:(rq + 1) * w, m * LANES:(m + 1) * LANES] = blk


def _na_bias(rpb, rows):
    nh = rpb.shape[0]
    nq, nk = NA_Q_ROWS * GRID_W, NA_K_ROWS * GRID_W
    return pl.pallas_call(
        functools.partial(_na_bias_kernel, rows=rows),
        grid=(nh,),
        in_specs=[pl.BlockSpec(memory_space=pltpu.SMEM)],
        out_specs=pl.BlockSpec((3, 1, nq, nk), lambda h: (0, h, 0, 0)),
        out_shape=jax.ShapeDtypeStruct((3, nh, nq, nk), F32),
        compiler_params=_params("parallel"),
        name="na_bias",
    )(rpb.reshape(-1))


def _na_attn_kernel(*refs, n_q, n_k):
    q_refs = refs[:n_q]
    k_refs = refs[n_q:n_q + n_k]
    v_refs = refs[n_q + n_k:n_q + 2 * n_k]
    kc_ref, vc_ref, bias_ref, o_ref = refs[n_q + 2 * n_k:]
    q = jnp.concatenate([r[0, 0] for r in q_refs], axis=0)
    k = jnp.concatenate([r[0, 0] for r in k_refs], axis=0)
    v = jnp.concatenate([r[0, 0] for r in v_refs], axis=0)
    kc, vc = kc_ref[0, 0], vc_ref[0, 0]
    half = LANES // 2
    first_head = lax.broadcasted_iota(jnp.int32, q.shape, 1) < half
    q_heads = [jnp.where(first_head, q, jnp.zeros_like(q)),
               jnp.where(first_head, jnp.zeros_like(q), q)]
    ones_v, ones_c = jnp.ones_like(v), jnp.ones_like(vc)
    fh_v = lax.broadcasted_iota(jnp.int32, v.shape, 1) < half
    fh_c = lax.broadcasted_iota(jnp.int32, vc.shape, 1) < half
    v_heads = [jnp.where(fh_v, v, ones_v), jnp.where(fh_v, ones_v, v)]
    vc_heads = [jnp.where(fh_c, vc, ones_c), jnp.where(fh_c, ones_c, vc)]

    scores = [(_dot_nt(qh, k) + bias_ref[0, hh], _dot_nt(qh, kc)) for hh, qh in enumerate(q_heads)]
    outs = []
    for hh in range(2):
        s, sc = scores[hh]
        m = jnp.maximum(jnp.max(s, axis=-1, keepdims=True), jnp.max(sc, axis=-1, keepdims=True))
        p = jnp.exp2(s - m).astype(BF16)
        pc = jnp.exp2(sc - m).astype(BF16)
        o = _dot(p, v_heads[hh]) + _dot(pc, vc_heads[hh])
        denom = o[:, half:half + 1] if hh == 0 else o[:, 0:1]
        outs.append(o / denom)
    o_ref[0, 0] = jnp.where(first_head, outs[0], outs[1]).astype(BF16)


def _na_attention(q, k, v, bias, ctx_rows):
    b, hp, s, _ = q.shape
    tb = TOKEN_TILE
    assert ctx_rows == tb
    seq = s - ctx_rows
    rows = seq // GRID_W
    n_tiles = rows // NA_Q_ROWS
    assert n_tiles >= 3
    n_q = NA_Q_ROWS * GRID_W // tb
    n_k = NA_K_ROWS * GRID_W // tb
    last_k0 = (rows - NA_K_ROWS) * GRID_W // tb

    def q_map(i):
        return lambda p, bi, t: (bi, p, 1 + n_q * t + i, 0)

    def k_map(i):
        def f(p, bi, t):
            first = jnp.clip(n_q * t - (n_k - n_q) // 2, 0, last_k0)
            return (bi, p, 1 + first + i, 0)
        return f

    def bias_map(p, bi, t):
        kind = jnp.where(t == 0, 0, jnp.where(t == n_tiles - 1, 2, 1))
        return (kind, p, 0, 0)

    tok = lambda f: pl.BlockSpec((1, 1, tb, LANES), f)
    ctx_spec = tok(lambda p, bi, t: (bi, p, 0, 0))
    in_specs = ([tok(q_map(i)) for i in range(n_q)] + [tok(k_map(i)) for i in range(n_k)]
                + [tok(k_map(i)) for i in range(n_k)] + [ctx_spec, ctx_spec]
                + [pl.BlockSpec((1, 2) + bias.shape[2:], bias_map)])
    args = [q] * n_q + [k] * n_k + [v] * n_k + [k, v, bias]
    nq = NA_Q_ROWS * GRID_W
    return pl.pallas_call(
        functools.partial(_na_attn_kernel, n_q=n_q, n_k=n_k),
        grid=(hp, b, n_tiles),
        in_specs=in_specs,
        out_specs=pl.BlockSpec((1, 1, nq, LANES), lambda p, bi, t: (bi, p, t, 0)),
        out_shape=jax.ShapeDtypeStruct((b, hp, seq, LANES), BF16),
        compiler_params=_params("parallel", "parallel", "parallel"),
        name="na_attention",
    )(*args)


def kernel(x, c, ctx, c_ctx, norm_g, ada_w, ada_b, ffn_w_in, ffn_w_out, ssm_w_in, ssm_lambda_re, ssm_lambda_im, ssm_log_step, ssm_b_re, ssm_b_im, ssm_c_re, ssm_c_im, ssm_d, ssm_w_glu, na_w_qkv, na_q_norm, na_k_norm, na_rpb, na_w_o):
    b, seq, d = x.shape
    ctx_rows = ctx.shape[1]
    depth = norm_g.shape[0]
    assert b + 1 <= 8 and ctx_rows == TOKEN_TILE and depth == 2

    cvec = jnp.zeros((8, d), F32).at[:b].set(c).at[b].set(c_ctx)
    m = _ada_modulation(cvec, ada_w, ada_b).reshape(depth, 8, N_SUB, 3, d)

    def mod_rows(layer, sub):
        lat = m[layer, :b, sub]
        cx = jnp.broadcast_to(m[layer, b, sub], (b, 3, d))
        return jnp.concatenate([cx, lat], axis=1)

    def gain(layer, sub):
        return norm_g[layer, sub].reshape(1, d)

    w_in = ffn_w_in.astype(BF16)
    w_out = ffn_w_out.astype(BF16)
    xs = jnp.concatenate([ctx, x], axis=1)

    def ffn(layer, which):
        sub = 2 * which
        return mod_rows(layer, sub), gain(layer, sub), w_in[layer, which], w_out[layer, which]

    xs, u = _stage(xs, ffn(0, 0), post=("s5", mod_rows(0, 1), gain(0, 1), ssm_w_in[0].astype(BF16)),
                   ctx_rows=ctx_rows, name="ffn_s5in")
    n_chunks = (ctx_rows + seq) // S5_CHUNK
    assert n_chunks % SUBLANES == 0 and ctx_rows % S5_CHUNK == 0
    mats = _s5_prep(ssm_lambda_re[0], ssm_lambda_im[0], ssm_log_step[0], ssm_b_re[0], ssm_b_im[0],
                    ssm_c_re[0], ssm_c_im[0], S5_CHUNK, n_chunks // SUBLANES)
    y = _s5_scan(u, ssm_d[0].reshape(1, d), mats, S5_CHUNK, ctx_rows)
    xs, = _stage(xs, ffn(0, 1), pre=("s5", y, mod_rows(0, 1), ssm_w_glu[0].astype(BF16)),
                 ctx_rows=ctx_rows, name="s5out_ffn")

    xs, q, k, v = _stage(xs, ffn(1, 0),
                         post=("qkv", mod_rows(1, 1), gain(1, 1), na_w_qkv[0].astype(BF16),
                               na_q_norm[0], na_k_norm[0]),
                         ctx_rows=ctx_rows, name="ffn_qkv")
    bias = _na_bias(na_rpb[0], seq // GRID_W)
    attn = _na_attention(q, k, v, bias, ctx_rows)
    out, = _stage(xs, ffn(1, 1), pre=("na", attn, mod_rows(1, 1), na_w_o[0].astype(BF16)),
                  ctx_rows=ctx_rows, name="naout_ffn")
    return out
```

```python
import functools
import math

import jax
import jax.numpy as jnp
from jax import lax
from jax.experimental import pallas as pl
from jax.experimental.pallas import tpu as pltpu

F32 = jnp.float32
BF16 = jnp.bfloat16

GRID_W = 64
N_SUB = 3
MACARON_WEIGHT = 0.5
RMS_EPS = 1e-6
S5_GROUP = 16
S5_STATE = 64
S5_MIN_NEG_RE = -1e-4
NA_HEADS = 16
WIN_H = 8
WIN_W = 16

LANES = 128
SUBLANES = 8
MXU_DIM = 256
VMEM_LIMIT = 56 * 1024 * 1024

TOKEN_TILE = 256
S5_CHUNK = 8
S5_LANE_GROUPS = LANES // S5_GROUP
NA_Q_ROWS = 8
NA_K_ROWS = 16
LOG2E = math.log2(math.e)


def _dot(a, b):
    return jnp.dot(a, b, preferred_element_type=F32)


def _dot_nt(a, b):
    return lax.dot_general(a, b, (((1,), (1,)), ((), ())), preferred_element_type=F32)


def _dot_nt_f32(a, b):
    return lax.dot_general(a, b, (((1,), (1,)), ((), ())), preferred_element_type=F32,
                           precision=lax.Precision.HIGHEST)


def _split_bf16(x):
    hi = x.astype(BF16)
    return hi, (x - hi.astype(F32)).astype(BF16)


def _params(*sem):
    return pltpu.CompilerParams(dimension_semantics=sem, vmem_limit_bytes=VMEM_LIMIT)


def _resident(shape, index_map):
    return pl.BlockSpec(shape, index_map, pipeline_mode=pl.Buffered(1))


def _ada_kernel(c_ref, w_ref, b_ref, o_ref):
    c = c_ref[...]
    s = c * jax.nn.sigmoid(c)
    o_ref[0] = jnp.dot(s, w_ref[0], preferred_element_type=F32,
                       precision=lax.Precision.HIGHEST) + b_ref[0]


def _ada_modulation(cvec, ada_w, ada_b):
    depth, d, n = ada_w.shape
    tn = d
    return pl.pallas_call(
        _ada_kernel,
        grid=(depth, n // tn),
        in_specs=[pl.BlockSpec((8, d), lambda l, j: (0, 0)),
                  pl.BlockSpec((1, d, tn), lambda l, j: (l, 0, j)),
                  pl.BlockSpec((1, 1, tn), lambda l, j: (l, 0, j))],
        out_specs=pl.BlockSpec((1, 8, tn), lambda l, j: (l, 0, j)),
        out_shape=jax.ShapeDtypeStruct((depth, 8, n), F32),
        compiler_params=_params("parallel", "parallel"),
        name="ada_modulation",
    )(cvec, ada_w, ada_b.reshape(depth, 1, n))


def _row_is_context(tm, first_row, ctx_rows):
    return (first_row + lax.broadcasted_iota(jnp.int32, (tm, 1), 0)) < ctx_rows


def _modulated_norm(x, g, mod, first_row, ctx_rows):
    tm = x.shape[0]
    y = x * lax.rsqrt(jnp.mean(x * x, axis=-1, keepdims=True) + RMS_EPS) * g
    if ctx_rows == 0:
        shift, scale, gate = mod[3:4], mod[4:5], mod[5:6]
    else:
        is_ctx = _row_is_context(tm, first_row, ctx_rows)
        shift = jnp.where(is_ctx, mod[0:1], mod[3:4])
        scale = jnp.where(is_ctx, mod[1:2], mod[4:5])
        gate = jnp.where(is_ctx, mod[2:3], mod[5:6])
    return y * (1.0 + scale) + shift, gate


def _gelu_tanh(y):
    return 0.5 * y * (1.0 + jnp.tanh(math.sqrt(2.0 / math.pi) * (y + 0.044715 * (y * y * y))))


def _stage_kernel(*refs, pre, post, ctx_rows, hidden, head_dim):
    it = iter(refs)
    x_ref = next(it)
    tm, d = x_ref.shape[1], x_ref.shape[2]
    first_row = pl.program_id(1) * tm
    x = x_ref[0]
    if pre == "join":
        x = jnp.where(pl.program_id(1) == 0, next(it)[0], x)

    def gate_of(mod):
        if ctx_rows == 0:
            return mod[5:6]
        return jnp.where(_row_is_context(tm, first_row, ctx_rows), mod[2:3], mod[5:6])

    if pre == "s5":
        y_ref, pmod_ref, wglu_ref = next(it), next(it), next(it)
        z = _dot(_gelu_tanh(y_ref[0]).astype(BF16), wglu_ref[...])
        x = x + gate_of(pmod_ref[0]) * (z[:, :d] * jax.nn.sigmoid(z[:, d:]))
    elif pre == "na":
        a_ref, pmod_ref, wo_ref = next(it), next(it), next(it)
        a = jnp.concatenate([a_ref[0, p] for p in range(a_ref.shape[1])], axis=-1)
        x = x + gate_of(pmod_ref[0]) * _dot(a, wo_ref[...])

    mod_ref, g_ref, win_ref, wout_ref = next(it), next(it), next(it), next(it)
    h, gate = _modulated_norm(x, g_ref[...], mod_ref[0], first_row, ctx_rows)
    hb = h.astype(BF16)
    gt = _dot(hb, win_ref[:, :hidden])
    up = _dot(hb, win_ref[:, hidden:])
    act = (gt * jax.nn.sigmoid(gt) * up).astype(BF16)
    x = x + (MACARON_WEIGHT * gate) * _dot(act, wout_ref[...])

    if post == "s5":
        qmod_ref, qg_ref, w_ref = next(it), next(it), next(it)
    elif post == "qkv":
        (qmod_ref, qg_ref, w_ref, reduce_ref, expand_ref,
         qgain_ref, kgain_ref) = (next(it) for _ in range(7))
    o_ref = next(it)
    o_ref[0] = x
    if post is None:
        return
    h2, _ = _modulated_norm(x, qg_ref[...], qmod_ref[0], first_row, ctx_rows)
    proj = _dot(h2.astype(BF16), w_ref[...])
    if post == "s5":
        next(it)[0] = proj
        return

    def head_norm(z, gain):
        ms = _dot((z * z).astype(BF16), reduce_ref[...]) * (1.0 / head_dim)
        hi, lo = _split_bf16(lax.rsqrt(ms + RMS_EPS))
        return z * _dot(jnp.concatenate([hi, lo], axis=1), expand_ref[...]) * gain

    q = head_norm(proj[:, :d], qgain_ref[...]) * (head_dim ** -0.5 * LOG2E)
    k = head_norm(proj[:, d:2 * d], kgain_ref[...])
    v = proj[:, 2 * d:]
    q_ref, k_ref, v_ref = next(it), next(it), next(it)
    for hp in range(d // LANES):
        sl = slice(hp * LANES, (hp + 1) * LANES)
        q_ref[0, hp] = q[:, sl].astype(BF16)
        k_ref[0, hp] = k[:, sl].astype(BF16)
        v_ref[0, hp] = v[:, sl].astype(BF16)


def _stage(xs, ffn, *, pre=None, post=None, ctx_rows, name):
    b, s, d = xs.shape
    tm = TOKEN_TILE
    mod, gain, w_in, w_out, which = ffn
    hidden = w_out.shape[-2]
    head_dim = d // NA_HEADS
    tok = pl.BlockSpec((1, tm, d), lambda i, t: (i, t, 0))
    modspec = pl.BlockSpec((1, 6, d), lambda i, t: (i, 0, 0))
    vec = pl.BlockSpec((1, d), lambda i, t: (0, 0))

    def weight(w, lead=()):
        return _resident((None,) * len(lead) + w.shape[len(lead):], lambda i, t: lead + (0, 0))

    args = [xs]
    if pre is not None and pre[0] == "join":
        assert ctx_rows == tm
        s += ctx_rows
        args.append(pre[1])
        in_specs = [pl.BlockSpec((1, tm, d), lambda i, t: (i, jnp.maximum(t - 1, 0), 0)),
                    pl.BlockSpec((1, tm, d), lambda i, t: (i, 0, 0))]
    elif pre is not None and pre[0] == "na":
        skip = ctx_rows // tm
        s, ctx_rows = s - ctx_rows, 0
        in_specs = [pl.BlockSpec((1, tm, d), lambda i, t: (i, t + skip, 0))]
    else:
        in_specs = [tok]
    if pre is not None and pre[0] == "join":
        pass
    elif pre is not None and pre[0] == "s5":
        _, y, pmod, w_glu = pre
        args += [y, pmod, w_glu]
        in_specs += [tok, modspec, weight(w_glu)]
    elif pre is not None:
        _, attn, pmod, w_o = pre
        args += [attn, pmod, w_o]
        in_specs += [pl.BlockSpec((1, attn.shape[1], tm, LANES), lambda i, t: (i, 0, t, 0)),
                     modspec, weight(w_o)]
    args += [mod, gain, w_in, w_out]
    in_specs += [modspec, vec, weight(w_in, which), weight(w_out, which)]
    out_shape = [jax.ShapeDtypeStruct((b, s, d), F32)]
    out_specs = [tok]
    if post is not None and post[0] == "s5":
        _, qmod, qgain, w = post
        args += [qmod, qgain, w]
        in_specs += [modspec, vec, weight(w)]
        out_shape.append(jax.ShapeDtypeStruct((b, s, d), F32))
        out_specs.append(tok)
    elif post is not None:
        _, qmod, qgain, w, q_gain, k_gain = post
        hid = jnp.arange(d) // head_dim
        to_head = (hid[:, None] == jnp.arange(LANES)[None, :]).astype(BF16)
        from_head = jnp.concatenate([to_head.T, to_head.T], axis=0)
        args += [qmod, qgain, w, to_head, from_head,
                 jnp.tile(q_gain, NA_HEADS).reshape(1, d), jnp.tile(k_gain, NA_HEADS).reshape(1, d)]
        in_specs += [modspec, vec, weight(w), weight(to_head), weight(from_head), vec, vec]
        hp = d // LANES
        out_shape += [jax.ShapeDtypeStruct((b, hp, s, LANES), BF16)] * 3
        out_specs += [pl.BlockSpec((1, hp, tm, LANES), lambda i, t: (i, 0, t, 0))] * 3
    return pl.pallas_call(
        functools.partial(_stage_kernel, pre=None if pre is None else pre[0],
                          post=None if post is None else post[0],
                          ctx_rows=ctx_rows, hidden=hidden, head_dim=head_dim),
        grid=(b, s // tm),
        in_specs=in_specs,
        out_specs=out_specs,
        out_shape=out_shape,
        compiler_params=_params("parallel", "parallel"),
        name=name,
    )(*args)


def _s5_prep_kernel(lr_ref, li_ref, ls_ref, btr_ref, bti_ref, cr_ref, ci_ref,
                    tpd_ref, wst_ref, prdt_ref, avec_ref, *, chunk, seg_chunks):
    n = LANES
    ng = S5_LANE_GROUPS
    half = n // 2
    wide = ng * half
    lr = jnp.minimum(lr_ref[0], S5_MIN_NEG_RE)
    li = li_ref[0]
    dt = jnp.exp(ls_ref[0])
    mag = jnp.exp(lr * dt)
    ar = mag * jnp.cos(li * dt)
    ai = mag * jnp.sin(li * dt)
    den = lr * lr + li * li
    zr = ((ar - 1.0) * lr + ai * li) / den
    zi = (ai * lr - (ar - 1.0) * li) / den
    btr, bti = btr_ref[0], bti_ref[0]
    bbr = zr * btr - zi * bti
    bbi = zr * bti + zi * btr
    cr, ci = cr_ref[0], ci_ref[0]

    lane = lax.broadcasted_iota(jnp.int32, (n, n), 1)
    row = lax.broadcasted_iota(jnp.int32, (n, n), 0)
    first_half = lane < half
    same_group = (row // S5_GROUP) == (lane // S5_GROUP)
    row_w = lax.broadcasted_iota(jnp.int32, (n, wide), 0)
    lane_w = lax.broadcasted_iota(jnp.int32, (n, wide), 1)
    own_states = (row_w // S5_GROUP) == (lane_w // half)

    def spread(v):
        return jnp.where(own_states, jnp.concatenate([v] * (wide // n), axis=1), 0.0).astype(BF16)

    powers = [(jnp.ones((n, n), F32), jnp.zeros((n, n), F32))]
    for _ in range(chunk):
        er, ei = powers[-1]
        powers.append((er * ar - ei * ai, er * ai + ei * ar))

    def input_map(k):
        er, ei = powers[k]
        return bbr * er - bbi * ei, bbr * ei + bbi * er

    def output_map(k):
        er, ei = powers[k]
        return cr * er - ci * ei, cr * ei + ci * er

    c2 = jnp.where(first_half, cr, -ci)
    lag_blocks = []
    for k in range(chunk):
        wr, wi = input_map(k)
        kd = _dot_nt_f32(jnp.where(first_half, wr, wi), c2)
        lag_blocks.append(jnp.where(same_group, kd, 0.0).astype(BF16))
    zero_block = jnp.zeros((n, n), BF16)

    def emit(reverse):
        for s in range(chunk):
            wr, wi = input_map(s if reverse else chunk - 1 - s)
            wst_ref[0, 0, s * n:(s + 1) * n, 0:wide] = spread(wr)
            wst_ref[0, 0, s * n:(s + 1) * n, wide:2 * wide] = spread(wi)
        for t in range(chunk):
            pr, pi = output_map(chunk - t if reverse else t + 1)
            prdt_ref[0, 0, t * n:(t + 1) * n, 0:wide] = spread(pr)
            prdt_ref[0, 0, t * n:(t + 1) * n, wide:2 * wide] = spread(-pi)
        for s in range(chunk):
            for t in range(chunk):
                lag = s - t if reverse else t - s
                tpd_ref[0, 0, s * n:(s + 1) * n, t * n:(t + 1) * n] = (
                    lag_blocks[lag] if lag >= 0 else zero_block)

    @pl.when(pl.program_id(0) == 0)
    def _():
        emit(False)

    @pl.when(pl.program_id(0) == 1)
    def _():
        emit(True)

    er, ei = powers[chunk]
    sr, si = jnp.ones((n, n), F32), jnp.zeros((n, n), F32)
    pr_, pi_ = er, ei
    e = seg_chunks
    while e:
        if e & 1:
            sr, si = sr * pr_ - si * pi_, sr * pi_ + si * pr_
        pr_, pi_ = pr_ * pr_ - pi_ * pi_, 2.0 * pr_ * pi_
        e >>= 1
    for r, v in enumerate((er, ei, sr, si)):
        for j in range(ng // 2):
            g0 = 2 * j * S5_GROUP
            g1 = g0 + S5_GROUP
            avec_ref[0, 0, r:r + 1, j * n:(j + 1) * n] = jnp.where(
                first_half[0:1], v[g0:g0 + 1], v[g1:g1 + 1])
    avec_ref[0, 0, 4:8, :] = jnp.zeros((4, wide), F32)


def _s5_prep(lam_re, lam_im, log_step, b_re, b_im, c_re, c_im, chunk, seg_chunks):
    nd, g, p = lam_re.shape
    gs = S5_GROUP
    assert 2 * p == LANES and g % S5_LANE_GROUPS == 0
    n_blocks = g // S5_LANE_GROUPS
    tk = chunk * LANES
    wide = S5_LANE_GROUPS * p

    def rows(v):
        v = v.reshape(nd, g * gs, p)
        return jnp.concatenate([v, v], axis=-1)

    def per_group(v):
        return rows(jnp.broadcast_to(v[:, :, None, :], (nd, g, gs, p)))

    args = (per_group(lam_re), per_group(lam_im),
            per_group(jnp.broadcast_to(log_step[:, :, None], (nd, g, p))),
            rows(jnp.swapaxes(b_re, 2, 3)), rows(jnp.swapaxes(b_im, 2, 3)), rows(c_re), rows(c_im))
    sq = pl.BlockSpec((1, LANES, LANES), lambda d, i: (d, i, 0))
    big = pl.BlockSpec((1, 1, tk, tk), lambda d, i: (d, i, 0, 0))
    assert tk == 2 * wide
    return pl.pallas_call(
        functools.partial(_s5_prep_kernel, chunk=chunk, seg_chunks=seg_chunks),
        grid=(nd, n_blocks),
        in_specs=[sq] * 7,
        out_specs=[big, big, big, pl.BlockSpec((1, 1, 8, wide), lambda d, i: (d, i, 0, 0))],
        out_shape=[jax.ShapeDtypeStruct((nd, n_blocks, tk, tk), BF16)] * 3
                  + [jax.ShapeDtypeStruct((nd, n_blocks, 8, wide), F32)],
        compiler_params=_params("parallel", "parallel"),
        name="s5_prep",
    )(*args)


def _s5_scan_kernel(u_ref, dsk_ref, tpd_ref, wst_ref, prdt_ref, avec_ref, y_ref,
                    xcat, sv, *, chunk, n_ctx):
    n = LANES
    n_c = xcat.shape[0]
    nb = sv.shape[0] // 2
    n_seg = SUBLANES
    seg = n_c // n_seg
    n_lat = n_c - n_ctx
    direction = pl.program_id(2)

    @pl.when(direction == 0)
    def _():
        for s in range(chunk):
            xcat[:, s * n:(s + 1) * n] = u_ref[0, pl.ds(s, n_c, stride=chunk), :].astype(BF16)

    def run(reverse):
        wst = wst_ref[0, 0]

        def summarise(dst, src):
            s = _dot(xcat[src, :], wst)
            for j in range(2 * nb):
                sv[j, dst, :] = s[:, j * n:(j + 1) * n]

        if reverse:
            summarise(slice(0, n_lat), slice(n_ctx, n_c))
            summarise(slice(n_lat, n_c), slice(0, n_ctx))
        else:
            summarise(slice(0, n_c), slice(0, n_c))

        def lane_blocks(r):
            return [avec_ref[0, 0, r:r + 1, j * n:(j + 1) * n] for j in range(nb)]

        ar = [jnp.broadcast_to(v, (n_seg, n)) for v in lane_blocks(0)]
        ai = [jnp.broadcast_to(v, (n_seg, n)) for v in lane_blocks(1)]

        def rows_at(i):
            return pl.ds(seg - 1 - i if reverse else i, n_seg, stride=seg)

        def advance(h, rows):
            out_r, out_i = [], []
            for j in range(nb):
                hr, hi = h[j], h[nb + j]
                out_r.append(ar[j] * hr - ai[j] * hi + sv[j, rows, :])
                out_i.append(ar[j] * hi + ai[j] * hr + sv[nb + j, rows, :])
            return tuple(out_r + out_i)

        zero = jnp.zeros((n_seg, n), F32)
        fin = lax.fori_loop(0, seg, lambda i, h: advance(h, rows_at(i)), (zero,) * (2 * nb))

        asr, asi = lane_blocks(2), lane_blocks(3)
        order = range(n_seg - 1, -1, -1) if reverse else range(n_seg)
        h0 = []
        for j in range(nb):
            rows_r, rows_i = [None] * n_seg, [None] * n_seg
            pr = pi = jnp.zeros((1, n), F32)
            prev = None
            for sgm in order:
                if prev is not None:
                    pr, pi = (asr[j] * pr - asi[j] * pi + fin[j][prev:prev + 1],
                              asr[j] * pi + asi[j] * pr + fin[nb + j][prev:prev + 1])
                rows_r[sgm], rows_i[sgm] = pr, pi
                prev = sgm
            h0.append((jnp.concatenate(rows_r, axis=0), jnp.concatenate(rows_i, axis=0)))
        h0 = tuple(v[0] for v in h0) + tuple(v[1] for v in h0)

        def step(i, h):
            rows = rows_at(i)
            nxt = advance(h, rows)
            for j in range(2 * nb):
                sv[j, rows, :] = h[j]
            return nxt

        lax.fori_loop(0, seg, step, h0)

        def entry_states(rows):
            return jnp.concatenate([sv[j, rows, :] for j in range(2 * nb)], axis=1).astype(BF16)

        if reverse:
            h_all = jnp.concatenate([entry_states(slice(n_lat, n_c)), entry_states(slice(0, n_lat))],
                                    axis=0)
        else:
            h_all = entry_states(slice(0, n_c))
        blk = MXU_DIM
        n_blk = xcat.shape[1] // blk
        intra = []
        for j in range(n_blk):
            ks = slice(j * blk, n_blk * blk) if reverse else slice(0, (j + 1) * blk)
            intra.append(_dot(xcat[:, ks], tpd_ref[0, 0, ks, j * blk:(j + 1) * blk]))
        return jnp.concatenate(intra, axis=1) + _dot_nt(h_all, prdt_ref[0, 0])

    @pl.when(direction == 0)
    def _():
        y = run(False)
        for t in range(chunk):
            rows = pl.ds(t, n_c, stride=chunk)
            y_ref[0, rows, :] = y[:, t * n:(t + 1) * n] + u_ref[0, rows, :] * dsk_ref[...]

    @pl.when(direction == 1)
    def _():
        y = run(True)
        for t in range(chunk):
            rows = pl.ds(t, n_c, stride=chunk)
            y_ref[0, rows, :] = y_ref[0, rows, :] + y[:, t * n:(t + 1) * n]


def _s5_scan(u, d_skip, mats, chunk, ctx_rows):
    b, s, d = u.shape
    tpd, wst, prdt, avec = mats
    nd, n_blocks, tk, _ = tpd.shape
    wide = avec.shape[-1]
    n_c = s // chunk

    def mat(shape):
        return pl.BlockSpec((1, 1) + shape, lambda o, i, dr: (dr, o, 0, 0))

    tok = pl.BlockSpec((1, s, LANES), lambda o, i, dr: (i, 0, o))
    return pl.pallas_call(
        functools.partial(_s5_scan_kernel, chunk=chunk, n_ctx=ctx_rows // chunk),
        grid=(n_blocks, b, nd),
        in_specs=[tok, pl.BlockSpec((1, LANES), lambda o, i, dr: (0, o)),
                  mat((tk, tk)), mat((tk, tk)), mat((tk, tk)), mat((8, wide))],
        out_specs=tok,
        out_shape=jax.ShapeDtypeStruct((b, s, d), F32),
        scratch_shapes=[pltpu.VMEM((n_c, tk), BF16),
                        pltpu.VMEM((2 * wide // LANES, n_c, LANES), F32)],
        compiler_params=_params("parallel", "parallel", "arbitrary"),
        name="s5_scan",
    )(u, d_skip, tpd, wst, prdt, avec)


def _window_start(r, rows):
    kh = min(WIN_H, rows)
    return min(max(r - kh // 2, 0), rows - kh)


def _na_tile_geometry(kind, rows):
    n_tiles = rows // NA_Q_ROWS
    tile = {0: 0, 1: 1, 2: n_tiles - 1}[kind]
    q0 = tile * NA_Q_ROWS
    k0 = min(max(q0 - (NA_K_ROWS - NA_Q_ROWS) // 2, 0), rows - NA_K_ROWS)
    return q0, k0


def _na_bias_kernel(rpb_ref, o_ref, *, rows):
    h = pl.program_id(0)
    w = GRID_W
    kh = min(WIN_H, rows)
    ncol = 2 * WIN_W - 1
    nrow = 2 * WIN_H - 1
    cq = lax.broadcasted_iota(jnp.int32, (w, LANES), 0)
    lane = lax.broadcasted_iota(jnp.int32, (w, LANES), 1)
    ck = lane % w
    left = lane < w
    cs = jnp.clip(cq - WIN_W // 2, 0, w - WIN_W)
    col_ok = (ck >= cs) & (ck < cs + WIN_W)
    dc = jnp.clip(ck - cq + WIN_W - 1, 0, ncol - 1)
    neg = jnp.full((w, LANES), -jnp.inf, F32)

    def pair_table(d_left, d_right):
        t = jnp.zeros((w, LANES), F32)
        for j in range(ncol):
            vl = rpb_ref[(h * nrow + d_left) * ncol + j] if d_left is not None else 0.0
            vr = rpb_ref[(h * nrow + d_right) * ncol + j] if d_right is not None else 0.0
            t = jnp.where(dc == j, jnp.where(left, vl, vr), t)
        t = t * LOG2E
        ok = col_ok
        if d_left is None:
            ok = ok & jnp.logical_not(left)
        if d_right is None:
            ok = ok & left
        return jnp.where(ok, t, neg)

    cache = {}
    for kind in range(3):
        q0, k0 = _na_tile_geometry(kind, rows)
        for rq in range(NA_Q_ROWS):
            r = q0 + rq
            rs = _window_start(r, rows)
            for m in range(NA_K_ROWS // 2):
                ds = []
                for kr in (k0 + 2 * m, k0 + 2 * m + 1):
                    ds.append(kr - r + WIN_H - 1 if rs <= kr < rs + kh else None)
                key = tuple(ds)
                if key == (None, None):
                    blk = neg
                else:
                    if key not in cache:
                        cache[key] = pair_table(*key)
                    blk = cache[key]
                o_ref[kind, 0, rq * w:(rq + 1) * w, m * LANES:(m + 1) * LANES] = blk


def _na_bias(rpb, rows):
    nh = rpb.shape[0]
    nq, nk = NA_Q_ROWS * GRID_W, NA_K_ROWS * GRID_W
    return pl.pallas_call(
        functools.partial(_na_bias_kernel, rows=rows),
        grid=(nh,),
        in_specs=[pl.BlockSpec(memory_space=pltpu.SMEM)],
        out_specs=pl.BlockSpec((3, 1, nq, nk), lambda h: (0, h, 0, 0)),
        out_shape=jax.ShapeDtypeStruct((3, nh, nq, nk), F32),
        compiler_params=_params("parallel"),
        name="na_bias",
    )(rpb.reshape(-1))


def _na_attn_kernel(*refs, n_q, n_k):
    q_refs = refs[:n_q]
    k_refs = refs[n_q:n_q + n_k]
    v_refs = refs[n_q + n_k:n_q + 2 * n_k]
    kc_ref, vc_ref, bias_ref, o_ref = refs[n_q + 2 * n_k:]
    q = jnp.concatenate([r[0, 0] for r in q_refs], axis=0)
    k = jnp.concatenate([r[0, 0] for r in k_refs], axis=0)
    v = jnp.concatenate([r[0, 0] for r in v_refs], axis=0)
    kc, vc = kc_ref[0, 0], vc_ref[0, 0]
    half = LANES // 2
    first_head = lax.broadcasted_iota(jnp.int32, q.shape, 1) < half
    q_heads = [jnp.where(first_head, q, jnp.zeros_like(q)),
               jnp.where(first_head, jnp.zeros_like(q), q)]
    ones_v, ones_c = jnp.ones_like(v), jnp.ones_like(vc)
    fh_v = lax.broadcasted_iota(jnp.int32, v.shape, 1) < half
    fh_c = lax.broadcasted_iota(jnp.int32, vc.shape, 1) < half
    v_heads = [jnp.where(fh_v, v, ones_v), jnp.where(fh_v, ones_v, v)]
    vc_heads = [jnp.where(fh_c, vc, ones_c), jnp.where(fh_c, ones_c, vc)]

    scores = [(_dot_nt(qh, k) + bias_ref[0, hh], _dot_nt(qh, kc)) for hh, qh in enumerate(q_heads)]
    outs = []
    for hh in range(2):
        s, sc = scores[hh]
        m = jnp.maximum(jnp.max(s, axis=-1, keepdims=True), jnp.max(sc, axis=-1, keepdims=True))
        p = jnp.exp2(s - m).astype(BF16)
        pc = jnp.exp2(sc - m).astype(BF16)
        o = _dot(p, v_heads[hh]) + _dot(pc, vc_heads[hh])
        denom = o[:, half:half + 1] if hh == 0 else o[:, 0:1]
        outs.append(o / denom)
    o_ref[0, 0] = jnp.where(first_head, outs[0], outs[1]).astype(BF16)


def _na_attention(q, k, v, bias, ctx_rows):
    b, hp, s, _ = q.shape
    tb = TOKEN_TILE
    assert ctx_rows == tb
    seq = s - ctx_rows
    rows = seq // GRID_W
    n_tiles = rows // NA_Q_ROWS
    assert n_tiles >= 3
    n_q = NA_Q_ROWS * GRID_W // tb
    n_k = NA_K_ROWS * GRID_W // tb
    last_k0 = (rows - NA_K_ROWS) * GRID_W // tb

    def q_map(i):
        return lambda p, bi, t: (bi, p, 1 + n_q * t + i, 0)

    def k_map(i):
        def f(p, bi, t):
            first = jnp.clip(n_q * t - (n_k - n_q) // 2, 0, last_k0)
            return (bi, p, 1 + first + i, 0)
        return f

    def bias_map(p, bi, t):
        kind = jnp.where(t == 0, 0, jnp.where(t == n_tiles - 1, 2, 1))
        return (kind, p, 0, 0)

    tok = lambda f: pl.BlockSpec((1, 1, tb, LANES), f)
    ctx_spec = tok(lambda p, bi, t: (bi, p, 0, 0))
    in_specs = ([tok(q_map(i)) for i in range(n_q)] + [tok(k_map(i)) for i in range(n_k)]
                + [tok(k_map(i)) for i in range(n_k)] + [ctx_spec, ctx_spec]
                + [pl.BlockSpec((1, 2) + bias.shape[2:], bias_map)])
    args = [q] * n_q + [k] * n_k + [v] * n_k + [k, v, bias]
    nq = NA_Q_ROWS * GRID_W
    return pl.pallas_call(
        functools.partial(_na_attn_kernel, n_q=n_q, n_k=n_k),
        grid=(hp, b, n_tiles),
        in_specs=in_specs,
        out_specs=pl.BlockSpec((1, 1, nq, LANES), lambda p, bi, t: (bi, p, t, 0)),
        out_shape=jax.ShapeDtypeStruct((b, hp, seq, LANES), BF16),
        compiler_params=_params("parallel", "parallel", "parallel"),
        name="na_attention",
    )(*args)


def kernel(x, c, ctx, c_ctx, norm_g, ada_w, ada_b, ffn_w_in, ffn_w_out, ssm_w_in, ssm_lambda_re, ssm_lambda_im, ssm_log_step, ssm_b_re, ssm_b_im, ssm_c_re, ssm_c_im, ssm_d, ssm_w_glu, na_w_qkv, na_q_norm, na_k_norm, na_rpb, na_w_o):
    b, seq, d = x.shape
    ctx_rows = ctx.shape[1]
    depth = norm_g.shape[0]
    assert b + 1 <= 8 and ctx_rows == TOKEN_TILE and depth == 2

    cvec = jnp.zeros((8, d), F32).at[:b].set(c).at[b].set(c_ctx)
    m = _ada_modulation(cvec, ada_w, ada_b).reshape(depth, 8, N_SUB, 3, d)

    def mod_rows(layer, sub):
        lat = m[layer, :b, sub]
        cx = jnp.broadcast_to(m[layer, b, sub], (b, 3, d))
        return jnp.concatenate([cx, lat], axis=1)

    def gain(layer, sub):
        return norm_g[layer, sub].reshape(1, d)

    w_in = ffn_w_in.astype(BF16)
    w_out = ffn_w_out.astype(BF16)

    def ffn(layer, which):
        sub = 2 * which
        return mod_rows(layer, sub), gain(layer, sub), w_in, w_out, (layer, which)

    xs, u = _stage(x, ffn(0, 0), pre=("join", ctx),
                   post=("s5", mod_rows(0, 1), gain(0, 1), ssm_w_in[0].astype(BF16)),
                   ctx_rows=ctx_rows, name="ffn_s5in")
    n_chunks = (ctx_rows + seq) // S5_CHUNK
    assert n_chunks % SUBLANES == 0 and ctx_rows % S5_CHUNK == 0
    mats = _s5_prep(ssm_lambda_re[0], ssm_lambda_im[0], ssm_log_step[0], ssm_b_re[0], ssm_b_im[0],
                    ssm_c_re[0], ssm_c_im[0], S5_CHUNK, n_chunks // SUBLANES)
    y = _s5_scan(u, ssm_d[0].reshape(1, d), mats, S5_CHUNK, ctx_rows)
    xs, = _stage(xs, ffn(0, 1), pre=("s5", y, mod_rows(0, 1), ssm_w_glu[0].astype(BF16)),
                 ctx_rows=ctx_rows, name="s5out_ffn")

    xs, q, k, v = _stage(xs, ffn(1, 0),
                         post=("qkv", mod_rows(1, 1), gain(1, 1), na_w_qkv[0].astype(BF16),
                               na_q_norm[0], na_k_norm[0]),
                         ctx_rows=ctx_rows, name="ffn_qkv")
    bias = _na_bias(na_rpb[0], seq // GRID_W)
    attn = _na_attention(q, k, v, bias, ctx_rows)
    out, = _stage(xs, ffn(1, 1), pre=("na", attn, mod_rows(1, 1), na_w_o[0].astype(BF16)),
                  ctx_rows=ctx_rows, name="naout_ffn")
    return out
```

```python
import functools
import math

import jax
import jax.numpy as jnp
from jax import lax
from jax.experimental import pallas as pl
from jax.experimental.pallas import tpu as pltpu

F32 = jnp.float32
BF16 = jnp.bfloat16

GRID_W = 64
N_SUB = 3
MACARON_WEIGHT = 0.5
RMS_EPS = 1e-6
S5_GROUP = 16
S5_STATE = 64
S5_MIN_NEG_RE = -1e-4
NA_HEADS = 16
WIN_H = 8
WIN_W = 16

LANES = 128
SUBLANES = 8
MXU_DIM = 256
VMEM_LIMIT = 56 * 1024 * 1024

TOKEN_TILE = 256
S5_CHUNK = 8
S5_LANE_GROUPS = LANES // S5_GROUP
NA_Q_ROWS = 8
NA_K_ROWS = 16
NA_SUB_Q_ROWS = 4
NA_SUB_K_ROWS = 12
LOG2E = math.log2(math.e)


def _dot(a, b):
    return jnp.dot(a, b, preferred_element_type=F32)


def _dot_nt(a, b):
    return lax.dot_general(a, b, (((1,), (1,)), ((), ())), preferred_element_type=F32)


def _dot_nt_f32(a, b):
    return lax.dot_general(a, b, (((1,), (1,)), ((), ())), preferred_element_type=F32,
                           precision=lax.Precision.HIGHEST)


def _split_bf16(x):
    hi = x.astype(BF16)
    return hi, (x - hi.astype(F32)).astype(BF16)


def _params(*sem):
    return pltpu.CompilerParams(dimension_semantics=sem, vmem_limit_bytes=VMEM_LIMIT)


def _resident(shape, index_map):
    return pl.BlockSpec(shape, index_map, pipeline_mode=pl.Buffered(1))


def _ada_kernel(c_ref, w_ref, b_ref, o_ref):
    c = c_ref[...]
    s = c * jax.nn.sigmoid(c)
    o_ref[0] = jnp.dot(s, w_ref[0], preferred_element_type=F32,
                       precision=lax.Precision.HIGHEST) + b_ref[0]


def _ada_modulation(cvec, ada_w, ada_b):
    depth, d, n = ada_w.shape
    tn = d
    return pl.pallas_call(
        _ada_kernel,
        grid=(depth, n // tn),
        in_specs=[pl.BlockSpec((8, d), lambda l, j: (0, 0)),
                  pl.BlockSpec((1, d, tn), lambda l, j: (l, 0, j)),
                  pl.BlockSpec((1, 1, tn), lambda l, j: (l, 0, j))],
        out_specs=pl.BlockSpec((1, 8, tn), lambda l, j: (l, 0, j)),
        out_shape=jax.ShapeDtypeStruct((depth, 8, n), F32),
        compiler_params=_params("parallel", "parallel"),
        name="ada_modulation",
    )(cvec, ada_w, ada_b.reshape(depth, 1, n))


def _row_is_context(tm, first_row, ctx_rows):
    return (first_row + lax.broadcasted_iota(jnp.int32, (tm, 1), 0)) < ctx_rows


def _modulated_norm(x, g, mod, first_row, ctx_rows):
    tm = x.shape[0]
    y = x * lax.rsqrt(jnp.mean(x * x, axis=-1, keepdims=True) + RMS_EPS) * g
    if ctx_rows == 0:
        shift, scale, gate = mod[3:4], mod[4:5], mod[5:6]
    else:
        is_ctx = _row_is_context(tm, first_row, ctx_rows)
        shift = jnp.where(is_ctx, mod[0:1], mod[3:4])
        scale = jnp.where(is_ctx, mod[1:2], mod[4:5])
        gate = jnp.where(is_ctx, mod[2:3], mod[5:6])
    return y * (1.0 + scale) + shift, gate


def _gelu_tanh(y):
    return 0.5 * y * (1.0 + jnp.tanh(math.sqrt(2.0 / math.pi) * (y + 0.044715 * (y * y * y))))


def _stage_kernel(*refs, pre, post, ctx_rows, hidden, head_dim):
    it = iter(refs)
    x_ref = next(it)
    tm, d = x_ref.shape[1], x_ref.shape[2]
    first_row = pl.program_id(1) * tm
    x = x_ref[0]
    if pre == "join":
        x = jnp.where(pl.program_id(1) == 0, next(it)[0], x)

    def gate_of(mod):
        if ctx_rows == 0:
            return mod[5:6]
        return jnp.where(_row_is_context(tm, first_row, ctx_rows), mod[2:3], mod[5:6])

    if pre == "s5":
        y_ref, pmod_ref, wglu_ref = next(it), next(it), next(it)
        z = _dot(_gelu_tanh(y_ref[0]).astype(BF16), wglu_ref[...])
        x = x + gate_of(pmod_ref[0]) * (z[:, :d] * jax.nn.sigmoid(z[:, d:]))
    elif pre == "na":
        a_ref, pmod_ref, wo_ref = next(it), next(it), next(it)
        a = jnp.concatenate([a_ref[0, p] for p in range(a_ref.shape[1])], axis=-1)
        x = x + gate_of(pmod_ref[0]) * _dot(a, wo_ref[...])

    mod_ref, g_ref, win_ref, wout_ref = next(it), next(it), next(it), next(it)
    h, gate = _modulated_norm(x, g_ref[...], mod_ref[0], first_row, ctx_rows)
    hb = h.astype(BF16)
    gt = _dot(hb, win_ref[:, :hidden])
    up = _dot(hb, win_ref[:, hidden:])
    act = (gt * jax.nn.sigmoid(gt) * up).astype(BF16)
    x = x + (MACARON_WEIGHT * gate) * _dot(act, wout_ref[...])

    if post == "s5":
        qmod_ref, qg_ref, w_ref = next(it), next(it), next(it)
    elif post == "qkv":
        (qmod_ref, qg_ref, w_ref, reduce_ref, expand_ref,
         qgain_ref, kgain_ref) = (next(it) for _ in range(7))
    o_ref = next(it)
    o_ref[0] = x
    if post is None:
        return
    h2, _ = _modulated_norm(x, qg_ref[...], qmod_ref[0], first_row, ctx_rows)
    proj = _dot(h2.astype(BF16), w_ref[...])
    if post == "s5":
        next(it)[0] = proj
        return

    def head_norm(z, gain):
        ms = _dot((z * z).astype(BF16), reduce_ref[...]) * (1.0 / head_dim)
        hi, lo = _split_bf16(lax.rsqrt(ms + RMS_EPS))
        return z * _dot(jnp.concatenate([hi, lo], axis=1), expand_ref[...]) * gain

    q = head_norm(proj[:, :d], qgain_ref[...]) * (head_dim ** -0.5 * LOG2E)
    k = head_norm(proj[:, d:2 * d], kgain_ref[...])
    v = proj[:, 2 * d:]
    q_ref, kv_ref = next(it), next(it)
    for hp in range(d // LANES):
        sl = slice(hp * LANES, (hp + 1) * LANES)
        q_ref[0, hp] = q[:, sl].astype(BF16)
        kv_ref[0, hp, :, :LANES] = k[:, sl].astype(BF16)
        kv_ref[0, hp, :, LANES:] = v[:, sl].astype(BF16)


def _stage(xs, ffn, *, pre=None, post=None, ctx_rows, name):
    b, s, d = xs.shape
    tm = TOKEN_TILE
    mod, gain, w_in, w_out, which = ffn
    hidden = w_out.shape[-2]
    head_dim = d // NA_HEADS
    tok = pl.BlockSpec((1, tm, d), lambda i, t: (i, t, 0))
    modspec = pl.BlockSpec((1, 6, d), lambda i, t: (i, 0, 0))
    vec = pl.BlockSpec((1, d), lambda i, t: (0, 0))

    def weight(w, lead=()):
        return _resident((None,) * len(lead) + w.shape[len(lead):], lambda i, t: lead + (0, 0))

    args = [xs]
    if pre is not None and pre[0] == "join":
        assert ctx_rows == tm
        s += ctx_rows
        args.append(pre[1])
        in_specs = [pl.BlockSpec((1, tm, d), lambda i, t: (i, jnp.maximum(t - 1, 0), 0)),
                    pl.BlockSpec((1, tm, d), lambda i, t: (i, 0, 0))]
    elif pre is not None and pre[0] == "na":
        skip = ctx_rows // tm
        s, ctx_rows = s - ctx_rows, 0
        in_specs = [pl.BlockSpec((1, tm, d), lambda i, t: (i, t + skip, 0))]
    else:
        in_specs = [tok]
    if pre is not None and pre[0] == "join":
        pass
    elif pre is not None and pre[0] == "s5":
        _, y, pmod, w_glu = pre
        args += [y, pmod, w_glu]
        in_specs += [tok, modspec, weight(w_glu)]
    elif pre is not None:
        _, attn, pmod, w_o = pre
        args += [attn, pmod, w_o]
        in_specs += [pl.BlockSpec((1, attn.shape[1], tm, LANES), lambda i, t: (i, 0, t, 0)),
                     modspec, weight(w_o)]
    args += [mod, gain, w_in, w_out]
    in_specs += [modspec, vec, weight(w_in, which), weight(w_out, which)]
    out_shape = [jax.ShapeDtypeStruct((b, s, d), F32)]
    out_specs = [tok]
    if post is not None and post[0] == "s5":
        _, qmod, qgain, w = post
        args += [qmod, qgain, w]
        in_specs += [modspec, vec, weight(w)]
        out_shape.append(jax.ShapeDtypeStruct((b, s, d), F32))
        out_specs.append(tok)
    elif post is not None:
        _, qmod, qgain, w, q_gain, k_gain = post
        hid = jnp.arange(d) // head_dim
        to_head = (hid[:, None] == jnp.arange(LANES)[None, :]).astype(BF16)
        from_head = jnp.concatenate([to_head.T, to_head.T], axis=0)
        args += [qmod, qgain, w, to_head, from_head,
                 jnp.tile(q_gain, NA_HEADS).reshape(1, d), jnp.tile(k_gain, NA_HEADS).reshape(1, d)]
        in_specs += [modspec, vec, weight(w), weight(to_head), weight(from_head), vec, vec]
        hp = d // LANES
        assert ctx_rows == tm
        out_shape += [jax.ShapeDtypeStruct((b, hp, s - ctx_rows, LANES), BF16),
                      jax.ShapeDtypeStruct((b, hp, s, 2 * LANES), BF16)]
        out_specs += [pl.BlockSpec((1, hp, tm, LANES), lambda i, t: (i, 0, jnp.maximum(t - 1, 0), 0)),
                      pl.BlockSpec((1, hp, tm, 2 * LANES), lambda i, t: (i, 0, t, 0))]
    return pl.pallas_call(
        functools.partial(_stage_kernel, pre=None if pre is None else pre[0],
                          post=None if post is None else post[0],
                          ctx_rows=ctx_rows, hidden=hidden, head_dim=head_dim),
        grid=(b, s // tm),
        in_specs=in_specs,
        out_specs=out_specs,
        out_shape=out_shape,
        compiler_params=_params("parallel", "arbitrary"),
        name=name,
    )(*args)


def _s5_prep_kernel(lr_ref, li_ref, ls_ref, btr_ref, bti_ref, cr_ref, ci_ref,
                    tpd_ref, wst_ref, prdt_ref, avec_ref, *, chunk, seg_chunks):
    n = LANES
    ng = S5_LANE_GROUPS
    half = n // 2
    wide = ng * half
    lr = jnp.minimum(lr_ref[0], S5_MIN_NEG_RE)
    li = li_ref[0]
    dt = jnp.exp(ls_ref[0])
    mag = jnp.exp(lr * dt)
    ar = mag * jnp.cos(li * dt)
    ai = mag * jnp.sin(li * dt)
    den = lr * lr + li * li
    zr = ((ar - 1.0) * lr + ai * li) / den
    zi = (ai * lr - (ar - 1.0) * li) / den
    btr, bti = btr_ref[0], bti_ref[0]
    bbr = zr * btr - zi * bti
    bbi = zr * bti + zi * btr
    cr, ci = cr_ref[0], ci_ref[0]

    lane = lax.broadcasted_iota(jnp.int32, (n, n), 1)
    row = lax.broadcasted_iota(jnp.int32, (n, n), 0)
    first_half = lane < half
    same_group = (row // S5_GROUP) == (lane // S5_GROUP)
    row_w = lax.broadcasted_iota(jnp.int32, (n, wide), 0)
    lane_w = lax.broadcasted_iota(jnp.int32, (n, wide), 1)
    own_states = (row_w // S5_GROUP) == (lane_w // half)

    def spread(v):
        return jnp.where(own_states, jnp.concatenate([v] * (wide // n), axis=1), 0.0).astype(BF16)

    powers = [(jnp.ones((n, n), F32), jnp.zeros((n, n), F32))]
    for _ in range(chunk):
        er, ei = powers[-1]
        powers.append((er * ar - ei * ai, er * ai + ei * ar))

    def input_map(k):
        er, ei = powers[k]
        return bbr * er - bbi * ei, bbr * ei + bbi * er

    def output_map(k):
        er, ei = powers[k]
        return cr * er - ci * ei, cr * ei + ci * er

    c2 = jnp.where(first_half, cr, -ci)
    lag_blocks = []
    for k in range(chunk):
        wr, wi = input_map(k)
        kd = _dot_nt_f32(jnp.where(first_half, wr, wi), c2)
        lag_blocks.append(jnp.where(same_group, kd, 0.0).astype(BF16))
    zero_block = jnp.zeros((n, n), BF16)

    def emit(reverse):
        for s in range(chunk):
            wr, wi = input_map(s if reverse else chunk - 1 - s)
            wst_ref[0, 0, s * n:(s + 1) * n, 0:wide] = spread(wr)
            wst_ref[0, 0, s * n:(s + 1) * n, wide:2 * wide] = spread(wi)
        for t in range(chunk):
            pr, pi = output_map(chunk - t if reverse else t + 1)
            prdt_ref[0, 0, t * n:(t + 1) * n, 0:wide] = spread(pr)
            prdt_ref[0, 0, t * n:(t + 1) * n, wide:2 * wide] = spread(-pi)
        for s in range(chunk):
            for t in range(chunk):
                lag = s - t if reverse else t - s
                tpd_ref[0, 0, s * n:(s + 1) * n, t * n:(t + 1) * n] = (
                    lag_blocks[lag] if lag >= 0 else zero_block)

    @pl.when(pl.program_id(0) == 0)
    def _():
        emit(False)

    @pl.when(pl.program_id(0) == 1)
    def _():
        emit(True)

    er, ei = powers[chunk]
    sr, si = jnp.ones((n, n), F32), jnp.zeros((n, n), F32)
    pr_, pi_ = er, ei
    e = seg_chunks
    while e:
        if e & 1:
            sr, si = sr * pr_ - si * pi_, sr * pi_ + si * pr_
        pr_, pi_ = pr_ * pr_ - pi_ * pi_, 2.0 * pr_ * pi_
        e >>= 1
    for r, v in enumerate((er, ei, sr, si)):
        for j in range(ng // 2):
            g0 = 2 * j * S5_GROUP
            g1 = g0 + S5_GROUP
            avec_ref[0, 0, r:r + 1, j * n:(j + 1) * n] = jnp.where(
                first_half[0:1], v[g0:g0 + 1], v[g1:g1 + 1])
    avec_ref[0, 0, 4:8, :] = jnp.zeros((4, wide), F32)


def _s5_prep(lam_re, lam_im, log_step, b_re, b_im, c_re, c_im, chunk, seg_chunks):
    nd, g, p = lam_re.shape
    gs = S5_GROUP
    assert 2 * p == LANES and g % S5_LANE_GROUPS == 0
    n_blocks = g // S5_LANE_GROUPS
    tk = chunk * LANES
    wide = S5_LANE_GROUPS * p

    def rows(v):
        v = v.reshape(nd, g * gs, p)
        return jnp.concatenate([v, v], axis=-1)

    def per_group(v):
        return rows(jnp.broadcast_to(v[:, :, None, :], (nd, g, gs, p)))

    args = (per_group(lam_re), per_group(lam_im),
            per_group(jnp.broadcast_to(log_step[:, :, None], (nd, g, p))),
            rows(jnp.swapaxes(b_re, 2, 3)), rows(jnp.swapaxes(b_im, 2, 3)), rows(c_re), rows(c_im))
    sq = pl.BlockSpec((1, LANES, LANES), lambda d, i: (d, i, 0))
    big = pl.BlockSpec((1, 1, tk, tk), lambda d, i: (d, i, 0, 0))
    assert tk == 2 * wide
    return pl.pallas_call(
        functools.partial(_s5_prep_kernel, chunk=chunk, seg_chunks=seg_chunks),
        grid=(nd, n_blocks),
        in_specs=[sq] * 7,
        out_specs=[big, big, big, pl.BlockSpec((1, 1, 8, wide), lambda d, i: (d, i, 0, 0))],
        out_shape=[jax.ShapeDtypeStruct((nd, n_blocks, tk, tk), BF16)] * 3
                  + [jax.ShapeDtypeStruct((nd, n_blocks, 8, wide), F32)],
        compiler_params=_params("parallel", "parallel"),
        name="s5_prep",
    )(*args)


def _s5_scan_kernel(u_ref, dsk_ref, tpd_ref, wst_ref, prdt_ref, avec_ref, y_ref,
                    xcat, sv, *, chunk, n_ctx):
    n = LANES
    n_c = xcat.shape[0]
    nb = sv.shape[0] // 2
    n_seg = SUBLANES
    seg = n_c // n_seg
    n_lat = n_c - n_ctx
    direction = pl.program_id(2)

    @pl.when(direction == 0)
    def _():
        for s in range(chunk):
            xcat[:, s * n:(s + 1) * n] = u_ref[0, pl.ds(s, n_c, stride=chunk), :].astype(BF16)

    def run(reverse):
        wst = wst_ref[0, 0]

        def summarise(dst, src):
            s = _dot(xcat[src, :], wst)
            for j in range(2 * nb):
                sv[j, dst, :] = s[:, j * n:(j + 1) * n]

        if reverse:
            summarise(slice(0, n_lat), slice(n_ctx, n_c))
            summarise(slice(n_lat, n_c), slice(0, n_ctx))
        else:
            summarise(slice(0, n_c), slice(0, n_c))

        def lane_blocks(r):
            return [avec_ref[0, 0, r:r + 1, j * n:(j + 1) * n] for j in range(nb)]

        ar = [jnp.broadcast_to(v, (n_seg, n)) for v in lane_blocks(0)]
        ai = [jnp.broadcast_to(v, (n_seg, n)) for v in lane_blocks(1)]

        def rows_at(i):
            return pl.ds(seg - 1 - i if reverse else i, n_seg, stride=seg)

        def advance(h, rows):
            out_r, out_i = [], []
            for j in range(nb):
                hr, hi = h[j], h[nb + j]
                out_r.append(ar[j] * hr - ai[j] * hi + sv[j, rows, :])
                out_i.append(ar[j] * hi + ai[j] * hr + sv[nb + j, rows, :])
            return tuple(out_r + out_i)

        zero = jnp.zeros((n_seg, n), F32)
        fin = lax.fori_loop(0, seg, lambda i, h: advance(h, rows_at(i)), (zero,) * (2 * nb))

        asr, asi = lane_blocks(2), lane_blocks(3)
        order = range(n_seg - 1, -1, -1) if reverse else range(n_seg)
        h0 = []
        for j in range(nb):
            rows_r, rows_i = [None] * n_seg, [None] * n_seg
            pr = pi = jnp.zeros((1, n), F32)
            prev = None
            for sgm in order:
                if prev is not None:
                    pr, pi = (asr[j] * pr - asi[j] * pi + fin[j][prev:prev + 1],
                              asr[j] * pi + asi[j] * pr + fin[nb + j][prev:prev + 1])
                rows_r[sgm], rows_i[sgm] = pr, pi
                prev = sgm
            h0.append((jnp.concatenate(rows_r, axis=0), jnp.concatenate(rows_i, axis=0)))
        h0 = tuple(v[0] for v in h0) + tuple(v[1] for v in h0)

        def step(i, h):
            rows = rows_at(i)
            nxt = advance(h, rows)
            for j in range(2 * nb):
                sv[j, rows, :] = h[j]
            return nxt

        lax.fori_loop(0, seg, step, h0)

        def entry_states(rows):
            return jnp.concatenate([sv[j, rows, :] for j in range(2 * nb)], axis=1).astype(BF16)

        if reverse:
            h_all = jnp.concatenate([entry_states(slice(n_lat, n_c)), entry_states(slice(0, n_lat))],
                                    axis=0)
        else:
            h_all = entry_states(slice(0, n_c))
        blk = MXU_DIM
        n_blk = xcat.shape[1] // blk
        intra = []
        for j in range(n_blk):
            ks = slice(j * blk, n_blk * blk) if reverse else slice(0, (j + 1) * blk)
            intra.append(_dot(xcat[:, ks], tpd_ref[0, 0, ks, j * blk:(j + 1) * blk]))
        return jnp.concatenate(intra, axis=1) + _dot_nt(h_all, prdt_ref[0, 0])

    @pl.when(direction == 0)
    def _():
        y = run(False)
        for t in range(chunk):
            rows = pl.ds(t, n_c, stride=chunk)
            y_ref[0, rows, :] = y[:, t * n:(t + 1) * n] + u_ref[0, rows, :] * dsk_ref[...]

    @pl.when(direction == 1)
    def _():
        y = run(True)
        for t in range(chunk):
            rows = pl.ds(t, n_c, stride=chunk)
            y_ref[0, rows, :] = y_ref[0, rows, :] + y[:, t * n:(t + 1) * n]


def _s5_scan(u, d_skip, mats, chunk, ctx_rows):
    b, s, d = u.shape
    tpd, wst, prdt, avec = mats
    nd, n_blocks, tk, _ = tpd.shape
    wide = avec.shape[-1]
    n_c = s // chunk

    def mat(shape):
        return pl.BlockSpec((1, 1) + shape, lambda o, i, dr: (dr, o, 0, 0))

    tok = pl.BlockSpec((1, s, LANES), lambda o, i, dr: (i, 0, o))
    return pl.pallas_call(
        functools.partial(_s5_scan_kernel, chunk=chunk, n_ctx=ctx_rows // chunk),
        grid=(n_blocks, b, nd),
        in_specs=[tok, pl.BlockSpec((1, LANES), lambda o, i, dr: (0, o)),
                  mat((tk, tk)), mat((tk, tk)), mat((tk, tk)), mat((8, wide))],
        out_specs=tok,
        out_shape=jax.ShapeDtypeStruct((b, s, d), F32),
        scratch_shapes=[pltpu.VMEM((n_c, tk), BF16),
                        pltpu.VMEM((2 * wide // LANES, n_c, LANES), F32)],
        compiler_params=_params("parallel", "parallel", "arbitrary"),
        name="s5_scan",
    )(u, d_skip, tpd, wst, prdt, avec)


def _window_start(r, rows):
    kh = min(WIN_H, rows)
    return min(max(r - kh // 2, 0), rows - kh)


def _na_tile_geometry(kind, rows):
    n_tiles = rows // NA_Q_ROWS
    tile = {0: 0, 1: 1, 2: n_tiles - 1}[kind]
    q0 = tile * NA_Q_ROWS
    k0 = min(max(q0 - (NA_K_ROWS - NA_Q_ROWS) // 2, 0), rows - NA_K_ROWS)
    return q0, k0


def _na_sub_window(kind, sub, rows):
    q0, k0 = _na_tile_geometry(kind, rows)
    kh = min(WIN_H, rows)
    starts = [_window_start(q0 + NA_SUB_Q_ROWS * sub + i, rows) - k0 for i in range(NA_SUB_Q_ROWS)]
    first = min(min(starts) // NA_SUB_Q_ROWS * NA_SUB_Q_ROWS, NA_K_ROWS - NA_SUB_K_ROWS)
    assert first >= 0 and max(starts) + kh <= first + NA_SUB_K_ROWS
    return first


def _na_bias_kernel(rpb_ref, o_ref, *, rows):
    h = pl.program_id(0)
    w = GRID_W
    kh = min(WIN_H, rows)
    ncol = 2 * WIN_W - 1
    nrow = 2 * WIN_H - 1
    cq = lax.broadcasted_iota(jnp.int32, (w, LANES), 0)
    lane = lax.broadcasted_iota(jnp.int32, (w, LANES), 1)
    ck = lane % w
    left = lane < w
    cs = jnp.clip(cq - WIN_W // 2, 0, w - WIN_W)
    col_ok = (ck >= cs) & (ck < cs + WIN_W)
    dc = jnp.clip(ck - cq + WIN_W - 1, 0, ncol - 1)
    neg = jnp.full((w, LANES), -jnp.inf, F32)

    def pair_table(d_left, d_right):
        t = jnp.zeros((w, LANES), F32)
        for j in range(ncol):
            vl = rpb_ref[(h * nrow + d_left) * ncol + j] if d_left is not None else 0.0
            vr = rpb_ref[(h * nrow + d_right) * ncol + j] if d_right is not None else 0.0
            t = jnp.where(dc == j, jnp.where(left, vl, vr), t)
        t = t * LOG2E
        ok = col_ok
        if d_left is None:
            ok = ok & jnp.logical_not(left)
        if d_right is None:
            ok = ok & left
        return jnp.where(ok, t, neg)

    cache = {}
    for kind in range(3):
        q0, k0 = _na_tile_geometry(kind, rows)
        for rq in range(NA_Q_ROWS):
            r = q0 + rq
            rs = _window_start(r, rows)
            first = k0 + _na_sub_window(kind, rq // NA_SUB_Q_ROWS, rows)
            for m in range(NA_SUB_K_ROWS // 2):
                ds = []
                for kr in (first + 2 * m, first + 2 * m + 1):
                    ds.append(kr - r + WIN_H - 1 if rs <= kr < rs + kh else None)
                key = tuple(ds)
                if key == (None, None):
                    blk = neg
                else:
                    if key not in cache:
                        cache[key] = pair_table(*key)
                    blk = cache[key]
                o_ref[kind, 0, rq * w:(rq + 1) * w, m * LANES:(m + 1) * LANES] = blk


def _na_bias(rpb, rows):
    nh = rpb.shape[0]
    nq, nk = NA_Q_ROWS * GRID_W, NA_SUB_K_ROWS * GRID_W
    return pl.pallas_call(
        functools.partial(_na_bias_kernel, rows=rows),
        grid=(nh,),
        in_specs=[pl.BlockSpec(memory_space=pltpu.SMEM)],
        out_specs=pl.BlockSpec((3, 1, nq, nk), lambda h: (0, h, 0, 0)),
        out_shape=jax.ShapeDtypeStruct((3, nh, nq, nk), F32),
        compiler_params=_params("parallel"),
        name="na_bias",
    )(rpb.reshape(-1))


def _na_attn_kernel(*refs, n_q):
    q_ref = refs[0]
    kv_refs = refs[1:1 + n_q]
    kvc_ref, bias_ref, o_ref = refs[1 + n_q:]
    tb = kvc_ref.shape[2]
    half = LANES // 2

    def per_head(x, other):
        first_head = lax.broadcasted_iota(jnp.int32, x.shape, 1) < half
        fill = jnp.full_like(x, other)
        return [jnp.where(first_head, x, fill), jnp.where(first_head, fill, x)]

    kc = kvc_ref[0, 0, :, :LANES]
    vc_heads = per_head(kvc_ref[0, 0, :, LANES:], 1.0)
    units = [(sub, hh) for sub in range(n_q) for hh in range(2)]
    keys, values, queries = {}, {}, {}
    for sub in range(n_q):
        keys[sub] = kv_refs[sub][0, 0, :, :LANES]
        values[sub] = per_head(kv_refs[sub][0, 0, :, LANES:], 1.0)
        queries[sub] = per_head(q_ref[0, 0, sub * tb:(sub + 1) * tb, :], 0.0)

    def score(unit):
        sub, hh = unit
        qh = queries[sub][hh]
        return (_dot_nt(qh, keys[sub]) + bias_ref[0, hh, sub * tb:(sub + 1) * tb, :],
                _dot_nt(qh, kc))

    ahead = 2
    pending = [score(u) for u in units[:ahead]]
    outs = {}
    for i, (sub, hh) in enumerate(units):
        s, sc = pending.pop(0)
        if i + ahead < len(units):
            pending.append(score(units[i + ahead]))
        m = jnp.maximum(jnp.max(s, axis=-1, keepdims=True), jnp.max(sc, axis=-1, keepdims=True))
        p = jnp.exp2(s - m).astype(BF16)
        pc = jnp.exp2(sc - m).astype(BF16)
        o = _dot(p, values[sub][hh]) + _dot(pc, vc_heads[hh])
        denom = o[:, half:half + 1] if hh == 0 else o[:, 0:1]
        outs[sub, hh] = o / denom
    first_head = lax.broadcasted_iota(jnp.int32, (tb, LANES), 1) < half
    for sub in range(n_q):
        o_ref[0, 0, sub * tb:(sub + 1) * tb, :] = jnp.where(
            first_head, outs[sub, 0], outs[sub, 1]).astype(BF16)


def _na_attention(q, kv, bias, ctx_rows):
    b, hp, seq, _ = q.shape
    tb = TOKEN_TILE
    assert ctx_rows == tb and kv.shape[2] == ctx_rows + seq
    rows = seq // GRID_W
    n_tiles = rows // NA_Q_ROWS
    assert n_tiles >= 3
    assert NA_SUB_Q_ROWS * GRID_W == tb
    n_q = NA_Q_ROWS // NA_SUB_Q_ROWS
    n_k = NA_SUB_K_ROWS * GRID_W // tb
    slab = NA_K_ROWS * GRID_W // tb
    last_k0 = (rows - NA_K_ROWS) * GRID_W // tb

    def tile_kind(t):
        return jnp.where(t == 0, 0, jnp.where(t == n_tiles - 1, 2, 1))

    def window_map(sub):
        offs = [_na_sub_window(kind, sub, rows) * GRID_W // tb for kind in range(3)]

        def f(p, bi, t):
            first = jnp.clip(n_q * t - (slab - n_q) // 2, 0, last_k0)
            off = jnp.where(t == 0, offs[0], jnp.where(t == n_tiles - 1, offs[2], offs[1]))
            return (bi, p, (1 + first + off) * tb, 0)
        return f

    nq = NA_Q_ROWS * GRID_W
    tile_spec = pl.BlockSpec((1, 1, nq, LANES), lambda p, bi, t: (bi, p, t, 0))
    window_shape = tuple(pl.Element(n) for n in (1, 1, n_k * tb, 2 * LANES))
    kv_specs = [pl.BlockSpec(window_shape, window_map(sub)) for sub in range(n_q)]
    in_specs = ([tile_spec] + kv_specs
                + [pl.BlockSpec((1, 1, tb, 2 * LANES), lambda p, bi, t: (bi, p, 0, 0)),
                   pl.BlockSpec((1, 2) + bias.shape[2:], lambda p, bi, t: (tile_kind(t), p, 0, 0))])
    args = [q] + [kv] * (len(kv_specs) + 1) + [bias]
    return pl.pallas_call(
        functools.partial(_na_attn_kernel, n_q=n_q),
        grid=(hp, b, n_tiles),
        in_specs=in_specs,
        out_specs=tile_spec,
        out_shape=jax.ShapeDtypeStruct((b, hp, seq, LANES), BF16),
        compiler_params=_params("parallel", "parallel", "parallel"),
        name="na_attention",
    )(*args)


def kernel(x, c, ctx, c_ctx, norm_g, ada_w, ada_b, ffn_w_in, ffn_w_out, ssm_w_in, ssm_lambda_re, ssm_lambda_im, ssm_log_step, ssm_b_re, ssm_b_im, ssm_c_re, ssm_c_im, ssm_d, ssm_w_glu, na_w_qkv, na_q_norm, na_k_norm, na_rpb, na_w_o):
    b, seq, d = x.shape
    ctx_rows = ctx.shape[1]
    depth = norm_g.shape[0]
    assert b + 1 <= 8 and ctx_rows == TOKEN_TILE and depth == 2

    cvec = jnp.zeros((8, d), F32).at[:b].set(c).at[b].set(c_ctx)
    m = _ada_modulation(cvec, ada_w, ada_b).reshape(depth, 8, N_SUB, 3, d)

    def mod_rows(layer, sub):
        lat = m[layer, :b, sub]
        cx = jnp.broadcast_to(m[layer, b, sub], (b, 3, d))
        return jnp.concatenate([cx, lat], axis=1)

    def gain(layer, sub):
        return norm_g[layer, sub].reshape(1, d)

    w_in = ffn_w_in.astype(BF16)
    w_out = ffn_w_out.astype(BF16)

    def ffn(layer, which):
        sub = 2 * which
        return mod_rows(layer, sub), gain(layer, sub), w_in, w_out, (layer, which)

    xs, u = _stage(x, ffn(0, 0), pre=("join", ctx),
                   post=("s5", mod_rows(0, 1), gain(0, 1), ssm_w_in[0].astype(BF16)),
                   ctx_rows=ctx_rows, name="ffn_s5in")
    n_chunks = (ctx_rows + seq) // S5_CHUNK
    assert n_chunks % SUBLANES == 0 and ctx_rows % S5_CHUNK == 0
    mats = _s5_prep(ssm_lambda_re[0], ssm_lambda_im[0], ssm_log_step[0], ssm_b_re[0], ssm_b_im[0],
                    ssm_c_re[0], ssm_c_im[0], S5_CHUNK, n_chunks // SUBLANES)
    y = _s5_scan(u, ssm_d[0].reshape(1, d), mats, S5_CHUNK, ctx_rows)
    xs, = _stage(xs, ffn(0, 1), pre=("s5", y, mod_rows(0, 1), ssm_w_glu[0].astype(BF16)),
                 ctx_rows=ctx_rows, name="s5out_ffn")

    xs, q, kv = _stage(xs, ffn(1, 0),
                       post=("qkv", mod_rows(1, 1), gain(1, 1), na_w_qkv[0].astype(BF16),
                             na_q_norm[0], na_k_norm[0]),
                       ctx_rows=ctx_rows, name="ffn_qkv")
    bias = _na_bias(na_rpb[0], seq // GRID_W)
    attn = _na_attention(q, kv, bias, ctx_rows)
    out, = _stage(xs, ffn(1, 1), pre=("na", attn, mod_rows(1, 1), na_w_o[0].astype(BF16)),
                  ctx_rows=ctx_rows, name="naout_ffn")
    return out
```

```python
import functools
import math

import jax
import jax.numpy as jnp
from jax import lax
from jax.experimental import pallas as pl
from jax.experimental.pallas import tpu as pltpu

F32 = jnp.float32
BF16 = jnp.bfloat16

GRID_W = 64
N_SUB = 3
MACARON_WEIGHT = 0.5
RMS_EPS = 1e-6
S5_GROUP = 16
S5_STATE = 64
S5_MIN_NEG_RE = -1e-4
NA_HEADS = 16
WIN_H = 8
WIN_W = 16

LANES = 128
SUBLANES = 8
MXU_DIM = 256
VMEM_LIMIT = 56 * 1024 * 1024

TOKEN_TILE = 256
LATENT_TOKEN_TILE = 512
WEIGHT_CAST_CHUNKS = 16
BF16_SUBLANES = 16
S5_CHUNK = 8
S5_LANE_GROUPS = LANES // S5_GROUP
NA_Q_ROWS = 8
NA_K_ROWS = 16
NA_SUB_Q_ROWS = 4
NA_SUB_K_ROWS = 12
LOG2E = math.log2(math.e)


def _dot(a, b):
    return jnp.dot(a, b, preferred_element_type=F32)


def _dot_nt(a, b):
    return lax.dot_general(a, b, (((1,), (1,)), ((), ())), preferred_element_type=F32)


def _dot_nt_f32(a, b):
    return lax.dot_general(a, b, (((1,), (1,)), ((), ())), preferred_element_type=F32,
                           precision=lax.Precision.HIGHEST)


def _split_bf16(x):
    hi = x.astype(BF16)
    return hi, (x - hi.astype(F32)).astype(BF16)


def _params(*sem):
    return pltpu.CompilerParams(dimension_semantics=sem, vmem_limit_bytes=VMEM_LIMIT)


def _resident(shape, index_map):
    return pl.BlockSpec(shape, index_map, pipeline_mode=pl.Buffered(1))


def _ada_kernel(ct_ref, w_ref, b_ref, o_ref, *, n_rows):
    ct = ct_ref[...]
    s = ct * jax.nn.sigmoid(ct)
    w = w_ref[0]
    rows = [jnp.sum(w * s[:, r:r + 1], axis=0, keepdims=True) + b_ref[0] for r in range(n_rows)]
    rows.append(jnp.zeros((SUBLANES - n_rows, w.shape[1]), F32))
    o_ref[0] = jnp.concatenate(rows, axis=0)


def _ada_modulation(cvec, n_rows, ada_w, ada_b):
    depth, d, n = ada_w.shape
    tn = d
    return pl.pallas_call(
        functools.partial(_ada_kernel, n_rows=n_rows),
        grid=(depth, n // tn),
        in_specs=[pl.BlockSpec((d, SUBLANES), lambda l, j: (0, 0)),
                  pl.BlockSpec((1, d, tn), lambda l, j: (l, 0, j)),
                  pl.BlockSpec((1, 1, tn), lambda l, j: (l, 0, j))],
        out_specs=pl.BlockSpec((1, SUBLANES, tn), lambda l, j: (l, 0, j)),
        out_shape=jax.ShapeDtypeStruct((depth, SUBLANES, n), F32),
        compiler_params=_params("parallel", "parallel"),
        name="ada_modulation",
    )(cvec.T, ada_w, ada_b.reshape(depth, 1, n))


def _row_is_context(tm, first_row, ctx_rows):
    return (first_row + lax.broadcasted_iota(jnp.int32, (tm, 1), 0)) < ctx_rows


def _modulated_norm(x, g, mod, first_row, ctx_rows):
    tm = x.shape[0]
    y = x * lax.rsqrt(jnp.mean(x * x, axis=-1, keepdims=True) + RMS_EPS) * g
    if ctx_rows == 0:
        shift, scale, gate = mod[3:4], mod[4:5], mod[5:6]
    else:
        is_ctx = _row_is_context(tm, first_row, ctx_rows)
        shift = jnp.where(is_ctx, mod[0:1], mod[3:4])
        scale = jnp.where(is_ctx, mod[1:2], mod[4:5])
        gate = jnp.where(is_ctx, mod[2:3], mod[5:6])
    return y * (1.0 + scale) + shift, gate


def _gelu_tanh(y):
    return 0.5 * y * (1.0 + jnp.tanh(math.sqrt(2.0 / math.pi) * (y + 0.044715 * (y * y * y))))


def _stage_kernel(*refs, pre, post, n_convert, ctx_rows, hidden, head_dim):
    it = iter(refs)
    x_ref = next(it)
    tm, d = x_ref.shape[1], x_ref.shape[2]
    first_row = pl.program_id(1) * tm
    x = x_ref[0]
    if pre == "join":
        x = jnp.where(pl.program_id(1) == 0, next(it)[0], x)

    def gate_of(mod):
        if ctx_rows == 0:
            return mod[5:6]
        return jnp.where(_row_is_context(tm, first_row, ctx_rows), mod[2:3], mod[5:6])

    if pre == "s5":
        y_ref, pmod_ref, wglu_ref = next(it), next(it), next(it)
        z = _dot(_gelu_tanh(y_ref[0]).astype(BF16), wglu_ref[...])
        x = x + gate_of(pmod_ref[0]) * (z[:, :d] * jax.nn.sigmoid(z[:, d:]))
    elif pre == "na":
        a_ref, pmod_ref, wo_ref = next(it), next(it), next(it)
        a = jnp.concatenate([a_ref[0, p] for p in range(a_ref.shape[1])], axis=-1)
        x = x + gate_of(pmod_ref[0]) * _dot(a, wo_ref[...])

    mod_ref, g_ref, win_ref, wout_ref = next(it), next(it), next(it), next(it)
    h, gate = _modulated_norm(x, g_ref[...], mod_ref[0], first_row, ctx_rows)
    hb = h.astype(BF16)
    gt = _dot(hb, win_ref[:, :hidden])
    up = _dot(hb, win_ref[:, hidden:])
    act = (gt * jax.nn.sigmoid(gt) * up).astype(BF16)
    x = x + (MACARON_WEIGHT * gate) * _dot(act, wout_ref[...])

    if post == "s5":
        qmod_ref, qg_ref, w_ref = next(it), next(it), next(it)
    elif post == "qkv":
        (qmod_ref, qg_ref, w_ref, reduce_ref, expand_ref,
         qgain_ref, kgain_ref) = (next(it) for _ in range(7))

    cast_in = [next(it) for _ in range(n_convert)]
    cast_out = refs[len(refs) - n_convert:]

    @pl.when(pl.program_id(0) * pl.num_programs(1) + pl.program_id(1) < WEIGHT_CAST_CHUNKS)
    def _():
        for src, dst in zip(cast_in, cast_out):
            dst[...] = src[...].astype(BF16)

    o_ref = next(it)
    o_ref[0] = x
    if post is None:
        return
    h2, _ = _modulated_norm(x, qg_ref[...], qmod_ref[0], first_row, ctx_rows)
    proj = _dot(h2.astype(BF16), w_ref[...])
    if post == "s5":
        next(it)[0] = proj
        return

    def head_norm(z, gain):
        ms = _dot((z * z).astype(BF16), reduce_ref[...]) * (1.0 / head_dim)
        hi, lo = _split_bf16(lax.rsqrt(ms + RMS_EPS))
        return z * _dot(jnp.concatenate([hi, lo], axis=1), expand_ref[...]) * gain

    q = head_norm(proj[:, :d], qgain_ref[...]) * (head_dim ** -0.5 * LOG2E)
    k = head_norm(proj[:, d:2 * d], kgain_ref[...])
    v = proj[:, 2 * d:]
    q_ref, kv_ref = next(it), next(it)
    for hp in range(d // LANES):
        sl = slice(hp * LANES, (hp + 1) * LANES)
        q_ref[0, hp] = q[:, sl].astype(BF16)
        kv_ref[0, hp, :, :LANES] = k[:, sl].astype(BF16)
        kv_ref[0, hp, :, LANES:] = v[:, sl].astype(BF16)


def _stage(xs, ffn, *, pre=None, post=None, convert=(), ctx_rows, name):
    b, s, d = xs.shape
    tm = TOKEN_TILE
    mod, gain, w_in, w_out, which = ffn
    hidden = w_out.shape[-2]
    head_dim = d // NA_HEADS
    tok = pl.BlockSpec((1, tm, d), lambda i, t: (i, t, 0))
    modspec = pl.BlockSpec((1, 6, d), lambda i, t: (i, 0, 0))
    vec = pl.BlockSpec((1, d), lambda i, t: (0, 0))

    def weight(w, lead=()):
        return _resident((None,) * len(lead) + w.shape[len(lead):], lambda i, t: lead + (0, 0))

    args = [xs]
    if pre is not None and pre[0] == "join":
        assert ctx_rows == tm
        s += ctx_rows
        args.append(pre[1])
        in_specs = [pl.BlockSpec((1, tm, d), lambda i, t: (i, jnp.maximum(t - 1, 0), 0)),
                    pl.BlockSpec((1, tm, d), lambda i, t: (i, 0, 0))]
    elif pre is not None and pre[0] == "na":
        skip = ctx_rows
        s, ctx_rows = s - ctx_rows, 0
        tm = LATENT_TOKEN_TILE
        tok = pl.BlockSpec((1, tm, d), lambda i, t: (i, t, 0))
        assert tm % skip == 0
        in_specs = [pl.BlockSpec(tuple(pl.Element(n) for n in (1, tm, d)),
                                 lambda i, t: (i, (1 + t * (tm // skip)) * skip, 0))]
    else:
        in_specs = [tok]
    if pre is not None and pre[0] == "join":
        pass
    elif pre is not None and pre[0] == "s5":
        _, y, pmod, w_glu = pre
        args += [y, pmod, w_glu]
        in_specs += [tok, modspec, weight(w_glu)]
    elif pre is not None:
        _, attn, pmod, w_o = pre
        args += [attn, pmod, w_o]
        in_specs += [pl.BlockSpec((1, attn.shape[1], tm, LANES), lambda i, t: (i, 0, t, 0)),
                     modspec, weight(w_o)]
    args += [mod, gain, w_in, w_out]
    in_specs += [modspec, vec, weight(w_in, which), weight(w_out, which)]
    out_shape = [jax.ShapeDtypeStruct((b, s, d), F32)]
    out_specs = [tok]
    if post is not None and post[0] == "s5":
        _, qmod, qgain, w = post
        args += [qmod, qgain, w]
        in_specs += [modspec, vec, weight(w)]
        out_shape.append(jax.ShapeDtypeStruct((b, s, d), F32))
        out_specs.append(tok)
    elif post is not None:
        _, qmod, qgain, w, q_gain, k_gain = post
        hid = jnp.arange(d) // head_dim
        to_head = (hid[:, None] == jnp.arange(LANES)[None, :]).astype(BF16)
        from_head = jnp.concatenate([to_head.T, to_head.T], axis=0)
        args += [qmod, qgain, w, to_head, from_head,
                 jnp.tile(q_gain, NA_HEADS).reshape(1, d), jnp.tile(k_gain, NA_HEADS).reshape(1, d)]
        in_specs += [modspec, vec, weight(w), weight(to_head), weight(from_head), vec, vec]
        hp = d // LANES
        assert ctx_rows == tm
        out_shape += [jax.ShapeDtypeStruct((b, hp, s - ctx_rows, LANES), BF16),
                      jax.ShapeDtypeStruct((b, hp, s, 2 * LANES), BF16)]
        out_specs += [pl.BlockSpec((1, hp, tm, LANES), lambda i, t: (i, 0, jnp.maximum(t - 1, 0), 0)),
                      pl.BlockSpec((1, hp, tm, 2 * LANES), lambda i, t: (i, 0, t, 0))]
    n_t = s // tm
    for w, lead in convert:
        rows, cols = w.shape[-2:]
        chunk = rows // WEIGHT_CAST_CHUNKS
        assert chunk * WEIGHT_CAST_CHUNKS == rows and chunk % BF16_SUBLANES == 0
        assert b * n_t >= WEIGHT_CAST_CHUNKS

        def chunk_of(i, t):
            return jnp.minimum(i * n_t + t, WEIGHT_CAST_CHUNKS - 1)

        args.append(w)
        in_specs.append(pl.BlockSpec((None,) * len(lead) + (chunk, cols),
                                     lambda i, t, lead=lead: lead + (chunk_of(i, t), 0)))
        out_shape.append(jax.ShapeDtypeStruct((rows, cols), BF16))
        out_specs.append(pl.BlockSpec((chunk, cols), lambda i, t: (chunk_of(i, t), 0)))
    return pl.pallas_call(
        functools.partial(_stage_kernel, pre=None if pre is None else pre[0],
                          post=None if post is None else post[0], n_convert=len(convert),
                          ctx_rows=ctx_rows, hidden=hidden, head_dim=head_dim),
        grid=(b, n_t),
        in_specs=in_specs,
        out_specs=out_specs,
        out_shape=out_shape,
        compiler_params=_params("parallel", "arbitrary"),
        name=name,
    )(*args)


def _s5_prep_kernel(lr_ref, li_ref, ls_ref, btr_ref, bti_ref, cr_ref, ci_ref,
                    tpd_ref, wst_ref, prdt_ref, avec_ref, *, chunk, seg_chunks):
    n = LANES
    ng = S5_LANE_GROUPS
    half = n // 2
    wide = ng * half
    lr = jnp.minimum(lr_ref[0], S5_MIN_NEG_RE)
    li = li_ref[0]
    dt = jnp.exp(ls_ref[0])
    mag = jnp.exp(lr * dt)
    ar = mag * jnp.cos(li * dt)
    ai = mag * jnp.sin(li * dt)
    den = lr * lr + li * li
    zr = ((ar - 1.0) * lr + ai * li) / den
    zi = (ai * lr - (ar - 1.0) * li) / den
    btr, bti = btr_ref[0], bti_ref[0]
    bbr = zr * btr - zi * bti
    bbi = zr * bti + zi * btr
    cr, ci = cr_ref[0], ci_ref[0]

    lane = lax.broadcasted_iota(jnp.int32, (n, n), 1)
    row = lax.broadcasted_iota(jnp.int32, (n, n), 0)
    first_half = lane < half
    same_group = (row // S5_GROUP) == (lane // S5_GROUP)
    row_w = lax.broadcasted_iota(jnp.int32, (n, wide), 0)
    lane_w = lax.broadcasted_iota(jnp.int32, (n, wide), 1)
    own_states = (row_w // S5_GROUP) == (lane_w // half)

    def spread(v):
        return jnp.where(own_states, jnp.concatenate([v] * (wide // n), axis=1), 0.0).astype(BF16)

    powers = [(jnp.ones((n, n), F32), jnp.zeros((n, n), F32))]
    for _ in range(chunk):
        er, ei = powers[-1]
        powers.append((er * ar - ei * ai, er * ai + ei * ar))

    def input_map(k):
        er, ei = powers[k]
        return bbr * er - bbi * ei, bbr * ei + bbi * er

    def output_map(k):
        er, ei = powers[k]
        return cr * er - ci * ei, cr * ei + ci * er

    c2 = jnp.where(first_half, cr, -ci)
    lag_blocks = []
    for k in range(chunk):
        wr, wi = input_map(k)
        kd = _dot_nt_f32(jnp.where(first_half, wr, wi), c2)
        lag_blocks.append(jnp.where(same_group, kd, 0.0).astype(BF16))
    zero_block = jnp.zeros((n, n), BF16)

    def emit(reverse):
        for s in range(chunk):
            wr, wi = input_map(s if reverse else chunk - 1 - s)
            wst_ref[0, 0, s * n:(s + 1) * n, 0:wide] = spread(wr)
            wst_ref[0, 0, s * n:(s + 1) * n, wide:2 * wide] = spread(wi)
        for t in range(chunk):
            pr, pi = output_map(chunk - t if reverse else t + 1)
            prdt_ref[0, 0, t * n:(t + 1) * n, 0:wide] = spread(pr)
            prdt_ref[0, 0, t * n:(t + 1) * n, wide:2 * wide] = spread(-pi)
        for s in range(chunk):
            for t in range(chunk):
                lag = s - t if reverse else t - s
                tpd_ref[0, 0, s * n:(s + 1) * n, t * n:(t + 1) * n] = (
                    lag_blocks[lag] if lag >= 0 else zero_block)

    @pl.when(pl.program_id(0) == 0)
    def _():
        emit(False)

    @pl.when(pl.program_id(0) == 1)
    def _():
        emit(True)

    er, ei = powers[chunk]
    sr, si = jnp.ones((n, n), F32), jnp.zeros((n, n), F32)
    pr_, pi_ = er, ei
    e = seg_chunks
    while e:
        if e & 1:
            sr, si = sr * pr_ - si * pi_, sr * pi_ + si * pr_
        pr_, pi_ = pr_ * pr_ - pi_ * pi_, 2.0 * pr_ * pi_
        e >>= 1
    for r, v in enumerate((er, ei, sr, si)):
        for j in range(ng // 2):
            g0 = 2 * j * S5_GROUP
            g1 = g0 + S5_GROUP
            avec_ref[0, 0, r:r + 1, j * n:(j + 1) * n] = jnp.where(
                first_half[0:1], v[g0:g0 + 1], v[g1:g1 + 1])
    avec_ref[0, 0, 4:8, :] = jnp.zeros((4, wide), F32)


def _s5_prep(lam_re, lam_im, log_step, b_re, b_im, c_re, c_im, chunk, seg_chunks):
    nd, g, p = lam_re.shape
    gs = S5_GROUP
    assert 2 * p == LANES and g % S5_LANE_GROUPS == 0
    n_blocks = g // S5_LANE_GROUPS
    tk = chunk * LANES
    wide = S5_LANE_GROUPS * p

    def rows(v):
        v = v.reshape(nd, g * gs, p)
        return jnp.concatenate([v, v], axis=-1)

    def per_group(v):
        return rows(jnp.broadcast_to(v[:, :, None, :], (nd, g, gs, p)))

    args = (per_group(lam_re), per_group(lam_im),
            per_group(jnp.broadcast_to(log_step[:, :, None], (nd, g, p))),
            rows(jnp.swapaxes(b_re, 2, 3)), rows(jnp.swapaxes(b_im, 2, 3)), rows(c_re), rows(c_im))
    sq = pl.BlockSpec((1, LANES, LANES), lambda d, i: (d, i, 0))
    big = pl.BlockSpec((1, 1, tk, tk), lambda d, i: (d, i, 0, 0))
    assert tk == 2 * wide
    return pl.pallas_call(
        functools.partial(_s5_prep_kernel, chunk=chunk, seg_chunks=seg_chunks),
        grid=(nd, n_blocks),
        in_specs=[sq] * 7,
        out_specs=[big, big, big, pl.BlockSpec((1, 1, 8, wide), lambda d, i: (d, i, 0, 0))],
        out_shape=[jax.ShapeDtypeStruct((nd, n_blocks, tk, tk), BF16)] * 3
                  + [jax.ShapeDtypeStruct((nd, n_blocks, 8, wide), F32)],
        compiler_params=_params("parallel", "parallel"),
        name="s5_prep",
    )(*args)


def _s5_scan_kernel(u_ref, dsk_ref, tpd_ref, wst_ref, prdt_ref, avec_ref, y_ref,
                    xcat, sv, *, chunk, n_ctx):
    n = LANES
    n_c = xcat.shape[0]
    nb = sv.shape[0] // 2
    n_seg = SUBLANES
    seg = n_c // n_seg
    n_lat = n_c - n_ctx
    direction = pl.program_id(2)

    @pl.when(direction == 0)
    def _():
        for s in range(chunk):
            xcat[:, s * n:(s + 1) * n] = u_ref[0, pl.ds(s, n_c, stride=chunk), :].astype(BF16)

    def run(reverse):
        wst = wst_ref[0, 0]

        def summarise(dst, src):
            s = _dot(xcat[src, :], wst)
            for j in range(2 * nb):
                sv[j, dst, :] = s[:, j * n:(j + 1) * n]

        if reverse:
            summarise(slice(0, n_lat), slice(n_ctx, n_c))
            summarise(slice(n_lat, n_c), slice(0, n_ctx))
        else:
            summarise(slice(0, n_c), slice(0, n_c))

        def lane_blocks(r):
            return [avec_ref[0, 0, r:r + 1, j * n:(j + 1) * n] for j in range(nb)]

        ar = [jnp.broadcast_to(v, (n_seg, n)) for v in lane_blocks(0)]
        ai = [jnp.broadcast_to(v, (n_seg, n)) for v in lane_blocks(1)]

        def rows_at(i):
            return pl.ds(seg - 1 - i if reverse else i, n_seg, stride=seg)

        def advance(h, rows):
            out_r, out_i = [], []
            for j in range(nb):
                hr, hi = h[j], h[nb + j]
                out_r.append(ar[j] * hr - ai[j] * hi + sv[j, rows, :])
                out_i.append(ar[j] * hi + ai[j] * hr + sv[nb + j, rows, :])
            return tuple(out_r + out_i)

        zero = jnp.zeros((n_seg, n), F32)
        fin = lax.fori_loop(0, seg, lambda i, h: advance(h, rows_at(i)), (zero,) * (2 * nb))

        asr, asi = lane_blocks(2), lane_blocks(3)
        order = range(n_seg - 1, -1, -1) if reverse else range(n_seg)
        h0 = []
        for j in range(nb):
            rows_r, rows_i = [None] * n_seg, [None] * n_seg
            pr = pi = jnp.zeros((1, n), F32)
            prev = None
            for sgm in order:
                if prev is not None:
                    pr, pi = (asr[j] * pr - asi[j] * pi + fin[j][prev:prev + 1],
                              asr[j] * pi + asi[j] * pr + fin[nb + j][prev:prev + 1])
                rows_r[sgm], rows_i[sgm] = pr, pi
                prev = sgm
            h0.append((jnp.concatenate(rows_r, axis=0), jnp.concatenate(rows_i, axis=0)))
        h0 = tuple(v[0] for v in h0) + tuple(v[1] for v in h0)

        def step(i, h):
            rows = rows_at(i)
            nxt = advance(h, rows)
            for j in range(2 * nb):
                sv[j, rows, :] = h[j]
            return nxt

        lax.fori_loop(0, seg, step, h0)

        def entry_states(rows):
            return jnp.concatenate([sv[j, rows, :] for j in range(2 * nb)], axis=1).astype(BF16)

        if reverse:
            h_all = jnp.concatenate([entry_states(slice(n_lat, n_c)), entry_states(slice(0, n_lat))],
                                    axis=0)
        else:
            h_all = entry_states(slice(0, n_c))
        blk = MXU_DIM
        n_blk = xcat.shape[1] // blk
        intra = []
        for j in range(n_blk):
            ks = slice(j * blk, n_blk * blk) if reverse else slice(0, (j + 1) * blk)
            intra.append(_dot(xcat[:, ks], tpd_ref[0, 0, ks, j * blk:(j + 1) * blk]))
        return jnp.concatenate(intra, axis=1) + _dot_nt(h_all, prdt_ref[0, 0])

    @pl.when(direction == 0)
    def _():
        y = run(False)
        for t in range(chunk):
            rows = pl.ds(t, n_c, stride=chunk)
            y_ref[0, rows, :] = y[:, t * n:(t + 1) * n] + u_ref[0, rows, :] * dsk_ref[...]

    @pl.when(direction == 1)
    def _():
        y = run(True)
        for t in range(chunk):
            rows = pl.ds(t, n_c, stride=chunk)
            y_ref[0, rows, :] = y_ref[0, rows, :] + y[:, t * n:(t + 1) * n]


def _s5_scan(u, d_skip, mats, chunk, ctx_rows):
    b, s, d = u.shape
    tpd, wst, prdt, avec = mats
    nd, n_blocks, tk, _ = tpd.shape
    wide = avec.shape[-1]
    n_c = s // chunk

    def mat(shape):
        return pl.BlockSpec((1, 1) + shape, lambda o, i, dr: (dr, o, 0, 0))

    tok = pl.BlockSpec((1, s, LANES), lambda o, i, dr: (i, 0, o))
    return pl.pallas_call(
        functools.partial(_s5_scan_kernel, chunk=chunk, n_ctx=ctx_rows // chunk),
        grid=(n_blocks, b, nd),
        in_specs=[tok, pl.BlockSpec((1, LANES), lambda o, i, dr: (0, o)),
                  mat((tk, tk)), mat((tk, tk)), mat((tk, tk)), mat((8, wide))],
        out_specs=tok,
        out_shape=jax.ShapeDtypeStruct((b, s, d), F32),
        scratch_shapes=[pltpu.VMEM((n_c, tk), BF16),
                        pltpu.VMEM((2 * wide // LANES, n_c, LANES), F32)],
        compiler_params=_params("parallel", "parallel", "arbitrary"),
        name="s5_scan",
    )(u, d_skip, tpd, wst, prdt, avec)


def _window_start(r, rows):
    kh = min(WIN_H, rows)
    return min(max(r - kh // 2, 0), rows - kh)


def _na_tile_geometry(kind, rows):
    n_tiles = rows // NA_Q_ROWS
    tile = {0: 0, 1: 1, 2: n_tiles - 1}[kind]
    q0 = tile * NA_Q_ROWS
    k0 = min(max(q0 - (NA_K_ROWS - NA_Q_ROWS) // 2, 0), rows - NA_K_ROWS)
    return q0, k0


def _na_sub_window(kind, sub, rows):
    q0, k0 = _na_tile_geometry(kind, rows)
    kh = min(WIN_H, rows)
    starts = [_window_start(q0 + NA_SUB_Q_ROWS * sub + i, rows) - k0 for i in range(NA_SUB_Q_ROWS)]
    first = min(min(starts) // NA_SUB_Q_ROWS * NA_SUB_Q_ROWS, NA_K_ROWS - NA_SUB_K_ROWS)
    assert first >= 0 and max(starts) + kh <= first + NA_SUB_K_ROWS
    return first


def _na_bias_kernel(rpb_ref, o_ref, *, rows):
    h = pl.program_id(0)
    w = GRID_W
    kh = min(WIN_H, rows)
    ncol = 2 * WIN_W - 1
    nrow = 2 * WIN_H - 1
    cq = lax.broadcasted_iota(jnp.int32, (w, LANES), 0)
    lane = lax.broadcasted_iota(jnp.int32, (w, LANES), 1)
    ck = lane % w
    left = lane < w
    cs = jnp.clip(cq - WIN_W // 2, 0, w - WIN_W)
    col_ok = (ck >= cs) & (ck < cs + WIN_W)
    dc = jnp.clip(ck - cq + WIN_W - 1, 0, ncol - 1)
    neg = jnp.full((w, LANES), -jnp.inf, F32)

    def pair_table(d_left, d_right):
        t = jnp.zeros((w, LANES), F32)
        for j in range(ncol):
            vl = rpb_ref[(h * nrow + d_left) * ncol + j] if d_left is not None else 0.0
            vr = rpb_ref[(h * nrow + d_right) * ncol + j] if d_right is not None else 0.0
            t = jnp.where(dc == j, jnp.where(left, vl, vr), t)
        t = t * LOG2E
        ok = col_ok
        if d_left is None:
            ok = ok & jnp.logical_not(left)
        if d_right is None:
            ok = ok & left
        return jnp.where(ok, t, neg)

    cache = {}
    for kind in range(3):
        q0, k0 = _na_tile_geometry(kind, rows)
        for rq in range(NA_Q_ROWS):
            r = q0 + rq
            rs = _window_start(r, rows)
            first = k0 + _na_sub_window(kind, rq // NA_SUB_Q_ROWS, rows)
            for m in range(NA_SUB_K_ROWS // 2):
                ds = []
                for kr in (first + 2 * m, first + 2 * m + 1):
                    ds.append(kr - r + WIN_H - 1 if rs <= kr < rs + kh else None)
                key = tuple(ds)
                if key == (None, None):
                    blk = neg
                else:
                    if key not in cache:
                        cache[key] = pair_table(*key)
                    blk = cache[key]
                o_ref[kind, 0, rq * w:(rq + 1) * w, m * LANES:(m + 1) * LANES] = blk


def _na_bias(rpb, rows):
    nh = rpb.shape[0]
    nq, nk = NA_Q_ROWS * GRID_W, NA_SUB_K_ROWS * GRID_W
    return pl.pallas_call(
        functools.partial(_na_bias_kernel, rows=rows),
        grid=(nh,),
        in_specs=[pl.BlockSpec(memory_space=pltpu.SMEM)],
        out_specs=pl.BlockSpec((3, 1, nq, nk), lambda h: (0, h, 0, 0)),
        out_shape=jax.ShapeDtypeStruct((3, nh, nq, nk), F32),
        compiler_params=_params("parallel"),
        name="na_bias",
    )(rpb.reshape(-1))


def _na_attn_kernel(*refs, n_q):
    q_ref = refs[0]
    kv_refs = refs[1:1 + n_q]
    kvc_ref, bias_ref, o_ref = refs[1 + n_q:]
    tb = kvc_ref.shape[2]
    half = LANES // 2

    def per_head(x, other):
        first_head = lax.broadcasted_iota(jnp.int32, x.shape, 1) < half
        fill = jnp.full_like(x, other)
        return [jnp.where(first_head, x, fill), jnp.where(first_head, fill, x)]

    kc = kvc_ref[0, 0, :, :LANES]
    vc_heads = per_head(kvc_ref[0, 0, :, LANES:], 1.0)
    units = [(sub, hh) for sub in range(n_q) for hh in range(2)]
    keys, values, queries = {}, {}, {}
    for sub in range(n_q):
        keys[sub] = kv_refs[sub][0, 0, :, :LANES]
        values[sub] = per_head(kv_refs[sub][0, 0, :, LANES:], 1.0)
        queries[sub] = per_head(q_ref[0, 0, sub * tb:(sub + 1) * tb, :], 0.0)

    def score(unit):
        sub, hh = unit
        qh = queries[sub][hh]
        return (_dot_nt(qh, keys[sub]) + bias_ref[0, hh, sub * tb:(sub + 1) * tb, :],
                _dot_nt(qh, kc))

    ahead = 2
    pending = [score(u) for u in units[:ahead]]
    outs = {}
    for i, (sub, hh) in enumerate(units):
        s, sc = pending.pop(0)
        if i + ahead < len(units):
            pending.append(score(units[i + ahead]))
        m = jnp.maximum(jnp.max(s, axis=-1, keepdims=True), jnp.max(sc, axis=-1, keepdims=True))
        p = jnp.exp2(s - m).astype(BF16)
        pc = jnp.exp2(sc - m).astype(BF16)
        o = _dot(p, values[sub][hh]) + _dot(pc, vc_heads[hh])
        denom = o[:, half:half + 1] if hh == 0 else o[:, 0:1]
        outs[sub, hh] = o / denom
    first_head = lax.broadcasted_iota(jnp.int32, (tb, LANES), 1) < half
    for sub in range(n_q):
        o_ref[0, 0, sub * tb:(sub + 1) * tb, :] = jnp.where(
            first_head, outs[sub, 0], outs[sub, 1]).astype(BF16)


def _na_attention(q, kv, bias, ctx_rows):
    b, hp, seq, _ = q.shape
    tb = TOKEN_TILE
    assert ctx_rows == tb and kv.shape[2] == ctx_rows + seq
    rows = seq // GRID_W
    n_tiles = rows // NA_Q_ROWS
    assert n_tiles >= 3
    assert NA_SUB_Q_ROWS * GRID_W == tb
    n_q = NA_Q_ROWS // NA_SUB_Q_ROWS
    n_k = NA_SUB_K_ROWS * GRID_W // tb
    slab = NA_K_ROWS * GRID_W // tb
    last_k0 = (rows - NA_K_ROWS) * GRID_W // tb

    def tile_kind(t):
        return jnp.where(t == 0, 0, jnp.where(t == n_tiles - 1, 2, 1))

    def window_map(sub):
        offs = [_na_sub_window(kind, sub, rows) * GRID_W // tb for kind in range(3)]

        def f(p, bi, t):
            first = jnp.clip(n_q * t - (slab - n_q) // 2, 0, last_k0)
            off = jnp.where(t == 0, offs[0], jnp.where(t == n_tiles - 1, offs[2], offs[1]))
            return (bi, p, (1 + first + off) * tb, 0)
        return f

    nq = NA_Q_ROWS * GRID_W
    tile_spec = pl.BlockSpec((1, 1, nq, LANES), lambda p, bi, t: (bi, p, t, 0))
    window_shape = tuple(pl.Element(n) for n in (1, 1, n_k * tb, 2 * LANES))
    kv_specs = [pl.BlockSpec(window_shape, window_map(sub)) for sub in range(n_q)]
    in_specs = ([tile_spec] + kv_specs
                + [pl.BlockSpec((1, 1, tb, 2 * LANES), lambda p, bi, t: (bi, p, 0, 0)),
                   pl.BlockSpec((1, 2) + bias.shape[2:], lambda p, bi, t: (tile_kind(t), p, 0, 0))])
    args = [q] + [kv] * (len(kv_specs) + 1) + [bias]
    return pl.pallas_call(
        functools.partial(_na_attn_kernel, n_q=n_q),
        grid=(hp, b, n_tiles),
        in_specs=in_specs,
        out_specs=tile_spec,
        out_shape=jax.ShapeDtypeStruct((b, hp, seq, LANES), BF16),
        compiler_params=_params("parallel", "parallel", "parallel"),
        name="na_attention",
    )(*args)


def kernel(x, c, ctx, c_ctx, norm_g, ada_w, ada_b, ffn_w_in, ffn_w_out, ssm_w_in, ssm_lambda_re, ssm_lambda_im, ssm_log_step, ssm_b_re, ssm_b_im, ssm_c_re, ssm_c_im, ssm_d, ssm_w_glu, na_w_qkv, na_q_norm, na_k_norm, na_rpb, na_w_o):
    b, seq, d = x.shape
    ctx_rows = ctx.shape[1]
    depth = norm_g.shape[0]
    assert b + 1 <= 8 and ctx_rows == TOKEN_TILE and depth == 2

    cvec = jnp.zeros((8, d), F32).at[:b].set(c).at[b].set(c_ctx)
    m = _ada_modulation(cvec, b + 1, ada_w, ada_b).reshape(depth, 8, N_SUB, 3, d)

    def mod_rows(layer, sub):
        lat = m[layer, :b, sub]
        cx = jnp.broadcast_to(m[layer, b, sub], (b, 3, d))
        return jnp.concatenate([cx, lat], axis=1)

    def gain(layer, sub):
        return norm_g[layer, sub].reshape(1, d)

    def ffn(layer, which, w_in, w_out):
        sub = 2 * which
        return mod_rows(layer, sub), gain(layer, sub), w_in, w_out, ()

    def ffn_f32(layer, which):
        return [(ffn_w_in, (layer, which)), (ffn_w_out, (layer, which))]

    xs, u, w_in, w_out, w_glu = _stage(
        x, ffn(0, 0, ffn_w_in[0, 0].astype(BF16), ffn_w_out[0, 0].astype(BF16)),
        pre=("join", ctx), post=("s5", mod_rows(0, 1), gain(0, 1), ssm_w_in[0].astype(BF16)),
        convert=ffn_f32(0, 1) + [(ssm_w_glu, (0,))], ctx_rows=ctx_rows, name="ffn_s5in")
    n_chunks = (ctx_rows + seq) // S5_CHUNK
    assert n_chunks % SUBLANES == 0 and ctx_rows % S5_CHUNK == 0
    mats = _s5_prep(ssm_lambda_re[0], ssm_lambda_im[0], ssm_log_step[0], ssm_b_re[0], ssm_b_im[0],
                    ssm_c_re[0], ssm_c_im[0], S5_CHUNK, n_chunks // SUBLANES)
    y = _s5_scan(u, ssm_d[0].reshape(1, d), mats, S5_CHUNK, ctx_rows)
    xs, w_in, w_out, w_qkv = _stage(
        xs, ffn(0, 1, w_in, w_out), pre=("s5", y, mod_rows(0, 1), w_glu),
        convert=ffn_f32(1, 0) + [(na_w_qkv, (0,))], ctx_rows=ctx_rows, name="s5out_ffn")

    xs, q, kv, w_in, w_out, w_o = _stage(
        xs, ffn(1, 0, w_in, w_out),
        post=("qkv", mod_rows(1, 1), gain(1, 1), w_qkv, na_q_norm[0], na_k_norm[0]),
        convert=ffn_f32(1, 1) + [(na_w_o, (0,))], ctx_rows=ctx_rows, name="ffn_qkv")
    bias = _na_bias(na_rpb[0], seq // GRID_W)
    attn = _na_attention(q, kv, bias, ctx_rows)
    out, = _stage(xs, ffn(1, 1, w_in, w_out), pre=("na", attn, mod_rows(1, 1), w_o),
                  ctx_rows=ctx_rows, name="naout_ffn")
    return out
```

```python
import functools
import math

import jax
import jax.numpy as jnp
from jax import lax
from jax.experimental import pallas as pl
from jax.experimental.pallas import tpu as pltpu

F32 = jnp.float32
BF16 = jnp.bfloat16

GRID_W = 64
N_SUB = 3
MACARON_WEIGHT = 0.5
RMS_EPS = 1e-6
S5_GROUP = 16
S5_STATE = 64
S5_MIN_NEG_RE = -1e-4
NA_HEADS = 16
WIN_H = 8
WIN_W = 16

LANES = 128
SUBLANES = 8
MXU_DIM = 256
VMEM_LIMIT = 56 * 1024 * 1024

TOKEN_TILE = 256
LATENT_TOKEN_TILE = 512
WEIGHT_CAST_CHUNKS = 16
BF16_SUBLANES = 16
S5_CHUNK = 8
S5_LANE_GROUPS = LANES // S5_GROUP
NA_Q_ROWS = 8
NA_K_ROWS = 16
NA_TILES_PER_STEP = 4
NA_SUB_Q_ROWS = 4
NA_SUB_K_ROWS = 12
LOG2E = math.log2(math.e)


def _dot(a, b):
    return jnp.dot(a, b, preferred_element_type=F32)


def _dot_nt(a, b):
    return lax.dot_general(a, b, (((1,), (1,)), ((), ())), preferred_element_type=F32)


def _dot_nt_f32(a, b):
    return lax.dot_general(a, b, (((1,), (1,)), ((), ())), preferred_element_type=F32,
                           precision=lax.Precision.HIGHEST)


def _split_bf16(x):
    hi = x.astype(BF16)
    return hi, (x - hi.astype(F32)).astype(BF16)


def _params(*sem):
    return pltpu.CompilerParams(dimension_semantics=sem, vmem_limit_bytes=VMEM_LIMIT)


def _resident(shape, index_map):
    return pl.BlockSpec(shape, index_map, pipeline_mode=pl.Buffered(1))


def _ada_kernel(ct_ref, w_ref, b_ref, o_ref, *, n_rows):
    ct = ct_ref[...]
    s = ct * jax.nn.sigmoid(ct)
    w = w_ref[0]
    rows = [jnp.sum(w * s[:, r:r + 1], axis=0, keepdims=True) + b_ref[0] for r in range(n_rows)]
    rows.append(jnp.zeros((SUBLANES - n_rows, w.shape[1]), F32))
    o_ref[0] = jnp.concatenate(rows, axis=0)


def _ada_modulation(cvec, n_rows, ada_w, ada_b):
    depth, d, n = ada_w.shape
    tn = d
    return pl.pallas_call(
        functools.partial(_ada_kernel, n_rows=n_rows),
        grid=(depth, n // tn),
        in_specs=[pl.BlockSpec((d, SUBLANES), lambda l, j: (0, 0)),
                  pl.BlockSpec((1, d, tn), lambda l, j: (l, 0, j)),
                  pl.BlockSpec((1, 1, tn), lambda l, j: (l, 0, j))],
        out_specs=pl.BlockSpec((1, SUBLANES, tn), lambda l, j: (l, 0, j)),
        out_shape=jax.ShapeDtypeStruct((depth, SUBLANES, n), F32),
        compiler_params=_params("parallel", "parallel"),
        name="ada_modulation",
    )(cvec.T, ada_w, ada_b.reshape(depth, 1, n))


def _row_is_context(tm, first_row, ctx_rows):
    return (first_row + lax.broadcasted_iota(jnp.int32, (tm, 1), 0)) < ctx_rows


def _modulated_norm(x, g, mod, first_row, ctx_rows):
    tm = x.shape[0]
    y = x * lax.rsqrt(jnp.mean(x * x, axis=-1, keepdims=True) + RMS_EPS) * g
    if ctx_rows == 0:
        shift, scale, gate = mod[3:4], mod[4:5], mod[5:6]
    else:
        is_ctx = _row_is_context(tm, first_row, ctx_rows)
        shift = jnp.where(is_ctx, mod[0:1], mod[3:4])
        scale = jnp.where(is_ctx, mod[1:2], mod[4:5])
        gate = jnp.where(is_ctx, mod[2:3], mod[5:6])
    return y * (1.0 + scale) + shift, gate


def _gelu_tanh(y):
    return 0.5 * y * (1.0 + jnp.tanh(math.sqrt(2.0 / math.pi) * (y + 0.044715 * (y * y * y))))


def _stage_kernel(*refs, pre, post, n_convert, ctx_rows, hidden, head_dim):
    it = iter(refs)
    x_ref = next(it)
    tm, d = x_ref.shape[1], x_ref.shape[2]
    first_row = pl.program_id(1) * tm
    x = x_ref[0]
    if pre == "join":
        x = jnp.where(pl.program_id(1) == 0, next(it)[0], x)

    def gate_of(mod):
        if ctx_rows == 0:
            return mod[5:6]
        return jnp.where(_row_is_context(tm, first_row, ctx_rows), mod[2:3], mod[5:6])

    if pre == "s5":
        y_ref, pmod_ref, wglu_ref = next(it), next(it), next(it)
        z = _dot(_gelu_tanh(y_ref[0]).astype(BF16), wglu_ref[...])
        x = x + gate_of(pmod_ref[0]) * (z[:, :d] * jax.nn.sigmoid(z[:, d:]))
    elif pre == "na":
        a_ref, pmod_ref, wo_ref = next(it), next(it), next(it)
        a = jnp.concatenate([a_ref[0, p] for p in range(a_ref.shape[1])], axis=-1)
        x = x + gate_of(pmod_ref[0]) * _dot(a, wo_ref[...])

    mod_ref, g_ref, win_ref, wout_ref = next(it), next(it), next(it), next(it)
    h, gate = _modulated_norm(x, g_ref[...], mod_ref[0], first_row, ctx_rows)
    hb = h.astype(BF16)
    gt = _dot(hb, win_ref[:, :hidden])
    up = _dot(hb, win_ref[:, hidden:])
    act = (gt * jax.nn.sigmoid(gt) * up).astype(BF16)
    x = x + (MACARON_WEIGHT * gate) * _dot(act, wout_ref[...])

    if post == "s5":
        qmod_ref, qg_ref, w_ref = next(it), next(it), next(it)
    elif post == "qkv":
        (qmod_ref, qg_ref, w_ref, reduce_ref, expand_ref,
         qgain_ref, kgain_ref) = (next(it) for _ in range(7))

    cast_in = [next(it) for _ in range(n_convert)]
    cast_out = refs[len(refs) - n_convert:]

    @pl.when(pl.program_id(0) * pl.num_programs(1) + pl.program_id(1) < WEIGHT_CAST_CHUNKS)
    def _():
        for src, dst in zip(cast_in, cast_out):
            dst[...] = src[...].astype(BF16)

    o_ref = next(it)
    o_ref[0] = x
    if post is None:
        return
    h2, _ = _modulated_norm(x, qg_ref[...], qmod_ref[0], first_row, ctx_rows)
    proj = _dot(h2.astype(BF16), w_ref[...])
    if post == "s5":
        next(it)[0] = proj
        return

    def head_norm(z, gain):
        ms = _dot((z * z).astype(BF16), reduce_ref[...]) * (1.0 / head_dim)
        hi, lo = _split_bf16(lax.rsqrt(ms + RMS_EPS))
        return z * _dot(jnp.concatenate([hi, lo], axis=1), expand_ref[...]) * gain

    q = head_norm(proj[:, :d], qgain_ref[...]) * (head_dim ** -0.5 * LOG2E)
    k = head_norm(proj[:, d:2 * d], kgain_ref[...])
    v = proj[:, 2 * d:]
    q_ref, kv_ref = next(it), next(it)
    for hp in range(d // LANES):
        sl = slice(hp * LANES, (hp + 1) * LANES)
        q_ref[0, hp] = q[:, sl].astype(BF16)
        kv_ref[0, hp, :, :LANES] = k[:, sl].astype(BF16)
        kv_ref[0, hp, :, LANES:] = v[:, sl].astype(BF16)


def _stage(xs, ffn, *, pre=None, post=None, convert=(), ctx_rows, name):
    b, s, d = xs.shape
    tm = TOKEN_TILE
    mod, gain, w_in, w_out, which = ffn
    hidden = w_out.shape[-2]
    head_dim = d // NA_HEADS
    tok = pl.BlockSpec((1, tm, d), lambda i, t: (i, t, 0))
    modspec = pl.BlockSpec((1, 6, d), lambda i, t: (i, 0, 0))
    vec = pl.BlockSpec((1, d), lambda i, t: (0, 0))

    def weight(w, lead=()):
        return _resident((None,) * len(lead) + w.shape[len(lead):], lambda i, t: lead + (0, 0))

    args = [xs]
    if pre is not None and pre[0] == "join":
        assert ctx_rows == tm
        s += ctx_rows
        args.append(pre[1])
        in_specs = [pl.BlockSpec((1, tm, d), lambda i, t: (i, jnp.maximum(t - 1, 0), 0)),
                    pl.BlockSpec((1, tm, d), lambda i, t: (i, 0, 0))]
    elif pre is not None and pre[0] == "na":
        skip = ctx_rows
        s, ctx_rows = s - ctx_rows, 0
        tm = LATENT_TOKEN_TILE
        tok = pl.BlockSpec((1, tm, d), lambda i, t: (i, t, 0))
        assert tm % skip == 0
        in_specs = [pl.BlockSpec(tuple(pl.Element(n) for n in (1, tm, d)),
                                 lambda i, t: (i, (1 + t * (tm // skip)) * skip, 0))]
    else:
        in_specs = [tok]
    if pre is not None and pre[0] == "join":
        pass
    elif pre is not None and pre[0] == "s5":
        _, y, pmod, w_glu = pre
        args += [y, pmod, w_glu]
        in_specs += [tok, modspec, weight(w_glu)]
    elif pre is not None:
        _, attn, pmod, w_o = pre
        args += [attn, pmod, w_o]
        in_specs += [pl.BlockSpec((1, attn.shape[1], tm, LANES), lambda i, t: (i, 0, t, 0)),
                     modspec, weight(w_o)]
    args += [mod, gain, w_in, w_out]
    in_specs += [modspec, vec, weight(w_in, which), weight(w_out, which)]
    out_shape = [jax.ShapeDtypeStruct((b, s, d), F32)]
    out_specs = [tok]
    if post is not None and post[0] == "s5":
        _, qmod, qgain, w = post
        args += [qmod, qgain, w]
        in_specs += [modspec, vec, weight(w)]
        out_shape.append(jax.ShapeDtypeStruct((b, s, d), F32))
        out_specs.append(tok)
    elif post is not None:
        _, qmod, qgain, w, q_gain, k_gain = post
        hid = jnp.arange(d) // head_dim
        to_head = (hid[:, None] == jnp.arange(LANES)[None, :]).astype(BF16)
        from_head = jnp.concatenate([to_head.T, to_head.T], axis=0)
        args += [qmod, qgain, w, to_head, from_head,
                 jnp.tile(q_gain, NA_HEADS).reshape(1, d), jnp.tile(k_gain, NA_HEADS).reshape(1, d)]
        in_specs += [modspec, vec, weight(w), weight(to_head), weight(from_head), vec, vec]
        hp = d // LANES
        assert ctx_rows == tm
        out_shape += [jax.ShapeDtypeStruct((b, hp, s - ctx_rows, LANES), BF16),
                      jax.ShapeDtypeStruct((b, hp, s, 2 * LANES), BF16)]
        out_specs += [pl.BlockSpec((1, hp, tm, LANES), lambda i, t: (i, 0, jnp.maximum(t - 1, 0), 0)),
                      pl.BlockSpec((1, hp, tm, 2 * LANES), lambda i, t: (i, 0, t, 0))]
    n_t = s // tm
    for w, lead in convert:
        rows, cols = w.shape[-2:]
        chunk = rows // WEIGHT_CAST_CHUNKS
        assert chunk * WEIGHT_CAST_CHUNKS == rows and chunk % BF16_SUBLANES == 0
        assert b * n_t >= WEIGHT_CAST_CHUNKS

        def chunk_of(i, t):
            return jnp.minimum(i * n_t + t, WEIGHT_CAST_CHUNKS - 1)

        args.append(w)
        in_specs.append(pl.BlockSpec((None,) * len(lead) + (chunk, cols),
                                     lambda i, t, lead=lead: lead + (chunk_of(i, t), 0)))
        out_shape.append(jax.ShapeDtypeStruct((rows, cols), BF16))
        out_specs.append(pl.BlockSpec((chunk, cols), lambda i, t: (chunk_of(i, t), 0)))
    return pl.pallas_call(
        functools.partial(_stage_kernel, pre=None if pre is None else pre[0],
                          post=None if post is None else post[0], n_convert=len(convert),
                          ctx_rows=ctx_rows, hidden=hidden, head_dim=head_dim),
        grid=(b, n_t),
        in_specs=in_specs,
        out_specs=out_specs,
        out_shape=out_shape,
        compiler_params=_params("parallel", "arbitrary"),
        name=name,
    )(*args)


def _s5_prep_kernel(lr_ref, li_ref, ls_ref, btr_ref, bti_ref, cr_ref, ci_ref,
                    tpd_ref, wst_ref, prdt_ref, avec_ref, *, chunk, seg_chunks):
    n = LANES
    ng = S5_LANE_GROUPS
    half = n // 2
    wide = ng * half
    lr = jnp.minimum(lr_ref[0], S5_MIN_NEG_RE)
    li = li_ref[0]
    dt = jnp.exp(ls_ref[0])
    mag = jnp.exp(lr * dt)
    ar = mag * jnp.cos(li * dt)
    ai = mag * jnp.sin(li * dt)
    den = lr * lr + li * li
    zr = ((ar - 1.0) * lr + ai * li) / den
    zi = (ai * lr - (ar - 1.0) * li) / den
    btr, bti = btr_ref[0], bti_ref[0]
    bbr = zr * btr - zi * bti
    bbi = zr * bti + zi * btr
    cr, ci = cr_ref[0], ci_ref[0]

    lane = lax.broadcasted_iota(jnp.int32, (n, n), 1)
    row = lax.broadcasted_iota(jnp.int32, (n, n), 0)
    first_half = lane < half
    same_group = (row // S5_GROUP) == (lane // S5_GROUP)
    row_w = lax.broadcasted_iota(jnp.int32, (n, wide), 0)
    lane_w = lax.broadcasted_iota(jnp.int32, (n, wide), 1)
    own_states = (row_w // S5_GROUP) == (lane_w // half)

    def spread(v):
        return jnp.where(own_states, jnp.concatenate([v] * (wide // n), axis=1), 0.0).astype(BF16)

    powers = [(jnp.ones((n, n), F32), jnp.zeros((n, n), F32))]
    for _ in range(chunk):
        er, ei = powers[-1]
        powers.append((er * ar - ei * ai, er * ai + ei * ar))

    def input_map(k):
        er, ei = powers[k]
        return bbr * er - bbi * ei, bbr * ei + bbi * er

    def output_map(k):
        er, ei = powers[k]
        return cr * er - ci * ei, cr * ei + ci * er

    c2 = jnp.where(first_half, cr, -ci)
    lag_blocks = []
    for k in range(chunk):
        wr, wi = input_map(k)
        kd = _dot_nt_f32(jnp.where(first_half, wr, wi), c2)
        lag_blocks.append(jnp.where(same_group, kd, 0.0).astype(BF16))
    zero_block = jnp.zeros((n, n), BF16)

    def emit(reverse):
        for s in range(chunk):
            wr, wi = input_map(s if reverse else chunk - 1 - s)
            wst_ref[0, 0, s * n:(s + 1) * n, 0:wide] = spread(wr)
            wst_ref[0, 0, s * n:(s + 1) * n, wide:2 * wide] = spread(wi)
        for t in range(chunk):
            pr, pi = output_map(chunk - t if reverse else t + 1)
            prdt_ref[0, 0, t * n:(t + 1) * n, 0:wide] = spread(pr)
            prdt_ref[0, 0, t * n:(t + 1) * n, wide:2 * wide] = spread(-pi)
        for s in range(chunk):
            for t in range(chunk):
                lag = s - t if reverse else t - s
                tpd_ref[0, 0, s * n:(s + 1) * n, t * n:(t + 1) * n] = (
                    lag_blocks[lag] if lag >= 0 else zero_block)

    @pl.when(pl.program_id(0) == 0)
    def _():
        emit(False)

    @pl.when(pl.program_id(0) == 1)
    def _():
        emit(True)

    er, ei = powers[chunk]
    sr, si = jnp.ones((n, n), F32), jnp.zeros((n, n), F32)
    pr_, pi_ = er, ei
    e = seg_chunks
    while e:
        if e & 1:
            sr, si = sr * pr_ - si * pi_, sr * pi_ + si * pr_
        pr_, pi_ = pr_ * pr_ - pi_ * pi_, 2.0 * pr_ * pi_
        e >>= 1
    for r, v in enumerate((er, ei, sr, si)):
        for j in range(ng // 2):
            g0 = 2 * j * S5_GROUP
            g1 = g0 + S5_GROUP
            avec_ref[0, 0, r:r + 1, j * n:(j + 1) * n] = jnp.where(
                first_half[0:1], v[g0:g0 + 1], v[g1:g1 + 1])
    avec_ref[0, 0, 4:8, :] = jnp.zeros((4, wide), F32)


def _s5_prep(lam_re, lam_im, log_step, b_re, b_im, c_re, c_im, chunk, seg_chunks):
    nd, g, p = lam_re.shape
    gs = S5_GROUP
    assert 2 * p == LANES and g % S5_LANE_GROUPS == 0
    n_blocks = g // S5_LANE_GROUPS
    tk = chunk * LANES
    wide = S5_LANE_GROUPS * p

    def rows(v):
        v = v.reshape(nd, g * gs, p)
        return jnp.concatenate([v, v], axis=-1)

    def per_group(v):
        return rows(jnp.broadcast_to(v[:, :, None, :], (nd, g, gs, p)))

    args = (per_group(lam_re), per_group(lam_im),
            per_group(jnp.broadcast_to(log_step[:, :, None], (nd, g, p))),
            rows(jnp.swapaxes(b_re, 2, 3)), rows(jnp.swapaxes(b_im, 2, 3)), rows(c_re), rows(c_im))
    sq = pl.BlockSpec((1, LANES, LANES), lambda d, i: (d, i, 0))
    big = pl.BlockSpec((1, 1, tk, tk), lambda d, i: (d, i, 0, 0))
    assert tk == 2 * wide
    return pl.pallas_call(
        functools.partial(_s5_prep_kernel, chunk=chunk, seg_chunks=seg_chunks),
        grid=(nd, n_blocks),
        in_specs=[sq] * 7,
        out_specs=[big, big, big, pl.BlockSpec((1, 1, 8, wide), lambda d, i: (d, i, 0, 0))],
        out_shape=[jax.ShapeDtypeStruct((nd, n_blocks, tk, tk), BF16)] * 3
                  + [jax.ShapeDtypeStruct((nd, n_blocks, 8, wide), F32)],
        compiler_params=_params("parallel", "parallel"),
        name="s5_prep",
    )(*args)


def _s5_scan_kernel(u_ref, dsk_ref, tpd_ref, wst_ref, prdt_ref, avec_ref, y_ref,
                    xcat, sv, *, chunk, n_ctx):
    n = LANES
    n_c = xcat.shape[0]
    nb = sv.shape[0] // 2
    n_seg = SUBLANES
    seg = n_c // n_seg
    n_lat = n_c - n_ctx
    direction = pl.program_id(2)

    @pl.when(direction == 0)
    def _():
        for s in range(chunk):
            xcat[:, s * n:(s + 1) * n] = u_ref[0, pl.ds(s, n_c, stride=chunk), :].astype(BF16)

    def run(reverse):
        wst = wst_ref[0, 0]

        def summarise(dst, src):
            s = _dot(xcat[src, :], wst)
            for j in range(2 * nb):
                sv[j, dst, :] = s[:, j * n:(j + 1) * n]

        if reverse:
            summarise(slice(0, n_lat), slice(n_ctx, n_c))
            summarise(slice(n_lat, n_c), slice(0, n_ctx))
        else:
            summarise(slice(0, n_c), slice(0, n_c))

        def lane_blocks(r):
            return [avec_ref[0, 0, r:r + 1, j * n:(j + 1) * n] for j in range(nb)]

        ar = [jnp.broadcast_to(v, (n_seg, n)) for v in lane_blocks(0)]
        ai = [jnp.broadcast_to(v, (n_seg, n)) for v in lane_blocks(1)]

        def rows_at(i):
            return pl.ds(seg - 1 - i if reverse else i, n_seg, stride=seg)

        def advance(h, rows):
            out_r, out_i = [], []
            for j in range(nb):
                hr, hi = h[j], h[nb + j]
                out_r.append(ar[j] * hr - ai[j] * hi + sv[j, rows, :])
                out_i.append(ar[j] * hi + ai[j] * hr + sv[nb + j, rows, :])
            return tuple(out_r + out_i)

        zero = jnp.zeros((n_seg, n), F32)
        fin = lax.fori_loop(0, seg, lambda i, h: advance(h, rows_at(i)), (zero,) * (2 * nb))

        asr, asi = lane_blocks(2), lane_blocks(3)
        order = range(n_seg - 1, -1, -1) if reverse else range(n_seg)
        h0 = []
        for j in range(nb):
            rows_r, rows_i = [None] * n_seg, [None] * n_seg
            pr = pi = jnp.zeros((1, n), F32)
            prev = None
            for sgm in order:
                if prev is not None:
                    pr, pi = (asr[j] * pr - asi[j] * pi + fin[j][prev:prev + 1],
                              asr[j] * pi + asi[j] * pr + fin[nb + j][prev:prev + 1])
                rows_r[sgm], rows_i[sgm] = pr, pi
                prev = sgm
            h0.append((jnp.concatenate(rows_r, axis=0), jnp.concatenate(rows_i, axis=0)))
        h0 = tuple(v[0] for v in h0) + tuple(v[1] for v in h0)

        def step(i, h):
            rows = rows_at(i)
            nxt = advance(h, rows)
            for j in range(2 * nb):
                sv[j, rows, :] = h[j]
            return nxt

        lax.fori_loop(0, seg, step, h0)

        def entry_states(rows):
            return jnp.concatenate([sv[j, rows, :] for j in range(2 * nb)], axis=1).astype(BF16)

        if reverse:
            h_all = jnp.concatenate([entry_states(slice(n_lat, n_c)), entry_states(slice(0, n_lat))],
                                    axis=0)
        else:
            h_all = entry_states(slice(0, n_c))
        blk = MXU_DIM
        n_blk = xcat.shape[1] // blk
        intra = []
        for j in range(n_blk):
            ks = slice(j * blk, n_blk * blk) if reverse else slice(0, (j + 1) * blk)
            intra.append(_dot(xcat[:, ks], tpd_ref[0, 0, ks, j * blk:(j + 1) * blk]))
        return jnp.concatenate(intra, axis=1) + _dot_nt(h_all, prdt_ref[0, 0])

    @pl.when(direction == 0)
    def _():
        y = run(False)
        for t in range(chunk):
            rows = pl.ds(t, n_c, stride=chunk)
            y_ref[0, rows, :] = y[:, t * n:(t + 1) * n] + u_ref[0, rows, :] * dsk_ref[...]

    @pl.when(direction == 1)
    def _():
        y = run(True)
        for t in range(chunk):
            rows = pl.ds(t, n_c, stride=chunk)
            y_ref[0, rows, :] = y_ref[0, rows, :] + y[:, t * n:(t + 1) * n]


def _s5_scan(u, d_skip, mats, chunk, ctx_rows):
    b, s, d = u.shape
    tpd, wst, prdt, avec = mats
    nd, n_blocks, tk, _ = tpd.shape
    wide = avec.shape[-1]
    n_c = s // chunk

    def mat(shape):
        return pl.BlockSpec((1, 1) + shape, lambda o, i, dr: (dr, o, 0, 0))

    tok = pl.BlockSpec((1, s, LANES), lambda o, i, dr: (i, 0, o))
    return pl.pallas_call(
        functools.partial(_s5_scan_kernel, chunk=chunk, n_ctx=ctx_rows // chunk),
        grid=(n_blocks, b, nd),
        in_specs=[tok, pl.BlockSpec((1, LANES), lambda o, i, dr: (0, o)),
                  mat((tk, tk)), mat((tk, tk)), mat((tk, tk)), mat((8, wide))],
        out_specs=tok,
        out_shape=jax.ShapeDtypeStruct((b, s, d), F32),
        scratch_shapes=[pltpu.VMEM((n_c, tk), BF16),
                        pltpu.VMEM((2 * wide // LANES, n_c, LANES), F32)],
        compiler_params=_params("parallel", "parallel", "arbitrary"),
        name="s5_scan",
    )(u, d_skip, tpd, wst, prdt, avec)


def _window_start(r, rows):
    kh = min(WIN_H, rows)
    return min(max(r - kh // 2, 0), rows - kh)


def _na_tile_geometry(kind, rows):
    n_tiles = rows // NA_Q_ROWS
    tile = {0: 0, 1: 1, 2: n_tiles - 1}[kind]
    q0 = tile * NA_Q_ROWS
    k0 = min(max(q0 - (NA_K_ROWS - NA_Q_ROWS) // 2, 0), rows - NA_K_ROWS)
    return q0, k0


def _na_sub_window(kind, sub, rows):
    q0, k0 = _na_tile_geometry(kind, rows)
    kh = min(WIN_H, rows)
    starts = [_window_start(q0 + NA_SUB_Q_ROWS * sub + i, rows) - k0 for i in range(NA_SUB_Q_ROWS)]
    first = min(min(starts) // NA_SUB_Q_ROWS * NA_SUB_Q_ROWS, NA_K_ROWS - NA_SUB_K_ROWS)
    assert first >= 0 and max(starts) + kh <= first + NA_SUB_K_ROWS
    return first


def _na_bias_kernel(rpb_ref, o_ref, *, rows):
    h = pl.program_id(0)
    w = GRID_W
    kh = min(WIN_H, rows)
    ncol = 2 * WIN_W - 1
    nrow = 2 * WIN_H - 1
    cq = lax.broadcasted_iota(jnp.int32, (w, LANES), 0)
    lane = lax.broadcasted_iota(jnp.int32, (w, LANES), 1)
    ck = lane % w
    left = lane < w
    cs = jnp.clip(cq - WIN_W // 2, 0, w - WIN_W)
    col_ok = (ck >= cs) & (ck < cs + WIN_W)
    dc = jnp.clip(ck - cq + WIN_W - 1, 0, ncol - 1)
    neg = jnp.full((w, LANES), -jnp.inf, F32)

    def pair_table(d_left, d_right):
        t = jnp.zeros((w, LANES), F32)
        for j in range(ncol):
            vl = rpb_ref[(h * nrow + d_left) * ncol + j] if d_left is not None else 0.0
            vr = rpb_ref[(h * nrow + d_right) * ncol + j] if d_right is not None else 0.0
            t = jnp.where(dc == j, jnp.where(left, vl, vr), t)
        t = t * LOG2E
        ok = col_ok
        if d_left is None:
            ok = ok & jnp.logical_not(left)
        if d_right is None:
            ok = ok & left
        return jnp.where(ok, t, neg)

    cache = {}
    for kind in range(3):
        q0, k0 = _na_tile_geometry(kind, rows)
        for rq in range(NA_Q_ROWS):
            r = q0 + rq
            rs = _window_start(r, rows)
            first = k0 + _na_sub_window(kind, rq // NA_SUB_Q_ROWS, rows)
            for m in range(NA_SUB_K_ROWS // 2):
                ds = []
                for kr in (first + 2 * m, first + 2 * m + 1):
                    ds.append(kr - r + WIN_H - 1 if rs <= kr < rs + kh else None)
                key = tuple(ds)
                if key == (None, None):
                    blk = neg
                else:
                    if key not in cache:
                        cache[key] = pair_table(*key)
                    blk = cache[key]
                o_ref[kind, 0, rq * w:(rq + 1) * w, m * LANES:(m + 1) * LANES] = blk


def _na_bias(rpb, rows):
    nh = rpb.shape[0]
    nq, nk = NA_Q_ROWS * GRID_W, NA_SUB_K_ROWS * GRID_W
    return pl.pallas_call(
        functools.partial(_na_bias_kernel, rows=rows),
        grid=(nh,),
        in_specs=[pl.BlockSpec(memory_space=pltpu.SMEM)],
        out_specs=pl.BlockSpec((3, 1, nq, nk), lambda h: (0, h, 0, 0)),
        out_shape=jax.ShapeDtypeStruct((3, nh, nq, nk), F32),
        compiler_params=_params("parallel"),
        name="na_bias",
    )(rpb.reshape(-1))


def _na_attn_kernel(*refs, n_q, n_tiles):
    n_sub = n_tiles * n_q
    q_ref = refs[0]
    kv_refs = refs[1:1 + n_sub]
    kvc_ref = refs[1 + n_sub]
    bias_refs = refs[2 + n_sub:2 + n_sub + n_tiles]
    o_ref = refs[2 + n_sub + n_tiles]
    tb = kvc_ref.shape[2]
    half = LANES // 2

    def per_head(x, other):
        first_head = lax.broadcasted_iota(jnp.int32, x.shape, 1) < half
        fill = jnp.full_like(x, other)
        return [jnp.where(first_head, x, fill), jnp.where(first_head, fill, x)]

    kc = kvc_ref[0, 0, :, :LANES]
    vc_heads = per_head(kvc_ref[0, 0, :, LANES:], 1.0)
    units = [(sub, hh) for sub in range(n_sub) for hh in range(2)]
    keys, values, queries = {}, {}, {}
    for sub in range(n_sub):
        keys[sub] = kv_refs[sub][0, 0, :, :LANES]
        values[sub] = per_head(kv_refs[sub][0, 0, :, LANES:], 1.0)
        queries[sub] = per_head(q_ref[0, 0, sub * tb:(sub + 1) * tb, :], 0.0)

    def score(unit):
        sub, hh = unit
        qh = queries[sub][hh]
        in_tile = sub % n_q
        bias = bias_refs[sub // n_q][0, hh, in_tile * tb:(in_tile + 1) * tb, :]
        return _dot_nt(qh, keys[sub]) + bias, _dot_nt(qh, kc)

    ahead = 2
    pending = [score(u) for u in units[:ahead]]
    outs = {}
    for i, (sub, hh) in enumerate(units):
        s, sc = pending.pop(0)
        if i + ahead < len(units):
            pending.append(score(units[i + ahead]))
        m = jnp.maximum(jnp.max(s, axis=-1, keepdims=True), jnp.max(sc, axis=-1, keepdims=True))
        p = jnp.exp2(s - m).astype(BF16)
        pc = jnp.exp2(sc - m).astype(BF16)
        o = _dot(p, values[sub][hh]) + _dot(pc, vc_heads[hh])
        denom = o[:, half:half + 1] if hh == 0 else o[:, 0:1]
        outs[sub, hh] = o / denom
    first_head = lax.broadcasted_iota(jnp.int32, (tb, LANES), 1) < half
    for sub in range(n_sub):
        o_ref[0, 0, sub * tb:(sub + 1) * tb, :] = jnp.where(
            first_head, outs[sub, 0], outs[sub, 1]).astype(BF16)


def _na_attention(q, kv, bias, ctx_rows):
    b, hp, seq, _ = q.shape
    tb = TOKEN_TILE
    assert ctx_rows == tb and kv.shape[2] == ctx_rows + seq
    rows = seq // GRID_W
    n_tiles = rows // NA_Q_ROWS
    assert n_tiles >= 3
    assert NA_SUB_Q_ROWS * GRID_W == tb
    n_q = NA_Q_ROWS // NA_SUB_Q_ROWS
    n_k = NA_SUB_K_ROWS * GRID_W // tb
    slab = NA_K_ROWS * GRID_W // tb
    last_k0 = (rows - NA_K_ROWS) * GRID_W // tb

    per_step = NA_TILES_PER_STEP
    assert n_tiles % per_step == 0

    def pick(tile, by_kind):
        return jnp.where(tile == 0, by_kind[0], jnp.where(tile == n_tiles - 1, by_kind[2], by_kind[1]))

    def window_map(j, sub):
        offs = [_na_sub_window(kind, sub, rows) * GRID_W // tb for kind in range(3)]

        def f(p, bi, t):
            tile = per_step * t + j
            first = jnp.clip(n_q * tile - (slab - n_q) // 2, 0, last_k0)
            return (bi, p, (1 + first + pick(tile, offs)) * tb, 0)
        return f

    def bias_map(j):
        return lambda p, bi, t: (pick(per_step * t + j, (0, 1, 2)), p, 0, 0)

    nq = per_step * NA_Q_ROWS * GRID_W
    step_spec = pl.BlockSpec((1, 1, nq, LANES), lambda p, bi, t: (bi, p, t, 0))
    window_shape = tuple(pl.Element(n) for n in (1, 1, n_k * tb, 2 * LANES))
    kv_specs = [pl.BlockSpec(window_shape, window_map(j, sub))
                for j in range(per_step) for sub in range(n_q)]
    bias_specs = [pl.BlockSpec((1, 2) + bias.shape[2:], bias_map(j)) for j in range(per_step)]
    in_specs = ([step_spec] + kv_specs
                + [pl.BlockSpec((1, 1, tb, 2 * LANES), lambda p, bi, t: (bi, p, 0, 0))] + bias_specs)
    args = [q] + [kv] * (len(kv_specs) + 1) + [bias] * per_step
    return pl.pallas_call(
        functools.partial(_na_attn_kernel, n_q=n_q, n_tiles=per_step),
        grid=(hp, b, n_tiles // per_step),
        in_specs=in_specs,
        out_specs=step_spec,
        out_shape=jax.ShapeDtypeStruct((b, hp, seq, LANES), BF16),
        compiler_params=_params("parallel", "parallel", "parallel"),
        name="na_attention",
    )(*args)


def kernel(x, c, ctx, c_ctx, norm_g, ada_w, ada_b, ffn_w_in, ffn_w_out, ssm_w_in, ssm_lambda_re, ssm_lambda_im, ssm_log_step, ssm_b_re, ssm_b_im, ssm_c_re, ssm_c_im, ssm_d, ssm_w_glu, na_w_qkv, na_q_norm, na_k_norm, na_rpb, na_w_o):
    b, seq, d = x.shape
    ctx_rows = ctx.shape[1]
    depth = norm_g.shape[0]
    assert b + 1 <= 8 and ctx_rows == TOKEN_TILE and depth == 2

    cvec = jnp.zeros((8, d), F32).at[:b].set(c).at[b].set(c_ctx)
    m = _ada_modulation(cvec, b + 1, ada_w, ada_b).reshape(depth, 8, N_SUB, 3, d)

    def mod_rows(layer, sub):
        lat = m[layer, :b, sub]
        cx = jnp.broadcast_to(m[layer, b, sub], (b, 3, d))
        return jnp.concatenate([cx, lat], axis=1)

    def gain(layer, sub):
        return norm_g[layer, sub].reshape(1, d)

    def ffn(layer, which, w_in, w_out):
        sub = 2 * which
        return mod_rows(layer, sub), gain(layer, sub), w_in, w_out, ()

    def ffn_f32(layer, which):
        return [(ffn_w_in, (layer, which)), (ffn_w_out, (layer, which))]

    xs, u, w_in, w_out, w_glu = _stage(
        x, ffn(0, 0, ffn_w_in[0, 0].astype(BF16), ffn_w_out[0, 0].astype(BF16)),
        pre=("join", ctx), post=("s5", mod_rows(0, 1), gain(0, 1), ssm_w_in[0].astype(BF16)),
        convert=ffn_f32(0, 1) + [(ssm_w_glu, (0,))], ctx_rows=ctx_rows, name="ffn_s5in")
    n_chunks = (ctx_rows + seq) // S5_CHUNK
    assert n_chunks % SUBLANES == 0 and ctx_rows % S5_CHUNK == 0
    mats = _s5_prep(ssm_lambda_re[0], ssm_lambda_im[0], ssm_log_step[0], ssm_b_re[0], ssm_b_im[0],
                    ssm_c_re[0], ssm_c_im[0], S5_CHUNK, n_chunks // SUBLANES)
    y = _s5_scan(u, ssm_d[0].reshape(1, d), mats, S5_CHUNK, ctx_rows)
    xs, w_in, w_out, w_qkv = _stage(
        xs, ffn(0, 1, w_in, w_out), pre=("s5", y, mod_rows(0, 1), w_glu),
        convert=ffn_f32(1, 0) + [(na_w_qkv, (0,))], ctx_rows=ctx_rows, name="s5out_ffn")

    xs, q, kv, w_in, w_out, w_o = _stage(
        xs, ffn(1, 0, w_in, w_out),
        post=("qkv", mod_rows(1, 1), gain(1, 1), w_qkv, na_q_norm[0], na_k_norm[0]),
        convert=ffn_f32(1, 1) + [(na_w_o, (0,))], ctx_rows=ctx_rows, name="ffn_qkv")
    bias = _na_bias(na_rpb[0], seq // GRID_W)
    attn = _na_attention(q, kv, bias, ctx_rows)
    out, = _stage(xs, ffn(1, 1, w_in, w_out), pre=("na", attn, mod_rows(1, 1), w_o),
                  ctx_rows=ctx_rows, name="naout_ffn")
    return out
```

```python
import functools
import math

import jax
import jax.numpy as jnp
from jax import lax
from jax.experimental import pallas as pl
from jax.experimental.pallas import tpu as pltpu

F32 = jnp.float32
BF16 = jnp.bfloat16

GRID_W = 64
N_SUB = 3
MACARON_WEIGHT = 0.5
RMS_EPS = 1e-6
S5_GROUP = 16
S5_STATE = 64
S5_MIN_NEG_RE = -1e-4
NA_HEADS = 16
WIN_H = 8
WIN_W = 16

LANES = 128
SUBLANES = 8
MXU_DIM = 256
VMEM_LIMIT = 56 * 1024 * 1024

TOKEN_TILE = 256
S5_STAGE_TILE = 384
LATENT_TOKEN_TILE = 512
WEIGHT_CAST_CHUNKS = 16
BF16_SUBLANES = 16
S5_CHUNK = 8
S5_LANE_GROUPS = LANES // S5_GROUP
NA_Q_ROWS = 8
NA_K_ROWS = 16
NA_TILES_PER_STEP = 4
NA_SUB_Q_ROWS = 4
NA_SUB_K_ROWS = 12
LOG2E = math.log2(math.e)


def _dot(a, b):
    return jnp.dot(a, b, preferred_element_type=F32)


def _dot_nt(a, b):
    return lax.dot_general(a, b, (((1,), (1,)), ((), ())), preferred_element_type=F32)


def _dot_nt_f32(a, b):
    return lax.dot_general(a, b, (((1,), (1,)), ((), ())), preferred_element_type=F32,
                           precision=lax.Precision.HIGHEST)


def _split_bf16(x):
    hi = x.astype(BF16)
    return hi, (x - hi.astype(F32)).astype(BF16)


def _params(*sem):
    return pltpu.CompilerParams(dimension_semantics=sem, vmem_limit_bytes=VMEM_LIMIT)


def _resident(shape, index_map):
    return pl.BlockSpec(shape, index_map, pipeline_mode=pl.Buffered(1))


def _ada_kernel(ct_ref, w_ref, b_ref, o_ref, *, n_rows):
    ct = ct_ref[...]
    s = ct * jax.nn.sigmoid(ct)
    w = w_ref[0]
    rows = [jnp.sum(w * s[:, r:r + 1], axis=0, keepdims=True) + b_ref[0] for r in range(n_rows)]
    rows.append(jnp.zeros((SUBLANES - n_rows, w.shape[1]), F32))
    o_ref[0] = jnp.concatenate(rows, axis=0)


def _ada_modulation(cvec, n_rows, ada_w, ada_b):
    depth, d, n = ada_w.shape
    tn = d
    return pl.pallas_call(
        functools.partial(_ada_kernel, n_rows=n_rows),
        grid=(depth, n // tn),
        in_specs=[pl.BlockSpec((d, SUBLANES), lambda l, j: (0, 0)),
                  pl.BlockSpec((1, d, tn), lambda l, j: (l, 0, j)),
                  pl.BlockSpec((1, 1, tn), lambda l, j: (l, 0, j))],
        out_specs=pl.BlockSpec((1, SUBLANES, tn), lambda l, j: (l, 0, j)),
        out_shape=jax.ShapeDtypeStruct((depth, SUBLANES, n), F32),
        compiler_params=_params("parallel", "parallel"),
        name="ada_modulation",
    )(cvec.T, ada_w, ada_b.reshape(depth, 1, n))


def _row_is_context(tm, first_row, ctx_rows):
    return (first_row + lax.broadcasted_iota(jnp.int32, (tm, 1), 0)) < ctx_rows


def _modulated_norm(x, g, mod, first_row, ctx_rows):
    tm = x.shape[0]
    y = x * lax.rsqrt(jnp.mean(x * x, axis=-1, keepdims=True) + RMS_EPS) * g
    if ctx_rows == 0:
        shift, scale, gate = mod[3:4], mod[4:5], mod[5:6]
    else:
        is_ctx = _row_is_context(tm, first_row, ctx_rows)
        shift = jnp.where(is_ctx, mod[0:1], mod[3:4])
        scale = jnp.where(is_ctx, mod[1:2], mod[4:5])
        gate = jnp.where(is_ctx, mod[2:3], mod[5:6])
    return y * (1.0 + scale) + shift, gate


def _gelu_tanh(y):
    return 0.5 * y * (1.0 + jnp.tanh(math.sqrt(2.0 / math.pi) * (y + 0.044715 * (y * y * y))))


def _stage_kernel(*refs, pre, post, n_convert, ctx_rows, hidden, head_dim):
    it = iter(refs)
    x_ref = next(it)
    tm, d = x_ref.shape[1], x_ref.shape[2]
    first_row = pl.program_id(1) * tm
    x = x_ref[0]
    if pre == "join":
        head = next(it)[0]
        if ctx_rows < tm:
            head = jnp.concatenate([head, x[:tm - ctx_rows]], axis=0)
        x = jnp.where(pl.program_id(1) == 0, head, x)

    def gate_of(mod):
        if ctx_rows == 0:
            return mod[5:6]
        return jnp.where(_row_is_context(tm, first_row, ctx_rows), mod[2:3], mod[5:6])

    if pre == "s5":
        y_ref, pmod_ref, wglu_ref = next(it), next(it), next(it)
        z = _dot(_gelu_tanh(y_ref[0]).astype(BF16), wglu_ref[...])
        x = x + gate_of(pmod_ref[0]) * (z[:, :d] * jax.nn.sigmoid(z[:, d:]))
    elif pre == "na":
        a_ref, pmod_ref, wo_ref = next(it), next(it), next(it)
        a = jnp.concatenate([a_ref[0, p] for p in range(a_ref.shape[1])], axis=-1)
        x = x + gate_of(pmod_ref[0]) * _dot(a, wo_ref[...])

    mod_ref, g_ref, win_ref, wout_ref = next(it), next(it), next(it), next(it)
    h, gate = _modulated_norm(x, g_ref[...], mod_ref[0], first_row, ctx_rows)
    hb = h.astype(BF16)
    gt = _dot(hb, win_ref[:, :hidden])
    up = _dot(hb, win_ref[:, hidden:])
    act = (gt * jax.nn.sigmoid(gt) * up).astype(BF16)
    x = x + (MACARON_WEIGHT * gate) * _dot(act, wout_ref[...])

    if post == "s5":
        qmod_ref, qg_ref, w_ref = next(it), next(it), next(it)
    elif post == "qkv":
        (qmod_ref, qg_ref, w_ref, reduce_ref, expand_ref,
         qgain_ref, kgain_ref) = (next(it) for _ in range(7))

    cast_in = [next(it) for _ in range(n_convert)]
    cast_out = refs[len(refs) - n_convert:]

    @pl.when(pl.program_id(0) * pl.num_programs(1) + pl.program_id(1) < WEIGHT_CAST_CHUNKS)
    def _():
        for src, dst in zip(cast_in, cast_out):
            dst[...] = src[...].astype(BF16)

    o_ref = next(it)
    o_ref[0] = x
    if post is None:
        return
    h2, _ = _modulated_norm(x, qg_ref[...], qmod_ref[0], first_row, ctx_rows)
    proj = _dot(h2.astype(BF16), w_ref[...])
    if post == "s5":
        next(it)[0] = proj
        return

    def head_norm(z, gain):
        ms = _dot((z * z).astype(BF16), reduce_ref[...]) * (1.0 / head_dim)
        hi, lo = _split_bf16(lax.rsqrt(ms + RMS_EPS))
        return z * _dot(jnp.concatenate([hi, lo], axis=1), expand_ref[...]) * gain

    q = head_norm(proj[:, :d], qgain_ref[...]) * (head_dim ** -0.5 * LOG2E)
    k = head_norm(proj[:, d:2 * d], kgain_ref[...])
    v = proj[:, 2 * d:]
    q_ref, kv_ref = next(it), next(it)
    for hp in range(d // LANES):
        sl = slice(hp * LANES, (hp + 1) * LANES)
        q_ref[0, hp] = q[:, sl].astype(BF16)
        kv_ref[0, hp, :, :LANES] = k[:, sl].astype(BF16)
        kv_ref[0, hp, :, LANES:] = v[:, sl].astype(BF16)


def _stage(xs, ffn, *, pre=None, post=None, convert=(), ctx_rows, tile, name):
    b, s, d = xs.shape
    tm = tile
    mod, gain, w_in, w_out, which = ffn
    hidden = w_out.shape[-2]
    head_dim = d // NA_HEADS
    tok = pl.BlockSpec((1, tm, d), lambda i, t: (i, t, 0))
    modspec = pl.BlockSpec((1, 6, d), lambda i, t: (i, 0, 0))
    vec = pl.BlockSpec((1, d), lambda i, t: (0, 0))

    def weight(w, lead=()):
        return _resident((None,) * len(lead) + w.shape[len(lead):], lambda i, t: lead + (0, 0))

    def rows_from(offset):
        step = math.gcd(tm, ctx_rows)
        n_tile, n_ctx = tm // step, ctx_rows // step
        return pl.BlockSpec(tuple(pl.Element(n) for n in (1, tm, d)),
                            lambda i, t: (i, offset(t, n_tile, n_ctx) * step, 0))

    args = [xs]
    if pre is not None and pre[0] == "join":
        assert tm >= ctx_rows
        s += ctx_rows
        args.append(pre[1])
        in_specs = [rows_from(lambda t, n, c: jnp.maximum(t * n - c, 0)),
                    pl.BlockSpec((1, ctx_rows, d), lambda i, t: (i, 0, 0))]
    elif pre is not None and pre[0] == "na":
        in_specs = [rows_from(lambda t, n, c: t * n + c)]
        s, ctx_rows = s - ctx_rows, 0
    else:
        in_specs = [tok]
    if pre is not None and pre[0] == "join":
        pass
    elif pre is not None and pre[0] == "s5":
        _, y, pmod, w_glu = pre
        args += [y, pmod, w_glu]
        in_specs += [tok, modspec, weight(w_glu)]
    elif pre is not None:
        _, attn, pmod, w_o = pre
        args += [attn, pmod, w_o]
        in_specs += [pl.BlockSpec((1, attn.shape[1], tm, LANES), lambda i, t: (i, 0, t, 0)),
                     modspec, weight(w_o)]
    args += [mod, gain, w_in, w_out]
    in_specs += [modspec, vec, weight(w_in, which), weight(w_out, which)]
    out_shape = [jax.ShapeDtypeStruct((b, s, d), F32)]
    out_specs = [tok]
    if post is not None and post[0] == "s5":
        _, qmod, qgain, w = post
        args += [qmod, qgain, w]
        in_specs += [modspec, vec, weight(w)]
        out_shape.append(jax.ShapeDtypeStruct((b, s, d), F32))
        out_specs.append(tok)
    elif post is not None:
        _, qmod, qgain, w, q_gain, k_gain = post
        hid = jnp.arange(d) // head_dim
        to_head = (hid[:, None] == jnp.arange(LANES)[None, :]).astype(BF16)
        from_head = jnp.concatenate([to_head.T, to_head.T], axis=0)
        args += [qmod, qgain, w, to_head, from_head,
                 jnp.tile(q_gain, NA_HEADS).reshape(1, d), jnp.tile(k_gain, NA_HEADS).reshape(1, d)]
        in_specs += [modspec, vec, weight(w), weight(to_head), weight(from_head), vec, vec]
        hp = d // LANES
        out_shape += [jax.ShapeDtypeStruct((b, hp, s, LANES), BF16),
                      jax.ShapeDtypeStruct((b, hp, s, 2 * LANES), BF16)]
        out_specs += [pl.BlockSpec((1, hp, tm, LANES), lambda i, t: (i, 0, t, 0)),
                      pl.BlockSpec((1, hp, tm, 2 * LANES), lambda i, t: (i, 0, t, 0))]
    n_t = s // tm
    for w, lead in convert:
        rows, cols = w.shape[-2:]
        chunk = rows // WEIGHT_CAST_CHUNKS
        assert chunk * WEIGHT_CAST_CHUNKS == rows and chunk % BF16_SUBLANES == 0
        assert b * n_t >= WEIGHT_CAST_CHUNKS

        def chunk_of(i, t):
            return jnp.minimum(i * n_t + t, WEIGHT_CAST_CHUNKS - 1)

        args.append(w)
        in_specs.append(pl.BlockSpec((None,) * len(lead) + (chunk, cols),
                                     lambda i, t, lead=lead: lead + (chunk_of(i, t), 0)))
        out_shape.append(jax.ShapeDtypeStruct((rows, cols), BF16))
        out_specs.append(pl.BlockSpec((chunk, cols), lambda i, t: (chunk_of(i, t), 0)))
    return pl.pallas_call(
        functools.partial(_stage_kernel, pre=None if pre is None else pre[0],
                          post=None if post is None else post[0], n_convert=len(convert),
                          ctx_rows=ctx_rows, hidden=hidden, head_dim=head_dim),
        grid=(b, n_t),
        in_specs=in_specs,
        out_specs=out_specs,
        out_shape=out_shape,
        compiler_params=_params("arbitrary", "arbitrary"),
        name=name,
    )(*args)


def _s5_prep_kernel(lr_ref, li_ref, ls_ref, btr_ref, bti_ref, cr_ref, ci_ref,
                    tpd_ref, wst_ref, prdt_ref, avec_ref, *, chunk, seg_chunks):
    n = LANES
    ng = S5_LANE_GROUPS
    half = n // 2
    wide = ng * half
    lr = jnp.minimum(lr_ref[0], S5_MIN_NEG_RE)
    li = li_ref[0]
    dt = jnp.exp(ls_ref[0])
    mag = jnp.exp(lr * dt)
    ar = mag * jnp.cos(li * dt)
    ai = mag * jnp.sin(li * dt)
    den = lr * lr + li * li
    zr = ((ar - 1.0) * lr + ai * li) / den
    zi = (ai * lr - (ar - 1.0) * li) / den
    btr, bti = btr_ref[0], bti_ref[0]
    bbr = zr * btr - zi * bti
    bbi = zr * bti + zi * btr
    cr, ci = cr_ref[0], ci_ref[0]

    lane = lax.broadcasted_iota(jnp.int32, (n, n), 1)
    row = lax.broadcasted_iota(jnp.int32, (n, n), 0)
    first_half = lane < half
    same_group = (row // S5_GROUP) == (lane // S5_GROUP)
    row_w = lax.broadcasted_iota(jnp.int32, (n, wide), 0)
    lane_w = lax.broadcasted_iota(jnp.int32, (n, wide), 1)
    own_states = (row_w // S5_GROUP) == (lane_w // half)

    def spread(v):
        return jnp.where(own_states, jnp.concatenate([v] * (wide // n), axis=1), 0.0).astype(BF16)

    powers = [(jnp.ones((n, n), F32), jnp.zeros((n, n), F32))]
    for _ in range(chunk):
        er, ei = powers[-1]
        powers.append((er * ar - ei * ai, er * ai + ei * ar))

    def input_map(k):
        er, ei = powers[k]
        return bbr * er - bbi * ei, bbr * ei + bbi * er

    def output_map(k):
        er, ei = powers[k]
        return cr * er - ci * ei, cr * ei + ci * er

    c2 = jnp.where(first_half, cr, -ci)
    lag_blocks = []
    for k in range(chunk):
        wr, wi = input_map(k)
        kd = _dot_nt_f32(jnp.where(first_half, wr, wi), c2)
        lag_blocks.append(jnp.where(same_group, kd, 0.0).astype(BF16))
    zero_block = jnp.zeros((n, n), BF16)

    def emit(reverse):
        for s in range(chunk):
            wr, wi = input_map(s if reverse else chunk - 1 - s)
            wst_ref[0, 0, s * n:(s + 1) * n, 0:wide] = spread(wr)
            wst_ref[0, 0, s * n:(s + 1) * n, wide:2 * wide] = spread(wi)
        for t in range(chunk):
            pr, pi = output_map(chunk - t if reverse else t + 1)
            prdt_ref[0, 0, t * n:(t + 1) * n, 0:wide] = spread(pr)
            prdt_ref[0, 0, t * n:(t + 1) * n, wide:2 * wide] = spread(-pi)
        for s in range(chunk):
            for t in range(chunk):
                lag = s - t if reverse else t - s
                tpd_ref[0, 0, s * n:(s + 1) * n, t * n:(t + 1) * n] = (
                    lag_blocks[lag] if lag >= 0 else zero_block)

    @pl.when(pl.program_id(0) == 0)
    def _():
        emit(False)

    @pl.when(pl.program_id(0) == 1)
    def _():
        emit(True)

    er, ei = powers[chunk]
    sr, si = jnp.ones((n, n), F32), jnp.zeros((n, n), F32)
    pr_, pi_ = er, ei
    e = seg_chunks
    while e:
        if e & 1:
            sr, si = sr * pr_ - si * pi_, sr * pi_ + si * pr_
        pr_, pi_ = pr_ * pr_ - pi_ * pi_, 2.0 * pr_ * pi_
        e >>= 1
    for r, v in enumerate((er, ei, sr, si)):
        for j in range(ng // 2):
            g0 = 2 * j * S5_GROUP
            g1 = g0 + S5_GROUP
            avec_ref[0, 0, r:r + 1, j * n:(j + 1) * n] = jnp.where(
                first_half[0:1], v[g0:g0 + 1], v[g1:g1 + 1])
    avec_ref[0, 0, 4:8, :] = jnp.zeros((4, wide), F32)


def _s5_prep(lam_re, lam_im, log_step, b_re, b_im, c_re, c_im, chunk, seg_chunks):
    nd, g, p = lam_re.shape
    gs = S5_GROUP
    assert 2 * p == LANES and g % S5_LANE_GROUPS == 0
    n_blocks = g // S5_LANE_GROUPS
    tk = chunk * LANES
    wide = S5_LANE_GROUPS * p

    def rows(v):
        v = v.reshape(nd, g * gs, p)
        return jnp.concatenate([v, v], axis=-1)

    def per_group(v):
        return rows(jnp.broadcast_to(v[:, :, None, :], (nd, g, gs, p)))

    args = (per_group(lam_re), per_group(lam_im),
            per_group(jnp.broadcast_to(log_step[:, :, None], (nd, g, p))),
            rows(jnp.swapaxes(b_re, 2, 3)), rows(jnp.swapaxes(b_im, 2, 3)), rows(c_re), rows(c_im))
    sq = pl.BlockSpec((1, LANES, LANES), lambda d, i: (d, i, 0))
    big = pl.BlockSpec((1, 1, tk, tk), lambda d, i: (d, i, 0, 0))
    assert tk == 2 * wide
    return pl.pallas_call(
        functools.partial(_s5_prep_kernel, chunk=chunk, seg_chunks=seg_chunks),
        grid=(nd, n_blocks),
        in_specs=[sq] * 7,
        out_specs=[big, big, big, pl.BlockSpec((1, 1, 8, wide), lambda d, i: (d, i, 0, 0))],
        out_shape=[jax.ShapeDtypeStruct((nd, n_blocks, tk, tk), BF16)] * 3
                  + [jax.ShapeDtypeStruct((nd, n_blocks, 8, wide), F32)],
        compiler_params=_params("parallel", "parallel"),
        name="s5_prep",
    )(*args)


def _s5_scan_kernel(u_ref, dsk_ref, tpd_ref, wst_ref, prdt_ref, avec_ref, y_ref,
                    xcat, sv, *, chunk, n_ctx):
    n = LANES
    n_c = xcat.shape[0]
    nb = sv.shape[0] // 2
    n_seg = SUBLANES
    seg = n_c // n_seg
    n_lat = n_c - n_ctx
    direction = pl.program_id(2)

    @pl.when(direction == 0)
    def _():
        for s in range(chunk):
            xcat[:, s * n:(s + 1) * n] = u_ref[0, pl.ds(s, n_c, stride=chunk), :].astype(BF16)

    def run(reverse):
        wst = wst_ref[0, 0]

        def summarise(dst, src):
            s = _dot(xcat[src, :], wst)
            for j in range(2 * nb):
                sv[j, dst, :] = s[:, j * n:(j + 1) * n]

        if reverse:
            summarise(slice(0, n_lat), slice(n_ctx, n_c))
            summarise(slice(n_lat, n_c), slice(0, n_ctx))
        else:
            summarise(slice(0, n_c), slice(0, n_c))

        def lane_blocks(r):
            return [avec_ref[0, 0, r:r + 1, j * n:(j + 1) * n] for j in range(nb)]

        ar = [jnp.broadcast_to(v, (n_seg, n)) for v in lane_blocks(0)]
        ai = [jnp.broadcast_to(v, (n_seg, n)) for v in lane_blocks(1)]

        def rows_at(i):
            return pl.ds(seg - 1 - i if reverse else i, n_seg, stride=seg)

        def advance(h, rows):
            out_r, out_i = [], []
            for j in range(nb):
                hr, hi = h[j], h[nb + j]
                out_r.append(ar[j] * hr - ai[j] * hi + sv[j, rows, :])
                out_i.append(ar[j] * hi + ai[j] * hr + sv[nb + j, rows, :])
            return tuple(out_r + out_i)

        zero = jnp.zeros((n_seg, n), F32)
        fin = lax.fori_loop(0, seg, lambda i, h: advance(h, rows_at(i)), (zero,) * (2 * nb))

        asr, asi = lane_blocks(2), lane_blocks(3)
        order = range(n_seg - 1, -1, -1) if reverse else range(n_seg)
        h0 = []
        for j in range(nb):
            rows_r, rows_i = [None] * n_seg, [None] * n_seg
            pr = pi = jnp.zeros((1, n), F32)
            prev = None
            for sgm in order:
                if prev is not None:
                    pr, pi = (asr[j] * pr - asi[j] * pi + fin[j][prev:prev + 1],
                              asr[j] * pi + asi[j] * pr + fin[nb + j][prev:prev + 1])
                rows_r[sgm], rows_i[sgm] = pr, pi
                prev = sgm
            h0.append((jnp.concatenate(rows_r, axis=0), jnp.concatenate(rows_i, axis=0)))
        h0 = tuple(v[0] for v in h0) + tuple(v[1] for v in h0)

        def step(i, h):
            rows = rows_at(i)
            nxt = advance(h, rows)
            for j in range(2 * nb):
                sv[j, rows, :] = h[j]
            return nxt

        lax.fori_loop(0, seg, step, h0)

        def entry_states(rows):
            return jnp.concatenate([sv[j, rows, :] for j in range(2 * nb)], axis=1).astype(BF16)

        if reverse:
            h_all = jnp.concatenate([entry_states(slice(n_lat, n_c)), entry_states(slice(0, n_lat))],
                                    axis=0)
        else:
            h_all = entry_states(slice(0, n_c))
        blk = MXU_DIM
        n_blk = xcat.shape[1] // blk
        intra = []
        for j in range(n_blk):
            ks = slice(j * blk, n_blk * blk) if reverse else slice(0, (j + 1) * blk)
            intra.append(_dot(xcat[:, ks], tpd_ref[0, 0, ks, j * blk:(j + 1) * blk]))
        return jnp.concatenate(intra, axis=1) + _dot_nt(h_all, prdt_ref[0, 0])

    @pl.when(direction == 0)
    def _():
        y = run(False)
        for t in range(chunk):
            rows = pl.ds(t, n_c, stride=chunk)
            y_ref[0, rows, :] = y[:, t * n:(t + 1) * n] + u_ref[0, rows, :] * dsk_ref[...]

    @pl.when(direction == 1)
    def _():
        y = run(True)
        for t in range(chunk):
            rows = pl.ds(t, n_c, stride=chunk)
            y_ref[0, rows, :] = y_ref[0, rows, :] + y[:, t * n:(t + 1) * n]


def _s5_scan(u, d_skip, mats, chunk, ctx_rows):
    b, s, d = u.shape
    tpd, wst, prdt, avec = mats
    nd, n_blocks, tk, _ = tpd.shape
    wide = avec.shape[-1]
    n_c = s // chunk

    def mat(shape):
        return pl.BlockSpec((1, 1) + shape, lambda o, i, dr: (dr, o, 0, 0))

    tok = pl.BlockSpec((1, s, LANES), lambda o, i, dr: (i, 0, o))
    return pl.pallas_call(
        functools.partial(_s5_scan_kernel, chunk=chunk, n_ctx=ctx_rows // chunk),
        grid=(n_blocks, b, nd),
        in_specs=[tok, pl.BlockSpec((1, LANES), lambda o, i, dr: (0, o)),
                  mat((tk, tk)), mat((tk, tk)), mat((tk, tk)), mat((8, wide))],
        out_specs=tok,
        out_shape=jax.ShapeDtypeStruct((b, s, d), F32),
        scratch_shapes=[pltpu.VMEM((n_c, tk), BF16),
                        pltpu.VMEM((2 * wide // LANES, n_c, LANES), F32)],
        compiler_params=_params("parallel", "parallel", "arbitrary"),
        name="s5_scan",
    )(u, d_skip, tpd, wst, prdt, avec)


def _window_start(r, rows):
    kh = min(WIN_H, rows)
    return min(max(r - kh // 2, 0), rows - kh)


def _na_tile_geometry(kind, rows):
    n_tiles = rows // NA_Q_ROWS
    tile = {0: 0, 1: 1, 2: n_tiles - 1}[kind]
    q0 = tile * NA_Q_ROWS
    k0 = min(max(q0 - (NA_K_ROWS - NA_Q_ROWS) // 2, 0), rows - NA_K_ROWS)
    return q0, k0


def _na_sub_window(kind, sub, rows):
    q0, k0 = _na_tile_geometry(kind, rows)
    kh = min(WIN_H, rows)
    starts = [_window_start(q0 + NA_SUB_Q_ROWS * sub + i, rows) - k0 for i in range(NA_SUB_Q_ROWS)]
    first = min(min(starts) // NA_SUB_Q_ROWS * NA_SUB_Q_ROWS, NA_K_ROWS - NA_SUB_K_ROWS)
    assert first >= 0 and max(starts) + kh <= first + NA_SUB_K_ROWS
    return first


def _na_bias_kernel(rpb_ref, o_ref, *, rows):
    h = pl.program_id(0)
    w = GRID_W
    kh = min(WIN_H, rows)
    ncol = 2 * WIN_W - 1
    nrow = 2 * WIN_H - 1
    cq = lax.broadcasted_iota(jnp.int32, (w, LANES), 0)
    lane = lax.broadcasted_iota(jnp.int32, (w, LANES), 1)
    ck = lane % w
    left = lane < w
    cs = jnp.clip(cq - WIN_W // 2, 0, w - WIN_W)
    col_ok = (ck >= cs) & (ck < cs + WIN_W)
    dc = jnp.clip(ck - cq + WIN_W - 1, 0, ncol - 1)
    neg = jnp.full((w, LANES), -jnp.inf, F32)

    def pair_table(d_left, d_right):
        t = jnp.zeros((w, LANES), F32)
        for j in range(ncol):
            vl = rpb_ref[(h * nrow + d_left) * ncol + j] if d_left is not None else 0.0
            vr = rpb_ref[(h * nrow + d_right) * ncol + j] if d_right is not None else 0.0
            t = jnp.where(dc == j, jnp.where(left, vl, vr), t)
        t = t * LOG2E
        ok = col_ok
        if d_left is None:
            ok = ok & jnp.logical_not(left)
        if d_right is None:
            ok = ok & left
        return jnp.where(ok, t, neg)

    cache = {}
    for kind in range(3):
        q0, k0 = _na_tile_geometry(kind, rows)
        for rq in range(NA_Q_ROWS):
            r = q0 + rq
            rs = _window_start(r, rows)
            first = k0 + _na_sub_window(kind, rq // NA_SUB_Q_ROWS, rows)
            for m in range(NA_SUB_K_ROWS // 2):
                ds = []
                for kr in (first + 2 * m, first + 2 * m + 1):
                    ds.append(kr - r + WIN_H - 1 if rs <= kr < rs + kh else None)
                key = tuple(ds)
                if key == (None, None):
                    blk = neg
                else:
                    if key not in cache:
                        cache[key] = pair_table(*key)
                    blk = cache[key]
                o_ref[kind, 0, rq * w:(rq + 1) * w, m * LANES:(m + 1) * LANES] = blk


def _na_bias(rpb, rows):
    nh = rpb.shape[0]
    nq, nk = NA_Q_ROWS * GRID_W, NA_SUB_K_ROWS * GRID_W
    return pl.pallas_call(
        functools.partial(_na_bias_kernel, rows=rows),
        grid=(nh,),
        in_specs=[pl.BlockSpec(memory_space=pltpu.SMEM)],
        out_specs=pl.BlockSpec((3, 1, nq, nk), lambda h: (0, h, 0, 0)),
        out_shape=jax.ShapeDtypeStruct((3, nh, nq, nk), F32),
        compiler_params=_params("parallel"),
        name="na_bias",
    )(rpb.reshape(-1))


def _na_attn_kernel(*refs, n_q, n_tiles):
    n_sub = n_tiles * n_q
    q_ref = refs[0]
    kv_refs = refs[1:1 + n_sub]
    kvc_ref = refs[1 + n_sub]
    bias_refs = refs[2 + n_sub:2 + n_sub + n_tiles]
    o_ref = refs[2 + n_sub + n_tiles]
    tb = kvc_ref.shape[2]
    half = LANES // 2

    def per_head(x, other):
        first_head = lax.broadcasted_iota(jnp.int32, x.shape, 1) < half
        fill = jnp.full_like(x, other)
        return [jnp.where(first_head, x, fill), jnp.where(first_head, fill, x)]

    kc = kvc_ref[0, 0, :, :LANES]
    vc_heads = per_head(kvc_ref[0, 0, :, LANES:], 1.0)
    units = [(sub, hh) for sub in range(n_sub) for hh in range(2)]
    keys, values, queries = {}, {}, {}
    for sub in range(n_sub):
        keys[sub] = kv_refs[sub][0, 0, :, :LANES]
        values[sub] = per_head(kv_refs[sub][0, 0, :, LANES:], 1.0)
        queries[sub] = per_head(q_ref[0, 0, sub * tb:(sub + 1) * tb, :], 0.0)

    def score(unit):
        sub, hh = unit
        qh = queries[sub][hh]
        in_tile = sub % n_q
        bias = bias_refs[sub // n_q][0, hh, in_tile * tb:(in_tile + 1) * tb, :]
        return _dot_nt(qh, keys[sub]) + bias, _dot_nt(qh, kc)

    ahead = 2
    pending = [score(u) for u in units[:ahead]]
    outs = {}
    for i, (sub, hh) in enumerate(units):
        s, sc = pending.pop(0)
        if i + ahead < len(units):
            pending.append(score(units[i + ahead]))
        m = jnp.maximum(jnp.max(s, axis=-1, keepdims=True), jnp.max(sc, axis=-1, keepdims=True))
        p = jnp.exp2(s - m).astype(BF16)
        pc = jnp.exp2(sc - m).astype(BF16)
        o = _dot(p, values[sub][hh]) + _dot(pc, vc_heads[hh])
        denom = o[:, half:half + 1] if hh == 0 else o[:, 0:1]
        outs[sub, hh] = o / denom
    first_head = lax.broadcasted_iota(jnp.int32, (tb, LANES), 1) < half
    for sub in range(n_sub):
        o_ref[0, 0, sub * tb:(sub + 1) * tb, :] = jnp.where(
            first_head, outs[sub, 0], outs[sub, 1]).astype(BF16)


def _na_attention(q, kv, bias, ctx_rows):
    b, hp, s, _ = q.shape
    seq = s - ctx_rows
    tb = TOKEN_TILE
    assert ctx_rows == tb and kv.shape[2] == s
    rows = seq // GRID_W
    n_tiles = rows // NA_Q_ROWS
    assert n_tiles >= 3
    assert NA_SUB_Q_ROWS * GRID_W == tb
    n_q = NA_Q_ROWS // NA_SUB_Q_ROWS
    n_k = NA_SUB_K_ROWS * GRID_W // tb
    slab = NA_K_ROWS * GRID_W // tb
    last_k0 = (rows - NA_K_ROWS) * GRID_W // tb

    per_step = NA_TILES_PER_STEP
    assert n_tiles % per_step == 0

    def pick(tile, by_kind):
        return jnp.where(tile == 0, by_kind[0], jnp.where(tile == n_tiles - 1, by_kind[2], by_kind[1]))

    def window_map(j, sub):
        offs = [_na_sub_window(kind, sub, rows) * GRID_W // tb for kind in range(3)]

        def f(p, bi, t):
            tile = per_step * t + j
            first = jnp.clip(n_q * tile - (slab - n_q) // 2, 0, last_k0)
            return (bi, p, (1 + first + pick(tile, offs)) * tb, 0)
        return f

    def bias_map(j):
        return lambda p, bi, t: (pick(per_step * t + j, (0, 1, 2)), p, 0, 0)

    nq = per_step * NA_Q_ROWS * GRID_W
    step_spec = pl.BlockSpec((1, 1, nq, LANES), lambda p, bi, t: (bi, p, t, 0))
    window_shape = tuple(pl.Element(n) for n in (1, 1, n_k * tb, 2 * LANES))
    kv_specs = [pl.BlockSpec(window_shape, window_map(j, sub))
                for j in range(per_step) for sub in range(n_q)]
    bias_specs = [pl.BlockSpec((1, 2) + bias.shape[2:], bias_map(j)) for j in range(per_step)]
    q_spec = pl.BlockSpec(tuple(pl.Element(n) for n in (1, 1, nq, LANES)),
                          lambda p, bi, t: (bi, p, (1 + t * (nq // tb)) * tb, 0))
    in_specs = ([q_spec] + kv_specs
                + [pl.BlockSpec((1, 1, tb, 2 * LANES), lambda p, bi, t: (bi, p, 0, 0))] + bias_specs)
    args = [q] + [kv] * (len(kv_specs) + 1) + [bias] * per_step
    return pl.pallas_call(
        functools.partial(_na_attn_kernel, n_q=n_q, n_tiles=per_step),
        grid=(hp, b, n_tiles // per_step),
        in_specs=in_specs,
        out_specs=step_spec,
        out_shape=jax.ShapeDtypeStruct((b, hp, seq, LANES), BF16),
        compiler_params=_params("parallel", "parallel", "parallel"),
        name="na_attention",
    )(*args)


def kernel(x, c, ctx, c_ctx, norm_g, ada_w, ada_b, ffn_w_in, ffn_w_out, ssm_w_in, ssm_lambda_re, ssm_lambda_im, ssm_log_step, ssm_b_re, ssm_b_im, ssm_c_re, ssm_c_im, ssm_d, ssm_w_glu, na_w_qkv, na_q_norm, na_k_norm, na_rpb, na_w_o):
    b, seq, d = x.shape
    ctx_rows = ctx.shape[1]
    depth = norm_g.shape[0]
    assert b + 1 <= 8 and ctx_rows == TOKEN_TILE and depth == 2

    cvec = jnp.zeros((8, d), F32).at[:b].set(c).at[b].set(c_ctx)
    m = _ada_modulation(cvec, b + 1, ada_w, ada_b).reshape(depth, 8, N_SUB, 3, d)

    def mod_rows(layer, sub):
        lat = m[layer, :b, sub]
        cx = jnp.broadcast_to(m[layer, b, sub], (b, 3, d))
        return jnp.concatenate([cx, lat], axis=1)

    def gain(layer, sub):
        return norm_g[layer, sub].reshape(1, d)

    def ffn(layer, which, w_in, w_out):
        sub = 2 * which
        return mod_rows(layer, sub), gain(layer, sub), w_in, w_out, ()

    def ffn_f32(layer, which):
        return [(ffn_w_in, (layer, which)), (ffn_w_out, (layer, which))]

    xs, u, w_in, w_out, w_glu = _stage(
        x, ffn(0, 0, ffn_w_in[0, 0].astype(BF16), ffn_w_out[0, 0].astype(BF16)),
        pre=("join", ctx), post=("s5", mod_rows(0, 1), gain(0, 1), ssm_w_in[0].astype(BF16)),
        convert=ffn_f32(0, 1) + [(ssm_w_glu, (0,))], ctx_rows=ctx_rows,
        tile=S5_STAGE_TILE, name="ffn_s5in")
    n_chunks = (ctx_rows + seq) // S5_CHUNK
    assert n_chunks % SUBLANES == 0 and ctx_rows % S5_CHUNK == 0
    mats = _s5_prep(ssm_lambda_re[0], ssm_lambda_im[0], ssm_log_step[0], ssm_b_re[0], ssm_b_im[0],
                    ssm_c_re[0], ssm_c_im[0], S5_CHUNK, n_chunks // SUBLANES)
    y = _s5_scan(u, ssm_d[0].reshape(1, d), mats, S5_CHUNK, ctx_rows)
    xs, w_in, w_out, w_qkv = _stage(
        xs, ffn(0, 1, w_in, w_out), pre=("s5", y, mod_rows(0, 1), w_glu),
        convert=ffn_f32(1, 0) + [(na_w_qkv, (0,))], ctx_rows=ctx_rows,
        tile=S5_STAGE_TILE, name="s5out_ffn")

    xs, q, kv, w_in, w_out, w_o = _stage(
        xs, ffn(1, 0, w_in, w_out),
        post=("qkv", mod_rows(1, 1), gain(1, 1), w_qkv, na_q_norm[0], na_k_norm[0]),
        convert=ffn_f32(1, 1) + [(na_w_o, (0,))], ctx_rows=ctx_rows,
        tile=TOKEN_TILE, name="ffn_qkv")
    bias = _na_bias(na_rpb[0], seq // GRID_W)
    attn = _na_attention(q, kv, bias, ctx_rows)
    out, = _stage(xs, ffn(1, 1, w_in, w_out), pre=("na", attn, mod_rows(1, 1), w_o),
                  ctx_rows=ctx_rows, tile=LATENT_TOKEN_TILE, name="naout_ffn")
    return out
```

```python
import functools
import math

import jax
import jax.numpy as jnp
from jax import lax
from jax.experimental import pallas as pl
from jax.experimental.pallas import tpu as pltpu

F32 = jnp.float32
BF16 = jnp.bfloat16

GRID_W = 64
N_SUB = 3
MACARON_WEIGHT = 0.5
RMS_EPS = 1e-6
S5_GROUP = 16
S5_STATE = 64
S5_MIN_NEG_RE = -1e-4
NA_HEADS = 16
WIN_H = 8
WIN_W = 16

LANES = 128
SUBLANES = 8
MXU_DIM = 256
VMEM_LIMIT = 56 * 1024 * 1024

TOKEN_TILE = 256
S5_STAGE_TILE = 384
LATENT_TOKEN_TILE = 512
WEIGHT_CAST_CHUNKS = 16
BF16_SUBLANES = 16
S5_CHUNK = 8
S5_LANE_GROUPS = LANES // S5_GROUP
NA_Q_ROWS = 8
NA_K_ROWS = 16
NA_TILES_PER_STEP = 4
NA_SUB_Q_ROWS = 4
NA_SUB_K_ROWS = 12
LOG2E = math.log2(math.e)


def _dot(a, b):
    return jnp.dot(a, b, preferred_element_type=F32)


def _dot_nt(a, b):
    return lax.dot_general(a, b, (((1,), (1,)), ((), ())), preferred_element_type=F32)


def _dot_nt_f32(a, b):
    return lax.dot_general(a, b, (((1,), (1,)), ((), ())), preferred_element_type=F32,
                           precision=lax.Precision.HIGHEST)


def _split_bf16(x):
    hi = x.astype(BF16)
    return hi, (x - hi.astype(F32)).astype(BF16)


def _params(*sem):
    return pltpu.CompilerParams(dimension_semantics=sem, vmem_limit_bytes=VMEM_LIMIT)


def _resident(shape, index_map):
    return pl.BlockSpec(shape, index_map, pipeline_mode=pl.Buffered(1))


def _ada_kernel(ct_ref, w_ref, b_ref, o_ref, *, n_rows):
    ct = ct_ref[...]
    s = ct * jax.nn.sigmoid(ct)
    w = w_ref[0]
    rows = [jnp.sum(w * s[:, r:r + 1], axis=0, keepdims=True) + b_ref[0] for r in range(n_rows)]
    rows.append(jnp.zeros((SUBLANES - n_rows, w.shape[1]), F32))
    o_ref[0] = jnp.concatenate(rows, axis=0)


def _ada_modulation(cvec, n_rows, ada_w, ada_b):
    depth, d, n = ada_w.shape
    tn = d
    return pl.pallas_call(
        functools.partial(_ada_kernel, n_rows=n_rows),
        grid=(depth, n // tn),
        in_specs=[pl.BlockSpec((d, SUBLANES), lambda l, j: (0, 0)),
                  pl.BlockSpec((1, d, tn), lambda l, j: (l, 0, j)),
                  pl.BlockSpec((1, 1, tn), lambda l, j: (l, 0, j))],
        out_specs=pl.BlockSpec((1, SUBLANES, tn), lambda l, j: (l, 0, j)),
        out_shape=jax.ShapeDtypeStruct((depth, SUBLANES, n), F32),
        compiler_params=_params("parallel", "parallel"),
        name="ada_modulation",
    )(cvec.T, ada_w, ada_b.reshape(depth, 1, n))


def _row_is_context(tm, first_row, ctx_rows):
    return (first_row + lax.broadcasted_iota(jnp.int32, (tm, 1), 0)) < ctx_rows


def _modulated_norm(x, g, mod, first_row, ctx_rows):
    tm = x.shape[0]
    y = x * lax.rsqrt(jnp.mean(x * x, axis=-1, keepdims=True) + RMS_EPS) * g
    if ctx_rows == 0:
        shift, scale, gate = mod[3:4], mod[4:5], mod[5:6]
    else:
        is_ctx = _row_is_context(tm, first_row, ctx_rows)
        shift = jnp.where(is_ctx, mod[0:1], mod[3:4])
        scale = jnp.where(is_ctx, mod[1:2], mod[4:5])
        gate = jnp.where(is_ctx, mod[2:3], mod[5:6])
    return y * (1.0 + scale) + shift, gate


def _gelu_tanh(y):
    return 0.5 * y * (1.0 + jnp.tanh(math.sqrt(2.0 / math.pi) * (y + 0.044715 * (y * y * y))))


def _stage_kernel(*refs, pre, post, n_convert, ctx_rows, hidden, head_dim):
    it = iter(refs)
    x_ref = next(it)
    tm, d = x_ref.shape[1], x_ref.shape[2]
    first_row = pl.program_id(1) * tm
    x = x_ref[0]
    if pre == "join":
        head = next(it)[0]
        if ctx_rows < tm:
            head = jnp.concatenate([head, x[:tm - ctx_rows]], axis=0)
        x = jnp.where(pl.program_id(1) == 0, head, x)

    def gate_of(mod):
        if ctx_rows == 0:
            return mod[5:6]
        return jnp.where(_row_is_context(tm, first_row, ctx_rows), mod[2:3], mod[5:6])

    if pre == "s5":
        y_ref, pmod_ref, wglu_ref = next(it), next(it), next(it)
        z = _dot(_gelu_tanh(y_ref[0]).astype(BF16), wglu_ref[...])
        x = x + gate_of(pmod_ref[0]) * (z[:, :d] * jax.nn.sigmoid(z[:, d:]))
    elif pre == "na":
        a_ref, pmod_ref, wo_ref = next(it), next(it), next(it)
        a = jnp.concatenate([a_ref[0, p] for p in range(a_ref.shape[1])], axis=-1)
        x = x + gate_of(pmod_ref[0]) * _dot(a, wo_ref[...])

    mod_ref, g_ref, win_ref, wout_ref = next(it), next(it), next(it), next(it)
    h, gate = _modulated_norm(x, g_ref[...], mod_ref[0], first_row, ctx_rows)
    hb = h.astype(BF16)
    gt = _dot(hb, win_ref[:, :hidden])
    up = _dot(hb, win_ref[:, hidden:])
    act = (gt * jax.nn.sigmoid(gt) * up).astype(BF16)
    x = x + (MACARON_WEIGHT * gate) * _dot(act, wout_ref[...])

    if post == "s5":
        qmod_ref, qg_ref, w_ref = next(it), next(it), next(it)
    elif post == "qkv":
        (qmod_ref, qg_ref, w_ref, reduce_ref, expand_ref,
         qgain_ref, kgain_ref) = (next(it) for _ in range(7))

    cast_in = [next(it) for _ in range(n_convert)]
    cast_out = refs[len(refs) - n_convert:]

    @pl.when(pl.program_id(0) * pl.num_programs(1) + pl.program_id(1) < WEIGHT_CAST_CHUNKS)
    def _():
        for src, dst in zip(cast_in, cast_out):
            dst[...] = src[...].astype(BF16)

    o_ref = next(it)
    o_ref[0] = x
    if post is None:
        return
    h2, _ = _modulated_norm(x, qg_ref[...], qmod_ref[0], first_row, ctx_rows)
    proj = _dot(h2.astype(BF16), w_ref[...])
    if post == "s5":
        next(it)[0] = proj
        return

    def head_norm(z, gain):
        ms = _dot((z * z).astype(BF16), reduce_ref[...]) * (1.0 / head_dim)
        hi, lo = _split_bf16(lax.rsqrt(ms + RMS_EPS))
        return z * _dot(jnp.concatenate([hi, lo], axis=1), expand_ref[...]) * gain

    q = head_norm(proj[:, :d], qgain_ref[...]) * (head_dim ** -0.5 * LOG2E)
    k = head_norm(proj[:, d:2 * d], kgain_ref[...])
    v = proj[:, 2 * d:]
    q_ref, kv_ref = next(it), next(it)
    for hp in range(d // LANES):
        sl = slice(hp * LANES, (hp + 1) * LANES)
        q_ref[0, hp] = q[:, sl].astype(BF16)
        kv_ref[0, hp, :, :LANES] = k[:, sl].astype(BF16)
        kv_ref[0, hp, :, LANES:] = v[:, sl].astype(BF16)


def _stage(xs, ffn, *, pre=None, post=None, convert=(), ctx_rows, tile, name):
    b, s, d = xs.shape
    tm = tile
    mod, gain, w_in, w_out, which = ffn
    hidden = w_out.shape[-2]
    head_dim = d // NA_HEADS
    tok = pl.BlockSpec((1, tm, d), lambda i, t: (i, t, 0))
    modspec = pl.BlockSpec((1, 6, d), lambda i, t: (i, 0, 0))
    vec = pl.BlockSpec((1, d), lambda i, t: (0, 0))

    def weight(w, lead=()):
        return _resident((None,) * len(lead) + w.shape[len(lead):], lambda i, t: lead + (0, 0))

    def rows_from(offset):
        step = math.gcd(tm, ctx_rows)
        n_tile, n_ctx = tm // step, ctx_rows // step
        return pl.BlockSpec(tuple(pl.Element(n) for n in (1, tm, d)),
                            lambda i, t: (i, offset(t, n_tile, n_ctx) * step, 0))

    args = [xs]
    if pre is not None and pre[0] == "join":
        assert tm >= ctx_rows
        s += ctx_rows
        args.append(pre[1])
        in_specs = [rows_from(lambda t, n, c: jnp.maximum(t * n - c, 0)),
                    pl.BlockSpec((1, ctx_rows, d), lambda i, t: (i, 0, 0))]
    elif pre is not None and pre[0] == "na":
        in_specs = [rows_from(lambda t, n, c: t * n + c)]
        s, ctx_rows = s - ctx_rows, 0
    else:
        in_specs = [tok]
    if pre is not None and pre[0] == "join":
        pass
    elif pre is not None and pre[0] == "s5":
        _, y, pmod, w_glu = pre
        args += [y, pmod, w_glu]
        in_specs += [tok, modspec, weight(w_glu)]
    elif pre is not None:
        _, attn, pmod, w_o = pre
        args += [attn, pmod, w_o]
        in_specs += [pl.BlockSpec((1, attn.shape[1], tm, LANES), lambda i, t: (i, 0, t, 0)),
                     modspec, weight(w_o)]
    args += [mod, gain, w_in, w_out]
    in_specs += [modspec, vec, weight(w_in, which), weight(w_out, which)]
    out_shape = [jax.ShapeDtypeStruct((b, s, d), F32)]
    out_specs = [tok]
    if post is not None and post[0] == "s5":
        _, qmod, qgain, w = post
        args += [qmod, qgain, w]
        in_specs += [modspec, vec, weight(w)]
        out_shape.append(jax.ShapeDtypeStruct((b, s, d), F32))
        out_specs.append(tok)
    elif post is not None:
        _, qmod, qgain, w, q_gain, k_gain = post
        hid = jnp.arange(d) // head_dim
        to_head = (hid[:, None] == jnp.arange(LANES)[None, :]).astype(BF16)
        from_head = jnp.concatenate([to_head.T, to_head.T], axis=0)
        args += [qmod, qgain, w, to_head, from_head,
                 jnp.tile(q_gain, NA_HEADS).reshape(1, d), jnp.tile(k_gain, NA_HEADS).reshape(1, d)]
        in_specs += [modspec, vec, weight(w), weight(to_head), weight(from_head), vec, vec]
        hp = d // LANES
        out_shape += [jax.ShapeDtypeStruct((b, hp, s, LANES), BF16),
                      jax.ShapeDtypeStruct((b, hp, s, 2 * LANES), BF16)]
        out_specs += [pl.BlockSpec((1, hp, tm, LANES), lambda i, t: (i, 0, t, 0)),
                      pl.BlockSpec((1, hp, tm, 2 * LANES), lambda i, t: (i, 0, t, 0))]
    n_t = s // tm
    for w, lead in convert:
        rows, cols = w.shape[-2:]
        chunk = rows // WEIGHT_CAST_CHUNKS
        assert chunk * WEIGHT_CAST_CHUNKS == rows and chunk % BF16_SUBLANES == 0
        assert b * n_t >= WEIGHT_CAST_CHUNKS

        def chunk_of(i, t):
            return jnp.minimum(i * n_t + t, WEIGHT_CAST_CHUNKS - 1)

        args.append(w)
        in_specs.append(pl.BlockSpec((None,) * len(lead) + (chunk, cols),
                                     lambda i, t, lead=lead: lead + (chunk_of(i, t), 0)))
        out_shape.append(jax.ShapeDtypeStruct((rows, cols), BF16))
        out_specs.append(pl.BlockSpec((chunk, cols), lambda i, t: (chunk_of(i, t), 0)))
    return pl.pallas_call(
        functools.partial(_stage_kernel, pre=None if pre is None else pre[0],
                          post=None if post is None else post[0], n_convert=len(convert),
                          ctx_rows=ctx_rows, hidden=hidden, head_dim=head_dim),
        grid=(b, n_t),
        in_specs=in_specs,
        out_specs=out_specs,
        out_shape=out_shape,
        compiler_params=_params("arbitrary", "arbitrary"),
        name=name,
    )(*args)


def _s5_prep_kernel(lr_ref, li_ref, ls_ref, btr_ref, bti_ref, cr_ref, ci_ref,
                    tpd_ref, wst_ref, prdt_ref, avec_ref, *, chunk, seg_chunks):
    n = LANES
    ng = S5_LANE_GROUPS
    half = n // 2
    wide = ng * half
    lr = jnp.minimum(lr_ref[0], S5_MIN_NEG_RE)
    li = li_ref[0]
    dt = jnp.exp(ls_ref[0])
    mag = jnp.exp(lr * dt)
    ar = mag * jnp.cos(li * dt)
    ai = mag * jnp.sin(li * dt)
    den = lr * lr + li * li
    zr = ((ar - 1.0) * lr + ai * li) / den
    zi = (ai * lr - (ar - 1.0) * li) / den
    btr, bti = btr_ref[0], bti_ref[0]
    bbr = zr * btr - zi * bti
    bbi = zr * bti + zi * btr
    cr, ci = cr_ref[0], ci_ref[0]

    lane = lax.broadcasted_iota(jnp.int32, (n, n), 1)
    row = lax.broadcasted_iota(jnp.int32, (n, n), 0)
    first_half = lane < half
    same_group = (row // S5_GROUP) == (lane // S5_GROUP)
    row_w = lax.broadcasted_iota(jnp.int32, (n, wide), 0)
    lane_w = lax.broadcasted_iota(jnp.int32, (n, wide), 1)
    own_states = (row_w // S5_GROUP) == (lane_w // half)

    def spread(v):
        return jnp.where(own_states, jnp.concatenate([v] * (wide // n), axis=1), 0.0).astype(BF16)

    powers = [(jnp.ones((n, n), F32), jnp.zeros((n, n), F32))]
    for _ in range(chunk):
        er, ei = powers[-1]
        powers.append((er * ar - ei * ai, er * ai + ei * ar))

    def input_map(k):
        er, ei = powers[k]
        return bbr * er - bbi * ei, bbr * ei + bbi * er

    def output_map(k):
        er, ei = powers[k]
        return cr * er - ci * ei, cr * ei + ci * er

    c2 = jnp.where(first_half, cr, -ci)
    lag_blocks = []
    for k in range(chunk):
        wr, wi = input_map(k)
        kd = _dot_nt_f32(jnp.where(first_half, wr, wi), c2)
        lag_blocks.append(jnp.where(same_group, kd, 0.0).astype(BF16))
    zero_block = jnp.zeros((n, n), BF16)

    def emit(reverse):
        for s in range(chunk):
            wr, wi = input_map(s if reverse else chunk - 1 - s)
            wst_ref[0, 0, s * n:(s + 1) * n, 0:wide] = spread(wr)
            wst_ref[0, 0, s * n:(s + 1) * n, wide:2 * wide] = spread(wi)
        for t in range(chunk):
            pr, pi = output_map(chunk - t if reverse else t + 1)
            prdt_ref[0, 0, t * n:(t + 1) * n, 0:wide] = spread(pr)
            prdt_ref[0, 0, t * n:(t + 1) * n, wide:2 * wide] = spread(-pi)
        for s in range(chunk):
            for t in range(chunk):
                lag = s - t if reverse else t - s
                tpd_ref[0, 0, s * n:(s + 1) * n, t * n:(t + 1) * n] = (
                    lag_blocks[lag] if lag >= 0 else zero_block)

    @pl.when(pl.program_id(0) == 0)
    def _():
        emit(False)

    @pl.when(pl.program_id(0) == 1)
    def _():
        emit(True)

    er, ei = powers[chunk]
    sr, si = jnp.ones((n, n), F32), jnp.zeros((n, n), F32)
    pr_, pi_ = er, ei
    e = seg_chunks
    while e:
        if e & 1:
            sr, si = sr * pr_ - si * pi_, sr * pi_ + si * pr_
        pr_, pi_ = pr_ * pr_ - pi_ * pi_, 2.0 * pr_ * pi_
        e >>= 1
    for r, v in enumerate((er, ei, sr, si)):
        for j in range(ng // 2):
            g0 = 2 * j * S5_GROUP
            g1 = g0 + S5_GROUP
            avec_ref[0, 0, r:r + 1, j * n:(j + 1) * n] = jnp.where(
                first_half[0:1], v[g0:g0 + 1], v[g1:g1 + 1])
    avec_ref[0, 0, 4:8, :] = jnp.zeros((4, wide), F32)


def _s5_prep(lam_re, lam_im, log_step, b_re, b_im, c_re, c_im, chunk, seg_chunks):
    nd, g, p = lam_re.shape
    gs = S5_GROUP
    assert 2 * p == LANES and g % S5_LANE_GROUPS == 0
    n_blocks = g // S5_LANE_GROUPS
    tk = chunk * LANES
    wide = S5_LANE_GROUPS * p

    def rows(v):
        v = v.reshape(nd, g * gs, p)
        return jnp.concatenate([v, v], axis=-1)

    def per_group(v):
        return rows(jnp.broadcast_to(v[:, :, None, :], (nd, g, gs, p)))

    args = (per_group(lam_re), per_group(lam_im),
            per_group(jnp.broadcast_to(log_step[:, :, None], (nd, g, p))),
            rows(jnp.swapaxes(b_re, 2, 3)), rows(jnp.swapaxes(b_im, 2, 3)), rows(c_re), rows(c_im))
    sq = pl.BlockSpec((1, LANES, LANES), lambda d, i: (d, i, 0))
    big = pl.BlockSpec((1, 1, tk, tk), lambda d, i: (d, i, 0, 0))
    assert tk == 2 * wide
    return pl.pallas_call(
        functools.partial(_s5_prep_kernel, chunk=chunk, seg_chunks=seg_chunks),
        grid=(nd, n_blocks),
        in_specs=[sq] * 7,
        out_specs=[big, big, big, pl.BlockSpec((1, 1, 8, wide), lambda d, i: (d, i, 0, 0))],
        out_shape=[jax.ShapeDtypeStruct((nd, n_blocks, tk, tk), BF16)] * 3
                  + [jax.ShapeDtypeStruct((nd, n_blocks, 8, wide), F32)],
        compiler_params=_params("parallel", "parallel"),
        name="s5_prep",
    )(*args)


def _s5_scan_kernel(u_ref, dsk_ref, tpd_ref, wst_ref, prdt_ref, avec_ref, y_ref,
                    xcat, sv, *, chunk, n_ctx):
    n = LANES
    n_c = xcat.shape[0]
    nb = sv.shape[0] // 2
    n_seg = SUBLANES
    seg = n_c // n_seg
    n_lat = n_c - n_ctx
    direction = pl.program_id(2)

    @pl.when(direction == 0)
    def _():
        for s in range(chunk):
            xcat[:, s * n:(s + 1) * n] = u_ref[0, pl.ds(s, n_c, stride=chunk), :].astype(BF16)

    def run(reverse):
        wst = wst_ref[0, 0]

        def summarise(dst, src):
            s = _dot(xcat[src, :], wst)
            for j in range(2 * nb):
                sv[j, dst, :] = s[:, j * n:(j + 1) * n]

        if reverse:
            summarise(slice(0, n_lat), slice(n_ctx, n_c))
            summarise(slice(n_lat, n_c), slice(0, n_ctx))
        else:
            summarise(slice(0, n_c), slice(0, n_c))

        blk = MXU_DIM
        n_blk = xcat.shape[1] // blk
        intra = []
        for j in range(n_blk):
            ks = slice(j * blk, n_blk * blk) if reverse else slice(0, (j + 1) * blk)
            intra.append(_dot(xcat[:, ks], tpd_ref[0, 0, ks, j * blk:(j + 1) * blk]))
        intra = jnp.concatenate(intra, axis=1)

        def lane_blocks(r):
            return [avec_ref[0, 0, r:r + 1, j * n:(j + 1) * n] for j in range(nb)]

        ar = [jnp.broadcast_to(v, (n_seg, n)) for v in lane_blocks(0)]
        ai = [jnp.broadcast_to(v, (n_seg, n)) for v in lane_blocks(1)]

        def rows_at(i):
            return pl.ds(seg - 1 - i if reverse else i, n_seg, stride=seg)

        def advance(h, rows):
            out_r, out_i = [], []
            for j in range(nb):
                hr, hi = h[j], h[nb + j]
                out_r.append(ar[j] * hr - ai[j] * hi + sv[j, rows, :])
                out_i.append(ar[j] * hi + ai[j] * hr + sv[nb + j, rows, :])
            return tuple(out_r + out_i)

        zero = jnp.zeros((n_seg, n), F32)
        fin = lax.fori_loop(0, seg, lambda i, h: advance(h, rows_at(i)), (zero,) * (2 * nb),
                            unroll=True)

        asr, asi = lane_blocks(2), lane_blocks(3)
        order = range(n_seg - 1, -1, -1) if reverse else range(n_seg)
        h0 = []
        for j in range(nb):
            rows_r, rows_i = [None] * n_seg, [None] * n_seg
            pr = pi = jnp.zeros((1, n), F32)
            prev = None
            for sgm in order:
                if prev is not None:
                    pr, pi = (asr[j] * pr - asi[j] * pi + fin[j][prev:prev + 1],
                              asr[j] * pi + asi[j] * pr + fin[nb + j][prev:prev + 1])
                rows_r[sgm], rows_i[sgm] = pr, pi
                prev = sgm
            h0.append((jnp.concatenate(rows_r, axis=0), jnp.concatenate(rows_i, axis=0)))
        h0 = tuple(v[0] for v in h0) + tuple(v[1] for v in h0)

        def step(i, h):
            rows = rows_at(i)
            nxt = advance(h, rows)
            for j in range(2 * nb):
                sv[j, rows, :] = h[j]
            return nxt

        lax.fori_loop(0, seg, step, h0, unroll=True)

        def entry_states(rows):
            return jnp.concatenate([sv[j, rows, :] for j in range(2 * nb)], axis=1).astype(BF16)

        if reverse:
            h_all = jnp.concatenate([entry_states(slice(n_lat, n_c)), entry_states(slice(0, n_lat))],
                                    axis=0)
        else:
            h_all = entry_states(slice(0, n_c))
        return intra + _dot_nt(h_all, prdt_ref[0, 0])

    @pl.when(direction == 0)
    def _():
        y = run(False)
        for t in range(chunk):
            rows = pl.ds(t, n_c, stride=chunk)
            y_ref[0, rows, :] = y[:, t * n:(t + 1) * n] + u_ref[0, rows, :] * dsk_ref[...]

    @pl.when(direction == 1)
    def _():
        y = run(True)
        for t in range(chunk):
            rows = pl.ds(t, n_c, stride=chunk)
            y_ref[0, rows, :] = y_ref[0, rows, :] + y[:, t * n:(t + 1) * n]


def _s5_scan(u, d_skip, mats, chunk, ctx_rows):
    b, s, d = u.shape
    tpd, wst, prdt, avec = mats
    nd, n_blocks, tk, _ = tpd.shape
    wide = avec.shape[-1]
    n_c = s // chunk

    def mat(shape):
        return pl.BlockSpec((1, 1) + shape, lambda o, i, dr: (dr, o, 0, 0))

    tok = pl.BlockSpec((1, s, LANES), lambda o, i, dr: (i, 0, o))
    return pl.pallas_call(
        functools.partial(_s5_scan_kernel, chunk=chunk, n_ctx=ctx_rows // chunk),
        grid=(n_blocks, b, nd),
        in_specs=[tok, pl.BlockSpec((1, LANES), lambda o, i, dr: (0, o)),
                  mat((tk, tk)), mat((tk, tk)), mat((tk, tk)), mat((8, wide))],
        out_specs=tok,
        out_shape=jax.ShapeDtypeStruct((b, s, d), F32),
        scratch_shapes=[pltpu.VMEM((n_c, tk), BF16),
                        pltpu.VMEM((2 * wide // LANES, n_c, LANES), F32)],
        compiler_params=_params("parallel", "parallel", "arbitrary"),
        name="s5_scan",
    )(u, d_skip, tpd, wst, prdt, avec)


def _window_start(r, rows):
    kh = min(WIN_H, rows)
    return min(max(r - kh // 2, 0), rows - kh)


def _na_tile_geometry(kind, rows):
    n_tiles = rows // NA_Q_ROWS
    tile = {0: 0, 1: 1, 2: n_tiles - 1}[kind]
    q0 = tile * NA_Q_ROWS
    k0 = min(max(q0 - (NA_K_ROWS - NA_Q_ROWS) // 2, 0), rows - NA_K_ROWS)
    return q0, k0


def _na_sub_window(kind, sub, rows):
    q0, k0 = _na_tile_geometry(kind, rows)
    kh = min(WIN_H, rows)
    starts = [_window_start(q0 + NA_SUB_Q_ROWS * sub + i, rows) - k0 for i in range(NA_SUB_Q_ROWS)]
    first = min(min(starts) // NA_SUB_Q_ROWS * NA_SUB_Q_ROWS, NA_K_ROWS - NA_SUB_K_ROWS)
    assert first >= 0 and max(starts) + kh <= first + NA_SUB_K_ROWS
    return first


def _na_bias_kernel(rpb_ref, o_ref, *, rows):
    h = pl.program_id(0)
    w = GRID_W
    kh = min(WIN_H, rows)
    ncol = 2 * WIN_W - 1
    nrow = 2 * WIN_H - 1
    cq = lax.broadcasted_iota(jnp.int32, (w, LANES), 0)
    lane = lax.broadcasted_iota(jnp.int32, (w, LANES), 1)
    ck = lane % w
    left = lane < w
    cs = jnp.clip(cq - WIN_W // 2, 0, w - WIN_W)
    col_ok = (ck >= cs) & (ck < cs + WIN_W)
    dc = jnp.clip(ck - cq + WIN_W - 1, 0, ncol - 1)
    neg = jnp.full((w, LANES), -jnp.inf, F32)

    def pair_table(d_left, d_right):
        t = jnp.zeros((w, LANES), F32)
        for j in range(ncol):
            vl = rpb_ref[(h * nrow + d_left) * ncol + j] if d_left is not None else 0.0
            vr = rpb_ref[(h * nrow + d_right) * ncol + j] if d_right is not None else 0.0
            t = jnp.where(dc == j, jnp.where(left, vl, vr), t)
        t = t * LOG2E
        ok = col_ok
        if d_left is None:
            ok = ok & jnp.logical_not(left)
        if d_right is None:
            ok = ok & left
        return jnp.where(ok, t, neg)

    cache = {}
    for kind in range(3):
        q0, k0 = _na_tile_geometry(kind, rows)
        for rq in range(NA_Q_ROWS):
            r = q0 + rq
            rs = _window_start(r, rows)
            first = k0 + _na_sub_window(kind, rq // NA_SUB_Q_ROWS, rows)
            for m in range(NA_SUB_K_ROWS // 2):
                ds = []
                for kr in (first + 2 * m, first + 2 * m + 1):
                    ds.append(kr - r + WIN_H - 1 if rs <= kr < rs + kh else None)
                key = tuple(ds)
                if key == (None, None):
                    blk = neg
                else:
                    if key not in cache:
                        cache[key] = pair_table(*key)
                    blk = cache[key]
                o_ref[kind, 0, rq * w:(rq + 1) * w, m * LANES:(m + 1) * LANES] = blk


def _na_bias(rpb, rows):
    nh = rpb.shape[0]
    nq, nk = NA_Q_ROWS * GRID_W, NA_SUB_K_ROWS * GRID_W
    return pl.pallas_call(
        functools.partial(_na_bias_kernel, rows=rows),
        grid=(nh,),
        in_specs=[pl.BlockSpec(memory_space=pltpu.SMEM)],
        out_specs=pl.BlockSpec((3, 1, nq, nk), lambda h: (0, h, 0, 0)),
        out_shape=jax.ShapeDtypeStruct((3, nh, nq, nk), F32),
        compiler_params=_params("parallel"),
        name="na_bias",
    )(rpb.reshape(-1))


def _na_attn_kernel(*refs, n_q, n_tiles):
    n_sub = n_tiles * n_q
    q_ref = refs[0]
    kv_refs = refs[1:1 + n_sub]
    kvc_ref = refs[1 + n_sub]
    bias_refs = refs[2 + n_sub:2 + n_sub + n_tiles]
    o_ref = refs[2 + n_sub + n_tiles]
    tb = kvc_ref.shape[2]
    half = LANES // 2

    def per_head(x, other):
        first_head = lax.broadcasted_iota(jnp.int32, x.shape, 1) < half
        fill = jnp.full_like(x, other)
        return [jnp.where(first_head, x, fill), jnp.where(first_head, fill, x)]

    kc = kvc_ref[0, 0, :, :LANES]
    vc_heads = per_head(kvc_ref[0, 0, :, LANES:], 1.0)
    units = [(sub, hh) for sub in range(n_sub) for hh in range(2)]
    keys, values, queries = {}, {}, {}
    for sub in range(n_sub):
        keys[sub] = kv_refs[sub][0, 0, :, :LANES]
        values[sub] = per_head(kv_refs[sub][0, 0, :, LANES:], 1.0)
        queries[sub] = per_head(q_ref[0, 0, sub * tb:(sub + 1) * tb, :], 0.0)

    def score(unit):
        sub, hh = unit
        qh = queries[sub][hh]
        in_tile = sub % n_q
        bias = bias_refs[sub // n_q][0, hh, in_tile * tb:(in_tile + 1) * tb, :]
        return _dot_nt(qh, keys[sub]) + bias, _dot_nt(qh, kc)

    ahead = 2
    pending = [score(u) for u in units[:ahead]]
    outs = {}
    for i, (sub, hh) in enumerate(units):
        s, sc = pending.pop(0)
        if i + ahead < len(units):
            pending.append(score(units[i + ahead]))
        m = jnp.maximum(jnp.max(s, axis=-1, keepdims=True), jnp.max(sc, axis=-1, keepdims=True))
        p = jnp.exp2(s - m).astype(BF16)
        pc = jnp.exp2(sc - m).astype(BF16)
        o = _dot(p, values[sub][hh]) + _dot(pc, vc_heads[hh])
        denom = o[:, half:half + 1] if hh == 0 else o[:, 0:1]
        outs[sub, hh] = o / denom
    first_head = lax.broadcasted_iota(jnp.int32, (tb, LANES), 1) < half
    for sub in range(n_sub):
        o_ref[0, 0, sub * tb:(sub + 1) * tb, :] = jnp.where(
            first_head, outs[sub, 0], outs[sub, 1]).astype(BF16)


def _na_attention(q, kv, bias, ctx_rows):
    b, hp, s, _ = q.shape
    seq = s - ctx_rows
    tb = TOKEN_TILE
    assert ctx_rows == tb and kv.shape[2] == s
    rows = seq // GRID_W
    n_tiles = rows // NA_Q_ROWS
    assert n_tiles >= 3
    assert NA_SUB_Q_ROWS * GRID_W == tb
    n_q = NA_Q_ROWS // NA_SUB_Q_ROWS
    n_k = NA_SUB_K_ROWS * GRID_W // tb
    slab = NA_K_ROWS * GRID_W // tb
    last_k0 = (rows - NA_K_ROWS) * GRID_W // tb

    per_step = NA_TILES_PER_STEP
    assert n_tiles % per_step == 0

    def pick(tile, by_kind):
        return jnp.where(tile == 0, by_kind[0], jnp.where(tile == n_tiles - 1, by_kind[2], by_kind[1]))

    def window_map(j, sub):
        offs = [_na_sub_window(kind, sub, rows) * GRID_W // tb for kind in range(3)]

        def f(p, bi, t):
            tile = per_step * t + j
            first = jnp.clip(n_q * tile - (slab - n_q) // 2, 0, last_k0)
            return (bi, p, (1 + first + pick(tile, offs)) * tb, 0)
        return f

    def bias_map(j):
        return lambda p, bi, t: (pick(per_step * t + j, (0, 1, 2)), p, 0, 0)

    nq = per_step * NA_Q_ROWS * GRID_W
    step_spec = pl.BlockSpec((1, 1, nq, LANES), lambda p, bi, t: (bi, p, t, 0))
    window_shape = tuple(pl.Element(n) for n in (1, 1, n_k * tb, 2 * LANES))
    kv_specs = [pl.BlockSpec(window_shape, window_map(j, sub))
                for j in range(per_step) for sub in range(n_q)]
    bias_specs = [pl.BlockSpec((1, 2) + bias.shape[2:], bias_map(j)) for j in range(per_step)]
    q_spec = pl.BlockSpec(tuple(pl.Element(n) for n in (1, 1, nq, LANES)),
                          lambda p, bi, t: (bi, p, (1 + t * (nq // tb)) * tb, 0))
    in_specs = ([q_spec] + kv_specs
                + [pl.BlockSpec((1, 1, tb, 2 * LANES), lambda p, bi, t: (bi, p, 0, 0))] + bias_specs)
    args = [q] + [kv] * (len(kv_specs) + 1) + [bias] * per_step
    return pl.pallas_call(
        functools.partial(_na_attn_kernel, n_q=n_q, n_tiles=per_step),
        grid=(hp, b, n_tiles // per_step),
        in_specs=in_specs,
        out_specs=step_spec,
        out_shape=jax.ShapeDtypeStruct((b, hp, seq, LANES), BF16),
        compiler_params=_params("parallel", "parallel", "parallel"),
        name="na_attention",
    )(*args)


def kernel(x, c, ctx, c_ctx, norm_g, ada_w, ada_b, ffn_w_in, ffn_w_out, ssm_w_in, ssm_lambda_re, ssm_lambda_im, ssm_log_step, ssm_b_re, ssm_b_im, ssm_c_re, ssm_c_im, ssm_d, ssm_w_glu, na_w_qkv, na_q_norm, na_k_norm, na_rpb, na_w_o):
    b, seq, d = x.shape
    ctx_rows = ctx.shape[1]
    depth = norm_g.shape[0]
    assert b + 1 <= 8 and ctx_rows == TOKEN_TILE and depth == 2

    cvec = jnp.zeros((8, d), F32).at[:b].set(c).at[b].set(c_ctx)
    m = _ada_modulation(cvec, b + 1, ada_w, ada_b).reshape(depth, 8, N_SUB, 3, d)

    def mod_rows(layer, sub):
        lat = m[layer, :b, sub]
        cx = jnp.broadcast_to(m[layer, b, sub], (b, 3, d))
        return jnp.concatenate([cx, lat], axis=1)

    def gain(layer, sub):
        return norm_g[layer, sub].reshape(1, d)

    def ffn(layer, which, w_in, w_out):
        sub = 2 * which
        return mod_rows(layer, sub), gain(layer, sub), w_in, w_out, ()

    def ffn_f32(layer, which):
        return [(ffn_w_in, (layer, which)), (ffn_w_out, (layer, which))]

    xs, u, w_in, w_out, w_glu = _stage(
        x, ffn(0, 0, ffn_w_in[0, 0].astype(BF16), ffn_w_out[0, 0].astype(BF16)),
        pre=("join", ctx), post=("s5", mod_rows(0, 1), gain(0, 1), ssm_w_in[0].astype(BF16)),
        convert=ffn_f32(0, 1) + [(ssm_w_glu, (0,))], ctx_rows=ctx_rows,
        tile=S5_STAGE_TILE, name="ffn_s5in")
    n_chunks = (ctx_rows + seq) // S5_CHUNK
    assert n_chunks % SUBLANES == 0 and ctx_rows % S5_CHUNK == 0
    mats = _s5_prep(ssm_lambda_re[0], ssm_lambda_im[0], ssm_log_step[0], ssm_b_re[0], ssm_b_im[0],
                    ssm_c_re[0], ssm_c_im[0], S5_CHUNK, n_chunks // SUBLANES)
    y = _s5_scan(u, ssm_d[0].reshape(1, d), mats, S5_CHUNK, ctx_rows)
    xs, w_in, w_out, w_qkv = _stage(
        xs, ffn(0, 1, w_in, w_out), pre=("s5", y, mod_rows(0, 1), w_glu),
        convert=ffn_f32(1, 0) + [(na_w_qkv, (0,))], ctx_rows=ctx_rows,
        tile=S5_STAGE_TILE, name="s5out_ffn")

    xs, q, kv, w_in, w_out, w_o = _stage(
        xs, ffn(1, 0, w_in, w_out),
        post=("qkv", mod_rows(1, 1), gain(1, 1), w_qkv, na_q_norm[0], na_k_norm[0]),
        convert=ffn_f32(1, 1) + [(na_w_o, (0,))], ctx_rows=ctx_rows,
        tile=TOKEN_TILE, name="ffn_qkv")
    bias = _na_bias(na_rpb[0], seq // GRID_W)
    attn = _na_attention(q, kv, bias, ctx_rows)
    out, = _stage(xs, ffn(1, 1, w_in, w_out), pre=("na", attn, mod_rows(1, 1), w_o),
                  ctx_rows=ctx_rows, tile=LATENT_TOKEN_TILE, name="naout_ffn")
    return out
```

```python
import functools
import math

import jax
import jax.numpy as jnp
from jax import lax
from jax.experimental import pallas as pl
from jax.experimental.pallas import tpu as pltpu

F32 = jnp.float32
BF16 = jnp.bfloat16

GRID_W = 64
N_SUB = 3
MACARON_WEIGHT = 0.5
RMS_EPS = 1e-6
S5_GROUP = 16
S5_STATE = 64
S5_MIN_NEG_RE = -1e-4
NA_HEADS = 16
WIN_H = 8
WIN_W = 16

LANES = 128
SUBLANES = 8
MXU_DIM = 256
VMEM_LIMIT = 56 * 1024 * 1024

ADA_COLUMN_TILE = 2304
TOKEN_TILE = 256
S5_STAGE_TILE = 384
LATENT_TOKEN_TILE = 512
WEIGHT_CAST_CHUNKS = 16
BF16_SUBLANES = 16
S5_CHUNK = 8
S5_LANE_GROUPS = LANES // S5_GROUP
NA_Q_ROWS = 8
NA_K_ROWS = 16
NA_TILES_PER_STEP = 4
NA_SUB_Q_ROWS = 4
NA_SUB_K_ROWS = 12
LOG2E = math.log2(math.e)


def _dot(a, b):
    return jnp.dot(a, b, preferred_element_type=F32)


def _dot_nt(a, b):
    return lax.dot_general(a, b, (((1,), (1,)), ((), ())), preferred_element_type=F32)


def _dot_nt_f32(a, b):
    return lax.dot_general(a, b, (((1,), (1,)), ((), ())), preferred_element_type=F32,
                           precision=lax.Precision.HIGHEST)


def _split_bf16(x):
    hi = x.astype(BF16)
    return hi, (x - hi.astype(F32)).astype(BF16)


def _params(*sem):
    return pltpu.CompilerParams(dimension_semantics=sem, vmem_limit_bytes=VMEM_LIMIT)


def _resident(shape, index_map):
    return pl.BlockSpec(shape, index_map, pipeline_mode=pl.Buffered(1))


def _ada_kernel(ct_ref, w_ref, b_ref, o_ref, *, n_rows):
    ct = ct_ref[...]
    s = ct * jax.nn.sigmoid(ct)
    w = w_ref[0]
    rows = [jnp.sum(w * s[:, r:r + 1], axis=0, keepdims=True) + b_ref[0] for r in range(n_rows)]
    rows.append(jnp.zeros((SUBLANES - n_rows, w.shape[1]), F32))
    o_ref[0] = jnp.concatenate(rows, axis=0)


def _ada_modulation(cvec, n_rows, ada_w, ada_b):
    depth, d, n = ada_w.shape
    tn = ADA_COLUMN_TILE
    assert n % tn == 0
    return pl.pallas_call(
        functools.partial(_ada_kernel, n_rows=n_rows),
        grid=(depth, n // tn),
        in_specs=[pl.BlockSpec((d, SUBLANES), lambda l, j: (0, 0)),
                  pl.BlockSpec((1, d, tn), lambda l, j: (l, 0, j)),
                  pl.BlockSpec((1, 1, tn), lambda l, j: (l, 0, j))],
        out_specs=pl.BlockSpec((1, SUBLANES, tn), lambda l, j: (l, 0, j)),
        out_shape=jax.ShapeDtypeStruct((depth, SUBLANES, n), F32),
        compiler_params=_params("parallel", "parallel"),
        name="ada_modulation",
    )(cvec.T, ada_w, ada_b.reshape(depth, 1, n))


def _row_is_context(tm, first_row, ctx_rows):
    return (first_row + lax.broadcasted_iota(jnp.int32, (tm, 1), 0)) < ctx_rows


def _modulated_norm(x, g, mod, first_row, ctx_rows):
    tm = x.shape[0]
    y = x * lax.rsqrt(jnp.mean(x * x, axis=-1, keepdims=True) + RMS_EPS) * g
    if ctx_rows == 0:
        shift, scale, gate = mod[3:4], mod[4:5], mod[5:6]
    else:
        is_ctx = _row_is_context(tm, first_row, ctx_rows)
        shift = jnp.where(is_ctx, mod[0:1], mod[3:4])
        scale = jnp.where(is_ctx, mod[1:2], mod[4:5])
        gate = jnp.where(is_ctx, mod[2:3], mod[5:6])
    return y * (1.0 + scale) + shift, gate


def _gelu_tanh(y):
    return 0.5 * y * (1.0 + jnp.tanh(math.sqrt(2.0 / math.pi) * (y + 0.044715 * (y * y * y))))


def _stage_kernel(*refs, pre, post, n_convert, ctx_rows, hidden, head_dim):
    it = iter(refs)
    x_ref = next(it)
    tm, d = x_ref.shape[1], x_ref.shape[2]
    first_row = pl.program_id(1) * tm
    x = x_ref[0]
    if pre == "join":
        head = next(it)[0]
        if ctx_rows < tm:
            head = jnp.concatenate([head, x[:tm - ctx_rows]], axis=0)
        x = jnp.where(pl.program_id(1) == 0, head, x)

    def gate_of(mod):
        if ctx_rows == 0:
            return mod[5:6]
        return jnp.where(_row_is_context(tm, first_row, ctx_rows), mod[2:3], mod[5:6])

    if pre == "s5":
        y_ref, pmod_ref, wglu_ref = next(it), next(it), next(it)
        z = _dot(_gelu_tanh(y_ref[0]).astype(BF16), wglu_ref[...])
        x = x + gate_of(pmod_ref[0]) * (z[:, :d] * jax.nn.sigmoid(z[:, d:]))
    elif pre == "na":
        a_ref, pmod_ref, wo_ref = next(it), next(it), next(it)
        a = jnp.concatenate([a_ref[0, p] for p in range(a_ref.shape[1])], axis=-1)
        x = x + gate_of(pmod_ref[0]) * _dot(a, wo_ref[...])

    mod_ref, g_ref, win_ref, wout_ref = next(it), next(it), next(it), next(it)
    h, gate = _modulated_norm(x, g_ref[...], mod_ref[0], first_row, ctx_rows)
    hb = h.astype(BF16)
    gt = _dot(hb, win_ref[:, :hidden])
    up = _dot(hb, win_ref[:, hidden:])
    act = (gt * jax.nn.sigmoid(gt) * up).astype(BF16)
    x = x + (MACARON_WEIGHT * gate) * _dot(act, wout_ref[...])

    if post == "s5":
        qmod_ref, qg_ref, w_ref = next(it), next(it), next(it)
    elif post == "qkv":
        (qmod_ref, qg_ref, w_ref, reduce_ref, expand_ref,
         qgain_ref, kgain_ref) = (next(it) for _ in range(7))

    cast_in = [next(it) for _ in range(n_convert)]
    cast_out = refs[len(refs) - n_convert:]

    @pl.when(pl.program_id(0) * pl.num_programs(1) + pl.program_id(1) < WEIGHT_CAST_CHUNKS)
    def _():
        for src, dst in zip(cast_in, cast_out):
            dst[...] = src[...].astype(BF16)

    o_ref = next(it)
    o_ref[0] = x
    if post is None:
        return
    h2, _ = _modulated_norm(x, qg_ref[...], qmod_ref[0], first_row, ctx_rows)
    proj = _dot(h2.astype(BF16), w_ref[...])
    if post == "s5":
        next(it)[0] = proj
        return

    def head_norm(z, gain):
        ms = _dot((z * z).astype(BF16), reduce_ref[...]) * (1.0 / head_dim)
        hi, lo = _split_bf16(lax.rsqrt(ms + RMS_EPS))
        return z * _dot(jnp.concatenate([hi, lo], axis=1), expand_ref[...]) * gain

    q = head_norm(proj[:, :d], qgain_ref[...]) * (head_dim ** -0.5 * LOG2E)
    k = head_norm(proj[:, d:2 * d], kgain_ref[...])
    v = proj[:, 2 * d:]
    q_ref, kv_ref = next(it), next(it)
    for hp in range(d // LANES):
        sl = slice(hp * LANES, (hp + 1) * LANES)
        q_ref[0, hp] = q[:, sl].astype(BF16)
        kv_ref[0, hp, :, :LANES] = k[:, sl].astype(BF16)
        kv_ref[0, hp, :, LANES:] = v[:, sl].astype(BF16)


def _stage(xs, ffn, *, pre=None, post=None, convert=(), ctx_rows, tile, name):
    b, s, d = xs.shape
    tm = tile
    mod, gain, w_in, w_out, which = ffn
    hidden = w_out.shape[-2]
    head_dim = d // NA_HEADS
    tok = pl.BlockSpec((1, tm, d), lambda i, t: (i, t, 0))
    modspec = pl.BlockSpec((1, 6, d), lambda i, t: (i, 0, 0))
    vec = pl.BlockSpec((1, d), lambda i, t: (0, 0))

    def weight(w, lead=()):
        return _resident((None,) * len(lead) + w.shape[len(lead):], lambda i, t: lead + (0, 0))

    def rows_from(offset):
        step = math.gcd(tm, ctx_rows)
        n_tile, n_ctx = tm // step, ctx_rows // step
        return pl.BlockSpec(tuple(pl.Element(n) for n in (1, tm, d)),
                            lambda i, t: (i, offset(t, n_tile, n_ctx) * step, 0))

    args = [xs]
    if pre is not None and pre[0] == "join":
        assert tm >= ctx_rows
        s += ctx_rows
        args.append(pre[1])
        in_specs = [rows_from(lambda t, n, c: jnp.maximum(t * n - c, 0)),
                    pl.BlockSpec((1, ctx_rows, d), lambda i, t: (i, 0, 0))]
    elif pre is not None and pre[0] == "na":
        in_specs = [rows_from(lambda t, n, c: t * n + c)]
        s, ctx_rows = s - ctx_rows, 0
    else:
        in_specs = [tok]
    if pre is not None and pre[0] == "join":
        pass
    elif pre is not None and pre[0] == "s5":
        _, y, pmod, w_glu = pre
        args += [y, pmod, w_glu]
        in_specs += [tok, modspec, weight(w_glu)]
    elif pre is not None:
        _, attn, pmod, w_o = pre
        args += [attn, pmod, w_o]
        in_specs += [pl.BlockSpec((1, attn.shape[1], tm, LANES), lambda i, t: (i, 0, t, 0)),
                     modspec, weight(w_o)]
    args += [mod, gain, w_in, w_out]
    in_specs += [modspec, vec, weight(w_in, which), weight(w_out, which)]
    out_shape = [jax.ShapeDtypeStruct((b, s, d), F32)]
    out_specs = [tok]
    if post is not None and post[0] == "s5":
        _, qmod, qgain, w = post
        args += [qmod, qgain, w]
        in_specs += [modspec, vec, weight(w)]
        out_shape.append(jax.ShapeDtypeStruct((b, s, d), F32))
        out_specs.append(tok)
    elif post is not None:
        _, qmod, qgain, w, q_gain, k_gain = post
        hid = jnp.arange(d) // head_dim
        to_head = (hid[:, None] == jnp.arange(LANES)[None, :]).astype(BF16)
        from_head = jnp.concatenate([to_head.T, to_head.T], axis=0)
        args += [qmod, qgain, w, to_head, from_head,
                 jnp.tile(q_gain, NA_HEADS).reshape(1, d), jnp.tile(k_gain, NA_HEADS).reshape(1, d)]
        in_specs += [modspec, vec, weight(w), weight(to_head), weight(from_head), vec, vec]
        hp = d // LANES
        out_shape += [jax.ShapeDtypeStruct((b, hp, s, LANES), BF16),
                      jax.ShapeDtypeStruct((b, hp, s, 2 * LANES), BF16)]
        out_specs += [pl.BlockSpec((1, hp, tm, LANES), lambda i, t: (i, 0, t, 0)),
                      pl.BlockSpec((1, hp, tm, 2 * LANES), lambda i, t: (i, 0, t, 0))]
    n_t = s // tm
    for w, lead in convert:
        rows, cols = w.shape[-2:]
        chunk = rows // WEIGHT_CAST_CHUNKS
        assert chunk * WEIGHT_CAST_CHUNKS == rows and chunk % BF16_SUBLANES == 0
        assert b * n_t >= WEIGHT_CAST_CHUNKS

        def chunk_of(i, t):
            return jnp.minimum(i * n_t + t, WEIGHT_CAST_CHUNKS - 1)

        args.append(w)
        in_specs.append(pl.BlockSpec((None,) * len(lead) + (chunk, cols),
                                     lambda i, t, lead=lead: lead + (chunk_of(i, t), 0)))
        out_shape.append(jax.ShapeDtypeStruct((rows, cols), BF16))
        out_specs.append(pl.BlockSpec((chunk, cols), lambda i, t: (chunk_of(i, t), 0)))
    return pl.pallas_call(
        functools.partial(_stage_kernel, pre=None if pre is None else pre[0],
                          post=None if post is None else post[0], n_convert=len(convert),
                          ctx_rows=ctx_rows, hidden=hidden, head_dim=head_dim),
        grid=(b, n_t),
        in_specs=in_specs,
        out_specs=out_specs,
        out_shape=out_shape,
        compiler_params=_params("arbitrary", "arbitrary"),
        name=name,
    )(*args)


def _s5_prep_kernel(lr_ref, li_ref, ls_ref, btr_ref, bti_ref, cr_ref, ci_ref,
                    tpd_ref, wst_ref, prdt_ref, avec_ref, *, chunk, seg_chunks):
    n = LANES
    ng = S5_LANE_GROUPS
    half = n // 2
    wide = ng * half
    lr = jnp.minimum(lr_ref[0], S5_MIN_NEG_RE)
    li = li_ref[0]
    dt = jnp.exp(ls_ref[0])
    mag = jnp.exp(lr * dt)
    ar = mag * jnp.cos(li * dt)
    ai = mag * jnp.sin(li * dt)
    den = lr * lr + li * li
    zr = ((ar - 1.0) * lr + ai * li) / den
    zi = (ai * lr - (ar - 1.0) * li) / den
    btr, bti = btr_ref[0], bti_ref[0]
    bbr = zr * btr - zi * bti
    bbi = zr * bti + zi * btr
    cr, ci = cr_ref[0], ci_ref[0]

    lane = lax.broadcasted_iota(jnp.int32, (n, n), 1)
    row = lax.broadcasted_iota(jnp.int32, (n, n), 0)
    first_half = lane < half
    same_group = (row // S5_GROUP) == (lane // S5_GROUP)
    row_w = lax.broadcasted_iota(jnp.int32, (n, wide), 0)
    lane_w = lax.broadcasted_iota(jnp.int32, (n, wide), 1)
    own_states = (row_w // S5_GROUP) == (lane_w // half)

    def spread(v):
        return jnp.where(own_states, jnp.concatenate([v] * (wide // n), axis=1), 0.0).astype(BF16)

    powers = [(jnp.ones((n, n), F32), jnp.zeros((n, n), F32))]
    for _ in range(chunk):
        er, ei = powers[-1]
        powers.append((er * ar - ei * ai, er * ai + ei * ar))

    def input_map(k):
        er, ei = powers[k]
        return bbr * er - bbi * ei, bbr * ei + bbi * er

    def output_map(k):
        er, ei = powers[k]
        return cr * er - ci * ei, cr * ei + ci * er

    c2 = jnp.where(first_half, cr, -ci)
    lag_blocks = []
    for k in range(chunk):
        wr, wi = input_map(k)
        kd = _dot_nt_f32(jnp.where(first_half, wr, wi), c2)
        lag_blocks.append(jnp.where(same_group, kd, 0.0).astype(BF16))
    zero_block = jnp.zeros((n, n), BF16)

    def emit(reverse):
        for s in range(chunk):
            wr, wi = input_map(s if reverse else chunk - 1 - s)
            wst_ref[0, 0, s * n:(s + 1) * n, 0:wide] = spread(wr)
            wst_ref[0, 0, s * n:(s + 1) * n, wide:2 * wide] = spread(wi)
        for t in range(chunk):
            pr, pi = output_map(chunk - t if reverse else t + 1)
            prdt_ref[0, 0, t * n:(t + 1) * n, 0:wide] = spread(pr)
            prdt_ref[0, 0, t * n:(t + 1) * n, wide:2 * wide] = spread(-pi)
        for s in range(chunk):
            for t in range(chunk):
                lag = s - t if reverse else t - s
                tpd_ref[0, 0, s * n:(s + 1) * n, t * n:(t + 1) * n] = (
                    lag_blocks[lag] if lag >= 0 else zero_block)

    @pl.when(pl.program_id(0) == 0)
    def _():
        emit(False)

    @pl.when(pl.program_id(0) == 1)
    def _():
        emit(True)

    er, ei = powers[chunk]
    sr, si = jnp.ones((n, n), F32), jnp.zeros((n, n), F32)
    pr_, pi_ = er, ei
    e = seg_chunks
    while e:
        if e & 1:
            sr, si = sr * pr_ - si * pi_, sr * pi_ + si * pr_
        pr_, pi_ = pr_ * pr_ - pi_ * pi_, 2.0 * pr_ * pi_
        e >>= 1
    for r, v in enumerate((er, ei, sr, si)):
        for j in range(ng // 2):
            g0 = 2 * j * S5_GROUP
            g1 = g0 + S5_GROUP
            avec_ref[0, 0, r:r + 1, j * n:(j + 1) * n] = jnp.where(
                first_half[0:1], v[g0:g0 + 1], v[g1:g1 + 1])
    avec_ref[0, 0, 4:8, :] = jnp.zeros((4, wide), F32)


def _s5_prep(lam_re, lam_im, log_step, b_re, b_im, c_re, c_im, chunk, seg_chunks):
    nd, g, p = lam_re.shape
    gs = S5_GROUP
    assert 2 * p == LANES and g % S5_LANE_GROUPS == 0
    n_blocks = g // S5_LANE_GROUPS
    tk = chunk * LANES
    wide = S5_LANE_GROUPS * p

    def rows(v):
        v = v.reshape(nd, g * gs, p)
        return jnp.concatenate([v, v], axis=-1)

    def per_group(v):
        return rows(jnp.broadcast_to(v[:, :, None, :], (nd, g, gs, p)))

    args = (per_group(lam_re), per_group(lam_im),
            per_group(jnp.broadcast_to(log_step[:, :, None], (nd, g, p))),
            rows(jnp.swapaxes(b_re, 2, 3)), rows(jnp.swapaxes(b_im, 2, 3)), rows(c_re), rows(c_im))
    sq = pl.BlockSpec((1, LANES, LANES), lambda d, i: (d, i, 0))
    big = pl.BlockSpec((1, 1, tk, tk), lambda d, i: (d, i, 0, 0))
    assert tk == 2 * wide
    return pl.pallas_call(
        functools.partial(_s5_prep_kernel, chunk=chunk, seg_chunks=seg_chunks),
        grid=(nd, n_blocks),
        in_specs=[sq] * 7,
        out_specs=[big, big, big, pl.BlockSpec((1, 1, 8, wide), lambda d, i: (d, i, 0, 0))],
        out_shape=[jax.ShapeDtypeStruct((nd, n_blocks, tk, tk), BF16)] * 3
                  + [jax.ShapeDtypeStruct((nd, n_blocks, 8, wide), F32)],
        compiler_params=_params("parallel", "parallel"),
        name="s5_prep",
    )(*args)


def _s5_scan_kernel(u_ref, dsk_ref, tpd_ref, wst_ref, prdt_ref, avec_ref, y_ref,
                    xcat, sv, *, chunk, n_ctx):
    n = LANES
    n_c = xcat.shape[0]
    nb = sv.shape[0] // 2
    n_seg = SUBLANES
    seg = n_c // n_seg
    n_lat = n_c - n_ctx
    direction = pl.program_id(2)

    @pl.when(direction == 0)
    def _():
        for s in range(chunk):
            xcat[:, s * n:(s + 1) * n] = u_ref[0, pl.ds(s, n_c, stride=chunk), :].astype(BF16)

    def run(reverse):
        wst = wst_ref[0, 0]

        def summarise(dst, src):
            s = _dot(xcat[src, :], wst)
            for j in range(2 * nb):
                sv[j, dst, :] = s[:, j * n:(j + 1) * n]

        if reverse:
            summarise(slice(0, n_lat), slice(n_ctx, n_c))
            summarise(slice(n_lat, n_c), slice(0, n_ctx))
        else:
            summarise(slice(0, n_c), slice(0, n_c))

        blk = MXU_DIM
        n_blk = xcat.shape[1] // blk
        intra = []
        for j in range(n_blk):
            ks = slice(j * blk, n_blk * blk) if reverse else slice(0, (j + 1) * blk)
            intra.append(_dot(xcat[:, ks], tpd_ref[0, 0, ks, j * blk:(j + 1) * blk]))
        intra = jnp.concatenate(intra, axis=1)

        def lane_blocks(r):
            return [avec_ref[0, 0, r:r + 1, j * n:(j + 1) * n] for j in range(nb)]

        ar = [jnp.broadcast_to(v, (n_seg, n)) for v in lane_blocks(0)]
        ai = [jnp.broadcast_to(v, (n_seg, n)) for v in lane_blocks(1)]

        def rows_at(i):
            return pl.ds(seg - 1 - i if reverse else i, n_seg, stride=seg)

        def advance(h, rows):
            out_r, out_i = [], []
            for j in range(nb):
                hr, hi = h[j], h[nb + j]
                out_r.append(ar[j] * hr - ai[j] * hi + sv[j, rows, :])
                out_i.append(ar[j] * hi + ai[j] * hr + sv[nb + j, rows, :])
            return tuple(out_r + out_i)

        zero = jnp.zeros((n_seg, n), F32)
        fin = lax.fori_loop(0, seg, lambda i, h: advance(h, rows_at(i)), (zero,) * (2 * nb),
                            unroll=True)

        asr, asi = lane_blocks(2), lane_blocks(3)
        order = range(n_seg - 1, -1, -1) if reverse else range(n_seg)
        h0 = []
        for j in range(nb):
            rows_r, rows_i = [None] * n_seg, [None] * n_seg
            pr = pi = jnp.zeros((1, n), F32)
            prev = None
            for sgm in order:
                if prev is not None:
                    pr, pi = (asr[j] * pr - asi[j] * pi + fin[j][prev:prev + 1],
                              asr[j] * pi + asi[j] * pr + fin[nb + j][prev:prev + 1])
                rows_r[sgm], rows_i[sgm] = pr, pi
                prev = sgm
            h0.append((jnp.concatenate(rows_r, axis=0), jnp.concatenate(rows_i, axis=0)))
        h0 = tuple(v[0] for v in h0) + tuple(v[1] for v in h0)

        def step(i, h):
            rows = rows_at(i)
            nxt = advance(h, rows)
            for j in range(2 * nb):
                sv[j, rows, :] = h[j]
            return nxt

        lax.fori_loop(0, seg, step, h0, unroll=True)

        def entry_states(rows):
            return jnp.concatenate([sv[j, rows, :] for j in range(2 * nb)], axis=1).astype(BF16)

        if reverse:
            h_all = jnp.concatenate([entry_states(slice(n_lat, n_c)), entry_states(slice(0, n_lat))],
                                    axis=0)
        else:
            h_all = entry_states(slice(0, n_c))
        return intra + _dot_nt(h_all, prdt_ref[0, 0])

    @pl.when(direction == 0)
    def _():
        y = run(False)
        for t in range(chunk):
            rows = pl.ds(t, n_c, stride=chunk)
            y_ref[0, rows, :] = y[:, t * n:(t + 1) * n] + u_ref[0, rows, :] * dsk_ref[...]

    @pl.when(direction == 1)
    def _():
        y = run(True)
        for t in range(chunk):
            rows = pl.ds(t, n_c, stride=chunk)
            y_ref[0, rows, :] = y_ref[0, rows, :] + y[:, t * n:(t + 1) * n]


def _s5_scan(u, d_skip, mats, chunk, ctx_rows):
    b, s, d = u.shape
    tpd, wst, prdt, avec = mats
    nd, n_blocks, tk, _ = tpd.shape
    wide = avec.shape[-1]
    n_c = s // chunk

    def mat(shape):
        return pl.BlockSpec((1, 1) + shape, lambda o, i, dr: (dr, o, 0, 0))

    tok = pl.BlockSpec((1, s, LANES), lambda o, i, dr: (i, 0, o))
    return pl.pallas_call(
        functools.partial(_s5_scan_kernel, chunk=chunk, n_ctx=ctx_rows // chunk),
        grid=(n_blocks, b, nd),
        in_specs=[tok, pl.BlockSpec((1, LANES), lambda o, i, dr: (0, o)),
                  mat((tk, tk)), mat((tk, tk)), mat((tk, tk)), mat((8, wide))],
        out_specs=tok,
        out_shape=jax.ShapeDtypeStruct((b, s, d), F32),
        scratch_shapes=[pltpu.VMEM((n_c, tk), BF16),
                        pltpu.VMEM((2 * wide // LANES, n_c, LANES), F32)],
        compiler_params=_params("parallel", "parallel", "arbitrary"),
        name="s5_scan",
    )(u, d_skip, tpd, wst, prdt, avec)


def _window_start(r, rows):
    kh = min(WIN_H, rows)
    return min(max(r - kh // 2, 0), rows - kh)


def _na_tile_geometry(kind, rows):
    n_tiles = rows // NA_Q_ROWS
    tile = {0: 0, 1: 1, 2: n_tiles - 1}[kind]
    q0 = tile * NA_Q_ROWS
    k0 = min(max(q0 - (NA_K_ROWS - NA_Q_ROWS) // 2, 0), rows - NA_K_ROWS)
    return q0, k0


def _na_sub_window(kind, sub, rows):
    q0, k0 = _na_tile_geometry(kind, rows)
    kh = min(WIN_H, rows)
    starts = [_window_start(q0 + NA_SUB_Q_ROWS * sub + i, rows) - k0 for i in range(NA_SUB_Q_ROWS)]
    first = min(min(starts) // NA_SUB_Q_ROWS * NA_SUB_Q_ROWS, NA_K_ROWS - NA_SUB_K_ROWS)
    assert first >= 0 and max(starts) + kh <= first + NA_SUB_K_ROWS
    return first


def _na_bias_kernel(rpb_ref, o_ref, *, rows):
    h = pl.program_id(0)
    w = GRID_W
    kh = min(WIN_H, rows)
    ncol = 2 * WIN_W - 1
    nrow = 2 * WIN_H - 1
    cq = lax.broadcasted_iota(jnp.int32, (w, LANES), 0)
    lane = lax.broadcasted_iota(jnp.int32, (w, LANES), 1)
    ck = lane % w
    left = lane < w
    cs = jnp.clip(cq - WIN_W // 2, 0, w - WIN_W)
    col_ok = (ck >= cs) & (ck < cs + WIN_W)
    dc = jnp.clip(ck - cq + WIN_W - 1, 0, ncol - 1)
    neg = jnp.full((w, LANES), -jnp.inf, F32)

    def pair_table(d_left, d_right):
        t = jnp.zeros((w, LANES), F32)
        for j in range(ncol):
            vl = rpb_ref[(h * nrow + d_left) * ncol + j] if d_left is not None else 0.0
            vr = rpb_ref[(h * nrow + d_right) * ncol + j] if d_right is not None else 0.0
            t = jnp.where(dc == j, jnp.where(left, vl, vr), t)
        t = t * LOG2E
        ok = col_ok
        if d_left is None:
            ok = ok & jnp.logical_not(left)
        if d_right is None:
            ok = ok & left
        return jnp.where(ok, t, neg)

    cache = {}
    for kind in range(3):
        q0, k0 = _na_tile_geometry(kind, rows)
        for rq in range(NA_Q_ROWS):
            r = q0 + rq
            rs = _window_start(r, rows)
            first = k0 + _na_sub_window(kind, rq // NA_SUB_Q_ROWS, rows)
            for m in range(NA_SUB_K_ROWS // 2):
                ds = []
                for kr in (first + 2 * m, first + 2 * m + 1):
                    ds.append(kr - r + WIN_H - 1 if rs <= kr < rs + kh else None)
                key = tuple(ds)
                if key == (None, None):
                    blk = neg
                else:
                    if key not in cache:
                        cache[key] = pair_table(*key)
                    blk = cache[key]
                o_ref[kind, 0, rq * w:(rq + 1) * w, m * LANES:(m + 1) * LANES] = blk


def _na_bias(rpb, rows):
    nh = rpb.shape[0]
    nq, nk = NA_Q_ROWS * GRID_W, NA_SUB_K_ROWS * GRID_W
    return pl.pallas_call(
        functools.partial(_na_bias_kernel, rows=rows),
        grid=(nh,),
        in_specs=[pl.BlockSpec(memory_space=pltpu.SMEM)],
        out_specs=pl.BlockSpec((3, 1, nq, nk), lambda h: (0, h, 0, 0)),
        out_shape=jax.ShapeDtypeStruct((3, nh, nq, nk), F32),
        compiler_params=_params("parallel"),
        name="na_bias",
    )(rpb.reshape(-1))


def _na_attn_kernel(*refs, n_q, n_tiles):
    n_sub = n_tiles * n_q
    q_ref = refs[0]
    kv_refs = refs[1:1 + n_sub]
    kvc_ref = refs[1 + n_sub]
    bias_refs = refs[2 + n_sub:2 + n_sub + n_tiles]
    o_ref = refs[2 + n_sub + n_tiles]
    tb = kvc_ref.shape[2]
    half = LANES // 2

    def per_head(x, other):
        first_head = lax.broadcasted_iota(jnp.int32, x.shape, 1) < half
        fill = jnp.full_like(x, other)
        return [jnp.where(first_head, x, fill), jnp.where(first_head, fill, x)]

    kc = kvc_ref[0, 0, :, :LANES]
    vc_heads = per_head(kvc_ref[0, 0, :, LANES:], 1.0)
    units = [(sub, hh) for sub in range(n_sub) for hh in range(2)]
    keys, values, queries = {}, {}, {}
    for sub in range(n_sub):
        keys[sub] = kv_refs[sub][0, 0, :, :LANES]
        values[sub] = per_head(kv_refs[sub][0, 0, :, LANES:], 1.0)
        queries[sub] = per_head(q_ref[0, 0, sub * tb:(sub + 1) * tb, :], 0.0)

    def score(unit):
        sub, hh = unit
        qh = queries[sub][hh]
        in_tile = sub % n_q
        bias = bias_refs[sub // n_q][0, hh, in_tile * tb:(in_tile + 1) * tb, :]
        return _dot_nt(qh, keys[sub]) + bias, _dot_nt(qh, kc)

    ahead = 8
    pending = [score(u) for u in units[:ahead]]
    outs = {}
    for i, (sub, hh) in enumerate(units):
        s, sc = pending.pop(0)
        if i + ahead < len(units):
            pending.append(score(units[i + ahead]))
        m = jnp.maximum(jnp.max(s, axis=-1, keepdims=True), jnp.max(sc, axis=-1, keepdims=True))
        p = jnp.exp2(s - m).astype(BF16)
        pc = jnp.exp2(sc - m).astype(BF16)
        o = _dot(p, values[sub][hh]) + _dot(pc, vc_heads[hh])
        denom = o[:, half:half + 1] if hh == 0 else o[:, 0:1]
        outs[sub, hh] = o / denom
    first_head = lax.broadcasted_iota(jnp.int32, (tb, LANES), 1) < half
    for sub in range(n_sub):
        o_ref[0, 0, sub * tb:(sub + 1) * tb, :] = jnp.where(
            first_head, outs[sub, 0], outs[sub, 1]).astype(BF16)


def _na_attention(q, kv, bias, ctx_rows):
    b, hp, s, _ = q.shape
    seq = s - ctx_rows
    tb = TOKEN_TILE
    assert ctx_rows == tb and kv.shape[2] == s
    rows = seq // GRID_W
    n_tiles = rows // NA_Q_ROWS
    assert n_tiles >= 3
    assert NA_SUB_Q_ROWS * GRID_W == tb
    n_q = NA_Q_ROWS // NA_SUB_Q_ROWS
    n_k = NA_SUB_K_ROWS * GRID_W // tb
    slab = NA_K_ROWS * GRID_W // tb
    last_k0 = (rows - NA_K_ROWS) * GRID_W // tb

    per_step = NA_TILES_PER_STEP
    assert n_tiles % per_step == 0

    def pick(tile, by_kind):
        return jnp.where(tile == 0, by_kind[0], jnp.where(tile == n_tiles - 1, by_kind[2], by_kind[1]))

    def window_map(j, sub):
        offs = [_na_sub_window(kind, sub, rows) * GRID_W // tb for kind in range(3)]

        def f(p, bi, t):
            tile = per_step * t + j
            first = jnp.clip(n_q * tile - (slab - n_q) // 2, 0, last_k0)
            return (bi, p, (1 + first + pick(tile, offs)) * tb, 0)
        return f

    def bias_map(j):
        return lambda p, bi, t: (pick(per_step * t + j, (0, 1, 2)), p, 0, 0)

    nq = per_step * NA_Q_ROWS * GRID_W
    step_spec = pl.BlockSpec((1, 1, nq, LANES), lambda p, bi, t: (bi, p, t, 0))
    window_shape = tuple(pl.Element(n) for n in (1, 1, n_k * tb, 2 * LANES))
    kv_specs = [pl.BlockSpec(window_shape, window_map(j, sub))
                for j in range(per_step) for sub in range(n_q)]
    bias_specs = [pl.BlockSpec((1, 2) + bias.shape[2:], bias_map(j)) for j in range(per_step)]
    q_spec = pl.BlockSpec(tuple(pl.Element(n) for n in (1, 1, nq, LANES)),
                          lambda p, bi, t: (bi, p, (1 + t * (nq // tb)) * tb, 0))
    in_specs = ([q_spec] + kv_specs
                + [pl.BlockSpec((1, 1, tb, 2 * LANES), lambda p, bi, t: (bi, p, 0, 0))] + bias_specs)
    args = [q] + [kv] * (len(kv_specs) + 1) + [bias] * per_step
    return pl.pallas_call(
        functools.partial(_na_attn_kernel, n_q=n_q, n_tiles=per_step),
        grid=(hp, b, n_tiles // per_step),
        in_specs=in_specs,
        out_specs=step_spec,
        out_shape=jax.ShapeDtypeStruct((b, hp, seq, LANES), BF16),
        compiler_params=_params("parallel", "parallel", "parallel"),
        name="na_attention",
    )(*args)


def kernel(x, c, ctx, c_ctx, norm_g, ada_w, ada_b, ffn_w_in, ffn_w_out, ssm_w_in, ssm_lambda_re, ssm_lambda_im, ssm_log_step, ssm_b_re, ssm_b_im, ssm_c_re, ssm_c_im, ssm_d, ssm_w_glu, na_w_qkv, na_q_norm, na_k_norm, na_rpb, na_w_o):
    b, seq, d = x.shape
    ctx_rows = ctx.shape[1]
    depth = norm_g.shape[0]
    assert b + 1 <= 8 and ctx_rows == TOKEN_TILE and depth == 2

    cvec = jnp.zeros((8, d), F32).at[:b].set(c).at[b].set(c_ctx)
    m = _ada_modulation(cvec, b + 1, ada_w, ada_b).reshape(depth, 8, N_SUB, 3, d)

    def mod_rows(layer, sub):
        lat = m[layer, :b, sub]
        cx = jnp.broadcast_to(m[layer, b, sub], (b, 3, d))
        return jnp.concatenate([cx, lat], axis=1)

    def gain(layer, sub):
        return norm_g[layer, sub].reshape(1, d)

    def ffn(layer, which, w_in, w_out):
        sub = 2 * which
        return mod_rows(layer, sub), gain(layer, sub), w_in, w_out, ()

    def ffn_f32(layer, which):
        return [(ffn_w_in, (layer, which)), (ffn_w_out, (layer, which))]

    xs, u, w_in, w_out, w_glu = _stage(
        x, ffn(0, 0, ffn_w_in[0, 0].astype(BF16), ffn_w_out[0, 0].astype(BF16)),
        pre=("join", ctx), post=("s5", mod_rows(0, 1), gain(0, 1), ssm_w_in[0].astype(BF16)),
        convert=ffn_f32(0, 1) + [(ssm_w_glu, (0,))], ctx_rows=ctx_rows,
        tile=S5_STAGE_TILE, name="ffn_s5in")
    n_chunks = (ctx_rows + seq) // S5_CHUNK
    assert n_chunks % SUBLANES == 0 and ctx_rows % S5_CHUNK == 0
    mats = _s5_prep(ssm_lambda_re[0], ssm_lambda_im[0], ssm_log_step[0], ssm_b_re[0], ssm_b_im[0],
                    ssm_c_re[0], ssm_c_im[0], S5_CHUNK, n_chunks // SUBLANES)
    y = _s5_scan(u, ssm_d[0].reshape(1, d), mats, S5_CHUNK, ctx_rows)
    xs, w_in, w_out, w_qkv = _stage(
        xs, ffn(0, 1, w_in, w_out), pre=("s5", y, mod_rows(0, 1), w_glu),
        convert=ffn_f32(1, 0) + [(na_w_qkv, (0,))], ctx_rows=ctx_rows,
        tile=S5_STAGE_TILE, name="s5out_ffn")

    xs, q, kv, w_in, w_out, w_o = _stage(
        xs, ffn(1, 0, w_in, w_out),
        post=("qkv", mod_rows(1, 1), gain(1, 1), w_qkv, na_q_norm[0], na_k_norm[0]),
        convert=ffn_f32(1, 1) + [(na_w_o, (0,))], ctx_rows=ctx_rows,
        tile=TOKEN_TILE, name="ffn_qkv")
    bias = _na_bias(na_rpb[0], seq // GRID_W)
    attn = _na_attention(q, kv, bias, ctx_rows)
    out, = _stage(xs, ffn(1, 1, w_in, w_out), pre=("na", attn, mod_rows(1, 1), w_o),
                  ctx_rows=ctx_rows, tile=LATENT_TOKEN_TILE, name="naout_ffn")
    return out
```

```python
import functools
import math

import jax
import jax.numpy as jnp
from jax import lax
from jax.experimental import pallas as pl
from jax.experimental.pallas import tpu as pltpu

F32 = jnp.float32
BF16 = jnp.bfloat16

GRID_W = 64
N_SUB = 3
MACARON_WEIGHT = 0.5
RMS_EPS = 1e-6
S5_GROUP = 16
S5_STATE = 64
S5_MIN_NEG_RE = -1e-4
NA_HEADS = 16
WIN_H = 8
WIN_W = 16

LANES = 128
SUBLANES = 8
MXU_DIM = 256
VMEM_LIMIT = 56 * 1024 * 1024

ADA_COLUMN_TILE = 2304
TOKEN_TILE = 256
JOINT_STAGE_TILE = 384
LATENT_TOKEN_TILE = 512
WEIGHT_CAST_CHUNKS = 16
BF16_SUBLANES = 16
S5_CHUNK = 8
S5_LANE_GROUPS = LANES // S5_GROUP
NA_Q_ROWS = 8
NA_K_ROWS = 16
NA_TILES_PER_STEP = 4
NA_SUB_Q_ROWS = 4
NA_SUB_K_ROWS = 12
LOG2E = math.log2(math.e)


def _dot(a, b):
    return jnp.dot(a, b, preferred_element_type=F32)


def _dot_nt(a, b):
    return lax.dot_general(a, b, (((1,), (1,)), ((), ())), preferred_element_type=F32)


def _dot_nt_f32(a, b):
    return lax.dot_general(a, b, (((1,), (1,)), ((), ())), preferred_element_type=F32,
                           precision=lax.Precision.HIGHEST)


def _split_bf16(x):
    hi = x.astype(BF16)
    return hi, (x - hi.astype(F32)).astype(BF16)


def _params(*sem):
    return pltpu.CompilerParams(dimension_semantics=sem, vmem_limit_bytes=VMEM_LIMIT)


def _resident(shape, index_map):
    return pl.BlockSpec(shape, index_map, pipeline_mode=pl.Buffered(1))


def _ada_kernel(ct_ref, w_ref, b_ref, o_ref, *, n_rows):
    ct = ct_ref[...]
    s = ct * jax.nn.sigmoid(ct)
    w = w_ref[0]
    rows = [jnp.sum(w * s[:, r:r + 1], axis=0, keepdims=True) + b_ref[0] for r in range(n_rows)]
    rows.append(jnp.zeros((SUBLANES - n_rows, w.shape[1]), F32))
    o_ref[0] = jnp.concatenate(rows, axis=0)


def _ada_modulation(cvec, n_rows, ada_w, ada_b):
    depth, d, n = ada_w.shape
    tn = ADA_COLUMN_TILE
    assert n % tn == 0
    return pl.pallas_call(
        functools.partial(_ada_kernel, n_rows=n_rows),
        grid=(depth, n // tn),
        in_specs=[pl.BlockSpec((d, SUBLANES), lambda l, j: (0, 0)),
                  pl.BlockSpec((1, d, tn), lambda l, j: (l, 0, j)),
                  pl.BlockSpec((1, 1, tn), lambda l, j: (l, 0, j))],
        out_specs=pl.BlockSpec((1, SUBLANES, tn), lambda l, j: (l, 0, j)),
        out_shape=jax.ShapeDtypeStruct((depth, SUBLANES, n), F32),
        compiler_params=_params("parallel", "parallel"),
        name="ada_modulation",
    )(cvec.T, ada_w, ada_b.reshape(depth, 1, n))


def _row_is_context(tm, first_row, ctx_rows):
    return (first_row + lax.broadcasted_iota(jnp.int32, (tm, 1), 0)) < ctx_rows


def _modulated_norm(x, g, mod, first_row, ctx_rows):
    tm = x.shape[0]
    y = x * lax.rsqrt(jnp.mean(x * x, axis=-1, keepdims=True) + RMS_EPS) * g
    if ctx_rows == 0:
        shift, scale, gate = mod[3:4], mod[4:5], mod[5:6]
    else:
        is_ctx = _row_is_context(tm, first_row, ctx_rows)
        shift = jnp.where(is_ctx, mod[0:1], mod[3:4])
        scale = jnp.where(is_ctx, mod[1:2], mod[4:5])
        gate = jnp.where(is_ctx, mod[2:3], mod[5:6])
    return y * (1.0 + scale) + shift, gate


def _gelu_tanh(y):
    return 0.5 * y * (1.0 + jnp.tanh(math.sqrt(2.0 / math.pi) * (y + 0.044715 * (y * y * y))))


def _stage_kernel(*refs, pre, post, n_convert, ctx_rows, hidden, head_dim):
    it = iter(refs)
    x_ref = next(it)
    tm, d = x_ref.shape[1], x_ref.shape[2]
    first_row = pl.program_id(1) * tm
    x = x_ref[0]
    if pre == "join":
        head = next(it)[0]
        if ctx_rows < tm:
            head = jnp.concatenate([head, x[:tm - ctx_rows]], axis=0)
        x = jnp.where(pl.program_id(1) == 0, head, x)

    def gate_of(mod):
        if ctx_rows == 0:
            return mod[5:6]
        return jnp.where(_row_is_context(tm, first_row, ctx_rows), mod[2:3], mod[5:6])

    if pre == "s5":
        y_ref, pmod_ref, wglu_ref = next(it), next(it), next(it)
        z = _dot(_gelu_tanh(y_ref[0]).astype(BF16), wglu_ref[...])
        x = x + gate_of(pmod_ref[0]) * (z[:, :d] * jax.nn.sigmoid(z[:, d:]))
    elif pre == "na":
        a_ref, pmod_ref, wo_ref = next(it), next(it), next(it)
        a = jnp.concatenate([a_ref[0, p] for p in range(a_ref.shape[1])], axis=-1)
        x = x + gate_of(pmod_ref[0]) * _dot(a, wo_ref[...])

    mod_ref, g_ref, win_ref, wout_ref = next(it), next(it), next(it), next(it)
    h, gate = _modulated_norm(x, g_ref[...], mod_ref[0], first_row, ctx_rows)
    hb = h.astype(BF16)
    gt = _dot(hb, win_ref[:, :hidden])
    up = _dot(hb, win_ref[:, hidden:])
    act = (gt * jax.nn.sigmoid(gt) * up).astype(BF16)
    x = x + (MACARON_WEIGHT * gate) * _dot(act, wout_ref[...])

    if post == "s5":
        qmod_ref, qg_ref, w_ref = next(it), next(it), next(it)
    elif post == "qkv":
        (qmod_ref, qg_ref, w_ref, reduce_ref, expand_ref,
         qgain_ref, kgain_ref) = (next(it) for _ in range(7))

    cast_in = [next(it) for _ in range(n_convert)]
    cast_out = refs[len(refs) - n_convert:]

    @pl.when(pl.program_id(0) * pl.num_programs(1) + pl.program_id(1) < WEIGHT_CAST_CHUNKS)
    def _():
        for src, dst in zip(cast_in, cast_out):
            dst[...] = src[...].astype(BF16)

    o_ref = next(it)
    o_ref[0] = x
    if post is None:
        return
    h2, _ = _modulated_norm(x, qg_ref[...], qmod_ref[0], first_row, ctx_rows)
    proj = _dot(h2.astype(BF16), w_ref[...])
    if post == "s5":
        next(it)[0] = proj
        return

    def head_norm(z, gain):
        ms = _dot((z * z).astype(BF16), reduce_ref[...]) * (1.0 / head_dim)
        hi, lo = _split_bf16(lax.rsqrt(ms + RMS_EPS))
        return z * _dot(jnp.concatenate([hi, lo], axis=1), expand_ref[...]) * gain

    q = head_norm(proj[:, :d], qgain_ref[...]) * (head_dim ** -0.5 * LOG2E)
    k = head_norm(proj[:, d:2 * d], kgain_ref[...])
    v = proj[:, 2 * d:]
    q_ref, kv_ref = next(it), next(it)
    for hp in range(d // LANES):
        sl = slice(hp * LANES, (hp + 1) * LANES)
        q_ref[0, hp] = q[:, sl].astype(BF16)
        kv_ref[0, hp, :, :LANES] = k[:, sl].astype(BF16)
        kv_ref[0, hp, :, LANES:] = v[:, sl].astype(BF16)


def _stage(xs, ffn, *, pre=None, post=None, convert=(), ctx_rows, tile, name):
    b, s, d = xs.shape
    tm = tile
    mod, gain, w_in, w_out, which = ffn
    hidden = w_out.shape[-2]
    head_dim = d // NA_HEADS
    tok = pl.BlockSpec((1, tm, d), lambda i, t: (i, t, 0))
    modspec = pl.BlockSpec((1, 6, d), lambda i, t: (i, 0, 0))
    vec = pl.BlockSpec((1, d), lambda i, t: (0, 0))

    def weight(w, lead=()):
        return _resident((None,) * len(lead) + w.shape[len(lead):], lambda i, t: lead + (0, 0))

    def rows_from(offset):
        step = math.gcd(tm, ctx_rows)
        n_tile, n_ctx = tm // step, ctx_rows // step
        return pl.BlockSpec(tuple(pl.Element(n) for n in (1, tm, d)),
                            lambda i, t: (i, offset(t, n_tile, n_ctx) * step, 0))

    args = [xs]
    if pre is not None and pre[0] == "join":
        assert tm >= ctx_rows
        s += ctx_rows
        args.append(pre[1])
        in_specs = [rows_from(lambda t, n, c: jnp.maximum(t * n - c, 0)),
                    pl.BlockSpec((1, ctx_rows, d), lambda i, t: (i, 0, 0))]
    elif pre is not None and pre[0] == "na":
        in_specs = [rows_from(lambda t, n, c: t * n + c)]
        s, ctx_rows = s - ctx_rows, 0
    else:
        in_specs = [tok]
    if pre is not None and pre[0] == "join":
        pass
    elif pre is not None and pre[0] == "s5":
        _, y, pmod, w_glu = pre
        args += [y, pmod, w_glu]
        in_specs += [tok, modspec, weight(w_glu)]
    elif pre is not None:
        _, attn, pmod, w_o = pre
        args += [attn, pmod, w_o]
        in_specs += [pl.BlockSpec((1, attn.shape[1], tm, LANES), lambda i, t: (i, 0, t, 0)),
                     modspec, weight(w_o)]
    args += [mod, gain, w_in, w_out]
    in_specs += [modspec, vec, weight(w_in, which), weight(w_out, which)]
    out_shape = [jax.ShapeDtypeStruct((b, s, d), F32)]
    out_specs = [tok]
    if post is not None and post[0] == "s5":
        _, qmod, qgain, w = post
        args += [qmod, qgain, w]
        in_specs += [modspec, vec, weight(w)]
        out_shape.append(jax.ShapeDtypeStruct((b, s, d), F32))
        out_specs.append(tok)
    elif post is not None:
        _, qmod, qgain, w, q_gain, k_gain = post
        hid = jnp.arange(d) // head_dim
        to_head = (hid[:, None] == jnp.arange(LANES)[None, :]).astype(BF16)
        from_head = jnp.concatenate([to_head.T, to_head.T], axis=0)
        args += [qmod, qgain, w, to_head, from_head,
                 jnp.tile(q_gain, NA_HEADS).reshape(1, d), jnp.tile(k_gain, NA_HEADS).reshape(1, d)]
        in_specs += [modspec, vec, weight(w), weight(to_head), weight(from_head), vec, vec]
        hp = d // LANES
        out_shape += [jax.ShapeDtypeStruct((b, hp, s, LANES), BF16),
                      jax.ShapeDtypeStruct((b, hp, s, 2 * LANES), BF16)]
        out_specs += [pl.BlockSpec((1, hp, tm, LANES), lambda i, t: (i, 0, t, 0)),
                      pl.BlockSpec((1, hp, tm, 2 * LANES), lambda i, t: (i, 0, t, 0))]
    n_t = s // tm
    for w, lead in convert:
        rows, cols = w.shape[-2:]
        chunk = rows // WEIGHT_CAST_CHUNKS
        assert chunk * WEIGHT_CAST_CHUNKS == rows and chunk % BF16_SUBLANES == 0
        assert b * n_t >= WEIGHT_CAST_CHUNKS

        def chunk_of(i, t):
            return jnp.minimum(i * n_t + t, WEIGHT_CAST_CHUNKS - 1)

        args.append(w)
        in_specs.append(pl.BlockSpec((None,) * len(lead) + (chunk, cols),
                                     lambda i, t, lead=lead: lead + (chunk_of(i, t), 0)))
        out_shape.append(jax.ShapeDtypeStruct((rows, cols), BF16))
        out_specs.append(pl.BlockSpec((chunk, cols), lambda i, t: (chunk_of(i, t), 0)))
    return pl.pallas_call(
        functools.partial(_stage_kernel, pre=None if pre is None else pre[0],
                          post=None if post is None else post[0], n_convert=len(convert),
                          ctx_rows=ctx_rows, hidden=hidden, head_dim=head_dim),
        grid=(b, n_t),
        in_specs=in_specs,
        out_specs=out_specs,
        out_shape=out_shape,
        compiler_params=_params("arbitrary", "arbitrary"),
        name=name,
    )(*args)


def _s5_prep_kernel(lr_ref, li_ref, ls_ref, btr_ref, bti_ref, cr_ref, ci_ref,
                    tpd_ref, wst_ref, prdt_ref, avec_ref, *, chunk, seg_chunks):
    n = LANES
    ng = S5_LANE_GROUPS
    half = n // 2
    wide = ng * half
    lr = jnp.minimum(lr_ref[0], S5_MIN_NEG_RE)
    li = li_ref[0]
    dt = jnp.exp(ls_ref[0])
    mag = jnp.exp(lr * dt)
    ar = mag * jnp.cos(li * dt)
    ai = mag * jnp.sin(li * dt)
    den = lr * lr + li * li
    zr = ((ar - 1.0) * lr + ai * li) / den
    zi = (ai * lr - (ar - 1.0) * li) / den
    btr, bti = btr_ref[0], bti_ref[0]
    bbr = zr * btr - zi * bti
    bbi = zr * bti + zi * btr
    cr, ci = cr_ref[0], ci_ref[0]

    lane = lax.broadcasted_iota(jnp.int32, (n, n), 1)
    row = lax.broadcasted_iota(jnp.int32, (n, n), 0)
    first_half = lane < half
    same_group = (row // S5_GROUP) == (lane // S5_GROUP)
    row_w = lax.broadcasted_iota(jnp.int32, (n, wide), 0)
    lane_w = lax.broadcasted_iota(jnp.int32, (n, wide), 1)
    own_states = (row_w // S5_GROUP) == (lane_w // half)

    def spread(v):
        return jnp.where(own_states, jnp.concatenate([v] * (wide // n), axis=1), 0.0).astype(BF16)

    powers = [(jnp.ones((n, n), F32), jnp.zeros((n, n), F32))]
    for _ in range(chunk):
        er, ei = powers[-1]
        powers.append((er * ar - ei * ai, er * ai + ei * ar))

    def input_map(k):
        er, ei = powers[k]
        return bbr * er - bbi * ei, bbr * ei + bbi * er

    def output_map(k):
        er, ei = powers[k]
        return cr * er - ci * ei, cr * ei + ci * er

    c2 = jnp.where(first_half, cr, -ci)
    lag_blocks = []
    for k in range(chunk):
        wr, wi = input_map(k)
        kd = _dot_nt_f32(jnp.where(first_half, wr, wi), c2)
        lag_blocks.append(jnp.where(same_group, kd, 0.0).astype(BF16))
    zero_block = jnp.zeros((n, n), BF16)

    def emit(reverse):
        for s in range(chunk):
            wr, wi = input_map(s if reverse else chunk - 1 - s)
            wst_ref[0, 0, s * n:(s + 1) * n, 0:wide] = spread(wr)
            wst_ref[0, 0, s * n:(s + 1) * n, wide:2 * wide] = spread(wi)
        for t in range(chunk):
            pr, pi = output_map(chunk - t if reverse else t + 1)
            prdt_ref[0, 0, t * n:(t + 1) * n, 0:wide] = spread(pr)
            prdt_ref[0, 0, t * n:(t + 1) * n, wide:2 * wide] = spread(-pi)
        for s in range(chunk):
            for t in range(chunk):
                lag = s - t if reverse else t - s
                tpd_ref[0, 0, s * n:(s + 1) * n, t * n:(t + 1) * n] = (
                    lag_blocks[lag] if lag >= 0 else zero_block)

    @pl.when(pl.program_id(0) == 0)
    def _():
        emit(False)

    @pl.when(pl.program_id(0) == 1)
    def _():
        emit(True)

    er, ei = powers[chunk]
    sr, si = jnp.ones((n, n), F32), jnp.zeros((n, n), F32)
    pr_, pi_ = er, ei
    e = seg_chunks
    while e:
        if e & 1:
            sr, si = sr * pr_ - si * pi_, sr * pi_ + si * pr_
        pr_, pi_ = pr_ * pr_ - pi_ * pi_, 2.0 * pr_ * pi_
        e >>= 1
    for r, v in enumerate((er, ei, sr, si)):
        for j in range(ng // 2):
            g0 = 2 * j * S5_GROUP
            g1 = g0 + S5_GROUP
            avec_ref[0, 0, r:r + 1, j * n:(j + 1) * n] = jnp.where(
                first_half[0:1], v[g0:g0 + 1], v[g1:g1 + 1])
    avec_ref[0, 0, 4:8, :] = jnp.zeros((4, wide), F32)


def _s5_prep(lam_re, lam_im, log_step, b_re, b_im, c_re, c_im, chunk, seg_chunks):
    nd, g, p = lam_re.shape
    gs = S5_GROUP
    assert p == S5_STATE and 2 * p == LANES and g % S5_LANE_GROUPS == 0
    n_blocks = g // S5_LANE_GROUPS
    tk = chunk * LANES
    wide = S5_LANE_GROUPS * p

    def rows(v):
        v = v.reshape(nd, g * gs, p)
        return jnp.concatenate([v, v], axis=-1)

    def per_group(v):
        return rows(jnp.broadcast_to(v[:, :, None, :], (nd, g, gs, p)))

    args = (per_group(lam_re), per_group(lam_im),
            per_group(jnp.broadcast_to(log_step[:, :, None], (nd, g, p))),
            rows(jnp.swapaxes(b_re, 2, 3)), rows(jnp.swapaxes(b_im, 2, 3)), rows(c_re), rows(c_im))
    sq = pl.BlockSpec((1, LANES, LANES), lambda d, i: (d, i, 0))
    big = pl.BlockSpec((1, 1, tk, tk), lambda d, i: (d, i, 0, 0))
    assert tk == 2 * wide
    return pl.pallas_call(
        functools.partial(_s5_prep_kernel, chunk=chunk, seg_chunks=seg_chunks),
        grid=(nd, n_blocks),
        in_specs=[sq] * 7,
        out_specs=[big, big, big, pl.BlockSpec((1, 1, 8, wide), lambda d, i: (d, i, 0, 0))],
        out_shape=[jax.ShapeDtypeStruct((nd, n_blocks, tk, tk), BF16)] * 3
                  + [jax.ShapeDtypeStruct((nd, n_blocks, 8, wide), F32)],
        compiler_params=_params("parallel", "parallel"),
        name="s5_prep",
    )(*args)


def _s5_scan_kernel(u_ref, dsk_ref, tpd_ref, wst_ref, prdt_ref, avec_ref, y_ref,
                    xcat, sv, *, chunk, n_ctx):
    n = LANES
    n_c = xcat.shape[0]
    nb = sv.shape[0] // 2
    n_seg = SUBLANES
    seg = n_c // n_seg
    n_lat = n_c - n_ctx
    direction = pl.program_id(2)

    @pl.when(direction == 0)
    def _():
        for s in range(chunk):
            xcat[:, s * n:(s + 1) * n] = u_ref[0, pl.ds(s, n_c, stride=chunk), :].astype(BF16)

    def run(reverse):
        wst = wst_ref[0, 0]

        def summarise(dst, src):
            s = _dot(xcat[src, :], wst)
            for j in range(2 * nb):
                sv[j, dst, :] = s[:, j * n:(j + 1) * n]

        if reverse:
            summarise(slice(0, n_lat), slice(n_ctx, n_c))
            summarise(slice(n_lat, n_c), slice(0, n_ctx))
        else:
            summarise(slice(0, n_c), slice(0, n_c))

        blk = MXU_DIM
        n_blk = xcat.shape[1] // blk
        intra = []
        for j in range(n_blk):
            ks = slice(j * blk, n_blk * blk) if reverse else slice(0, (j + 1) * blk)
            intra.append(_dot(xcat[:, ks], tpd_ref[0, 0, ks, j * blk:(j + 1) * blk]))
        intra = jnp.concatenate(intra, axis=1)

        def lane_blocks(r):
            return [avec_ref[0, 0, r:r + 1, j * n:(j + 1) * n] for j in range(nb)]

        ar = [jnp.broadcast_to(v, (n_seg, n)) for v in lane_blocks(0)]
        ai = [jnp.broadcast_to(v, (n_seg, n)) for v in lane_blocks(1)]

        def rows_at(i):
            return pl.ds(seg - 1 - i if reverse else i, n_seg, stride=seg)

        def advance(h, rows):
            out_r, out_i = [], []
            for j in range(nb):
                hr, hi = h[j], h[nb + j]
                out_r.append(ar[j] * hr - ai[j] * hi + sv[j, rows, :])
                out_i.append(ar[j] * hi + ai[j] * hr + sv[nb + j, rows, :])
            return tuple(out_r + out_i)

        zero = jnp.zeros((n_seg, n), F32)
        fin = lax.fori_loop(0, seg, lambda i, h: advance(h, rows_at(i)), (zero,) * (2 * nb),
                            unroll=True)

        asr, asi = lane_blocks(2), lane_blocks(3)
        order = range(n_seg - 1, -1, -1) if reverse else range(n_seg)
        h0 = []
        for j in range(nb):
            rows_r, rows_i = [None] * n_seg, [None] * n_seg
            pr = pi = jnp.zeros((1, n), F32)
            prev = None
            for sgm in order:
                if prev is not None:
                    pr, pi = (asr[j] * pr - asi[j] * pi + fin[j][prev:prev + 1],
                              asr[j] * pi + asi[j] * pr + fin[nb + j][prev:prev + 1])
                rows_r[sgm], rows_i[sgm] = pr, pi
                prev = sgm
            h0.append((jnp.concatenate(rows_r, axis=0), jnp.concatenate(rows_i, axis=0)))
        h0 = tuple(v[0] for v in h0) + tuple(v[1] for v in h0)

        def step(i, h):
            rows = rows_at(i)
            nxt = advance(h, rows)
            for j in range(2 * nb):
                sv[j, rows, :] = h[j]
            return nxt

        lax.fori_loop(0, seg, step, h0, unroll=True)

        def entry_states(rows):
            return jnp.concatenate([sv[j, rows, :] for j in range(2 * nb)], axis=1).astype(BF16)

        if reverse:
            h_all = jnp.concatenate([entry_states(slice(n_lat, n_c)), entry_states(slice(0, n_lat))],
                                    axis=0)
        else:
            h_all = entry_states(slice(0, n_c))
        return intra + _dot_nt(h_all, prdt_ref[0, 0])

    @pl.when(direction == 0)
    def _():
        y = run(False)
        for t in range(chunk):
            rows = pl.ds(t, n_c, stride=chunk)
            y_ref[0, rows, :] = y[:, t * n:(t + 1) * n] + u_ref[0, rows, :] * dsk_ref[...]

    @pl.when(direction == 1)
    def _():
        y = run(True)
        for t in range(chunk):
            rows = pl.ds(t, n_c, stride=chunk)
            y_ref[0, rows, :] = y_ref[0, rows, :] + y[:, t * n:(t + 1) * n]


def _s5_scan(u, d_skip, mats, chunk, ctx_rows):
    b, s, d = u.shape
    tpd, wst, prdt, avec = mats
    nd, n_blocks, tk, _ = tpd.shape
    wide = avec.shape[-1]
    n_c = s // chunk

    def mat(shape):
        return pl.BlockSpec((1, 1) + shape, lambda o, i, dr: (dr, o, 0, 0))

    tok = pl.BlockSpec((1, s, LANES), lambda o, i, dr: (i, 0, o))
    return pl.pallas_call(
        functools.partial(_s5_scan_kernel, chunk=chunk, n_ctx=ctx_rows // chunk),
        grid=(n_blocks, b, nd),
        in_specs=[tok, pl.BlockSpec((1, LANES), lambda o, i, dr: (0, o)),
                  mat((tk, tk)), mat((tk, tk)), mat((tk, tk)), mat((8, wide))],
        out_specs=tok,
        out_shape=jax.ShapeDtypeStruct((b, s, d), F32),
        scratch_shapes=[pltpu.VMEM((n_c, tk), BF16),
                        pltpu.VMEM((2 * wide // LANES, n_c, LANES), F32)],
        compiler_params=_params("parallel", "parallel", "arbitrary"),
        name="s5_scan",
    )(u, d_skip, tpd, wst, prdt, avec)


def _window_start(r, rows):
    kh = min(WIN_H, rows)
    return min(max(r - kh // 2, 0), rows - kh)


def _na_tile_geometry(kind, rows):
    n_tiles = rows // NA_Q_ROWS
    tile = {0: 0, 1: 1, 2: n_tiles - 1}[kind]
    q0 = tile * NA_Q_ROWS
    k0 = min(max(q0 - (NA_K_ROWS - NA_Q_ROWS) // 2, 0), rows - NA_K_ROWS)
    return q0, k0


def _na_sub_window(kind, sub, rows):
    q0, k0 = _na_tile_geometry(kind, rows)
    kh = min(WIN_H, rows)
    starts = [_window_start(q0 + NA_SUB_Q_ROWS * sub + i, rows) - k0 for i in range(NA_SUB_Q_ROWS)]
    first = min(min(starts) // NA_SUB_Q_ROWS * NA_SUB_Q_ROWS, NA_K_ROWS - NA_SUB_K_ROWS)
    assert first >= 0 and max(starts) + kh <= first + NA_SUB_K_ROWS
    return first


def _na_bias_kernel(rpb_ref, o_ref, *, rows):
    h = pl.program_id(0)
    w = GRID_W
    kh = min(WIN_H, rows)
    ncol = 2 * WIN_W - 1
    nrow = 2 * WIN_H - 1
    cq = lax.broadcasted_iota(jnp.int32, (w, LANES), 0)
    lane = lax.broadcasted_iota(jnp.int32, (w, LANES), 1)
    ck = lane % w
    left = lane < w
    cs = jnp.clip(cq - WIN_W // 2, 0, w - WIN_W)
    col_ok = (ck >= cs) & (ck < cs + WIN_W)
    dc = jnp.clip(ck - cq + WIN_W - 1, 0, ncol - 1)
    neg = jnp.full((w, LANES), -jnp.inf, F32)

    def pair_table(d_left, d_right):
        t = jnp.zeros((w, LANES), F32)
        for j in range(ncol):
            vl = rpb_ref[(h * nrow + d_left) * ncol + j] if d_left is not None else 0.0
            vr = rpb_ref[(h * nrow + d_right) * ncol + j] if d_right is not None else 0.0
            t = jnp.where(dc == j, jnp.where(left, vl, vr), t)
        t = t * LOG2E
        ok = col_ok
        if d_left is None:
            ok = ok & jnp.logical_not(left)
        if d_right is None:
            ok = ok & left
        return jnp.where(ok, t, neg)

    cache = {}
    for kind in range(3):
        q0, k0 = _na_tile_geometry(kind, rows)
        for rq in range(NA_Q_ROWS):
            r = q0 + rq
            rs = _window_start(r, rows)
            first = k0 + _na_sub_window(kind, rq // NA_SUB_Q_ROWS, rows)
            for m in range(NA_SUB_K_ROWS // 2):
                ds = []
                for kr in (first + 2 * m, first + 2 * m + 1):
                    ds.append(kr - r + WIN_H - 1 if rs <= kr < rs + kh else None)
                key = tuple(ds)
                if key == (None, None):
                    blk = neg
                else:
                    if key not in cache:
                        cache[key] = pair_table(*key)
                    blk = cache[key]
                o_ref[kind, 0, rq * w:(rq + 1) * w, m * LANES:(m + 1) * LANES] = blk


def _na_bias(rpb, rows):
    nh = rpb.shape[0]
    nq, nk = NA_Q_ROWS * GRID_W, NA_SUB_K_ROWS * GRID_W
    return pl.pallas_call(
        functools.partial(_na_bias_kernel, rows=rows),
        grid=(nh,),
        in_specs=[pl.BlockSpec(memory_space=pltpu.SMEM)],
        out_specs=pl.BlockSpec((3, 1, nq, nk), lambda h: (0, h, 0, 0)),
        out_shape=jax.ShapeDtypeStruct((3, nh, nq, nk), F32),
        compiler_params=_params("parallel"),
        name="na_bias",
    )(rpb.reshape(-1))


def _na_attn_kernel(*refs, n_q, n_tiles):
    n_sub = n_tiles * n_q
    q_ref = refs[0]
    kv_refs = refs[1:1 + n_sub]
    kvc_ref = refs[1 + n_sub]
    bias_refs = refs[2 + n_sub:2 + n_sub + n_tiles]
    o_ref = refs[2 + n_sub + n_tiles]
    tb = kvc_ref.shape[2]
    half = LANES // 2

    def per_head(x, other):
        first_head = lax.broadcasted_iota(jnp.int32, x.shape, 1) < half
        fill = jnp.full_like(x, other)
        return [jnp.where(first_head, x, fill), jnp.where(first_head, fill, x)]

    kc = kvc_ref[0, 0, :, :LANES]
    vc_heads = per_head(kvc_ref[0, 0, :, LANES:], 1.0)
    units = [(sub, hh) for sub in range(n_sub) for hh in range(2)]
    keys, values, queries = {}, {}, {}
    for sub in range(n_sub):
        keys[sub] = kv_refs[sub][0, 0, :, :LANES]
        values[sub] = per_head(kv_refs[sub][0, 0, :, LANES:], 1.0)
        queries[sub] = per_head(q_ref[0, 0, sub * tb:(sub + 1) * tb, :], 0.0)

    def score(unit):
        sub, hh = unit
        qh = queries[sub][hh]
        in_tile = sub % n_q
        bias = bias_refs[sub // n_q][0, hh, in_tile * tb:(in_tile + 1) * tb, :]
        return _dot_nt(qh, keys[sub]) + bias, _dot_nt(qh, kc)

    ahead = 8
    pending = [score(u) for u in units[:ahead]]
    outs = {}
    for i, (sub, hh) in enumerate(units):
        s, sc = pending.pop(0)
        if i + ahead < len(units):
            pending.append(score(units[i + ahead]))
        m = jnp.maximum(jnp.max(s, axis=-1, keepdims=True), jnp.max(sc, axis=-1, keepdims=True))
        p = jnp.exp2(s - m).astype(BF16)
        pc = jnp.exp2(sc - m).astype(BF16)
        o = _dot(p, values[sub][hh]) + _dot(pc, vc_heads[hh])
        denom = o[:, half:half + 1] if hh == 0 else o[:, 0:1]
        outs[sub, hh] = o / denom
    first_head = lax.broadcasted_iota(jnp.int32, (tb, LANES), 1) < half
    for sub in range(n_sub):
        o_ref[0, 0, sub * tb:(sub + 1) * tb, :] = jnp.where(
            first_head, outs[sub, 0], outs[sub, 1]).astype(BF16)


def _na_attention(q, kv, bias, ctx_rows):
    b, hp, s, _ = q.shape
    seq = s - ctx_rows
    tb = TOKEN_TILE
    assert ctx_rows == tb and kv.shape[2] == s
    rows = seq // GRID_W
    n_tiles = rows // NA_Q_ROWS
    assert n_tiles >= 3
    assert NA_SUB_Q_ROWS * GRID_W == tb
    n_q = NA_Q_ROWS // NA_SUB_Q_ROWS
    n_k = NA_SUB_K_ROWS * GRID_W // tb
    slab = NA_K_ROWS * GRID_W // tb
    last_k0 = (rows - NA_K_ROWS) * GRID_W // tb

    per_step = NA_TILES_PER_STEP
    assert n_tiles % per_step == 0

    def pick(tile, by_kind):
        return jnp.where(tile == 0, by_kind[0], jnp.where(tile == n_tiles - 1, by_kind[2], by_kind[1]))

    def window_map(j, sub):
        offs = [_na_sub_window(kind, sub, rows) * GRID_W // tb for kind in range(3)]

        def f(p, bi, t):
            tile = per_step * t + j
            first = jnp.clip(n_q * tile - (slab - n_q) // 2, 0, last_k0)
            return (bi, p, (1 + first + pick(tile, offs)) * tb, 0)
        return f

    def bias_map(j):
        return lambda p, bi, t: (pick(per_step * t + j, (0, 1, 2)), p, 0, 0)

    nq = per_step * NA_Q_ROWS * GRID_W
    step_spec = pl.BlockSpec((1, 1, nq, LANES), lambda p, bi, t: (bi, p, t, 0))
    window_shape = tuple(pl.Element(n) for n in (1, 1, n_k * tb, 2 * LANES))
    kv_specs = [pl.BlockSpec(window_shape, window_map(j, sub))
                for j in range(per_step) for sub in range(n_q)]
    bias_specs = [pl.BlockSpec((1, 2) + bias.shape[2:], bias_map(j)) for j in range(per_step)]
    q_spec = pl.BlockSpec(tuple(pl.Element(n) for n in (1, 1, nq, LANES)),
                          lambda p, bi, t: (bi, p, (1 + t * (nq // tb)) * tb, 0))
    in_specs = ([q_spec] + kv_specs
                + [pl.BlockSpec((1, 1, tb, 2 * LANES), lambda p, bi, t: (bi, p, 0, 0))] + bias_specs)
    args = [q] + [kv] * (len(kv_specs) + 1) + [bias] * per_step
    return pl.pallas_call(
        functools.partial(_na_attn_kernel, n_q=n_q, n_tiles=per_step),
        grid=(hp, b, n_tiles // per_step),
        in_specs=in_specs,
        out_specs=step_spec,
        out_shape=jax.ShapeDtypeStruct((b, hp, seq, LANES), BF16),
        compiler_params=_params("parallel", "parallel", "parallel"),
        name="na_attention",
    )(*args)


def kernel(x, c, ctx, c_ctx, norm_g, ada_w, ada_b, ffn_w_in, ffn_w_out, ssm_w_in, ssm_lambda_re, ssm_lambda_im, ssm_log_step, ssm_b_re, ssm_b_im, ssm_c_re, ssm_c_im, ssm_d, ssm_w_glu, na_w_qkv, na_q_norm, na_k_norm, na_rpb, na_w_o):
    b, seq, d = x.shape
    ctx_rows = ctx.shape[1]
    depth = norm_g.shape[0]
    assert b + 1 <= 8 and ctx_rows == TOKEN_TILE and depth == 2

    cvec = jnp.zeros((8, d), F32).at[:b].set(c).at[b].set(c_ctx)
    m = _ada_modulation(cvec, b + 1, ada_w, ada_b).reshape(depth, 8, N_SUB, 3, d)

    def mod_rows(layer, sub):
        lat = m[layer, :b, sub]
        cx = jnp.broadcast_to(m[layer, b, sub], (b, 3, d))
        return jnp.concatenate([cx, lat], axis=1)

    def gain(layer, sub):
        return norm_g[layer, sub].reshape(1, d)

    def ffn(layer, which, w_in, w_out):
        sub = 2 * which
        return mod_rows(layer, sub), gain(layer, sub), w_in, w_out, ()

    def ffn_f32(layer, which):
        return [(ffn_w_in, (layer, which)), (ffn_w_out, (layer, which))]

    xs, u, w_in, w_out, w_glu = _stage(
        x, ffn(0, 0, ffn_w_in[0, 0].astype(BF16), ffn_w_out[0, 0].astype(BF16)),
        pre=("join", ctx), post=("s5", mod_rows(0, 1), gain(0, 1), ssm_w_in[0].astype(BF16)),
        convert=ffn_f32(0, 1) + [(ssm_w_glu, (0,))], ctx_rows=ctx_rows,
        tile=JOINT_STAGE_TILE, name="ffn_s5in")
    n_chunks = (ctx_rows + seq) // S5_CHUNK
    assert n_chunks % SUBLANES == 0 and ctx_rows % S5_CHUNK == 0
    mats = _s5_prep(ssm_lambda_re[0], ssm_lambda_im[0], ssm_log_step[0], ssm_b_re[0], ssm_b_im[0],
                    ssm_c_re[0], ssm_c_im[0], S5_CHUNK, n_chunks // SUBLANES)
    y = _s5_scan(u, ssm_d[0].reshape(1, d), mats, S5_CHUNK, ctx_rows)
    xs, w_in, w_out, w_qkv = _stage(
        xs, ffn(0, 1, w_in, w_out), pre=("s5", y, mod_rows(0, 1), w_glu),
        convert=ffn_f32(1, 0) + [(na_w_qkv, (0,))], ctx_rows=ctx_rows,
        tile=JOINT_STAGE_TILE, name="s5out_ffn")

    xs, q, kv, w_in, w_out, w_o = _stage(
        xs, ffn(1, 0, w_in, w_out),
        post=("qkv", mod_rows(1, 1), gain(1, 1), w_qkv, na_q_norm[0], na_k_norm[0]),
        convert=ffn_f32(1, 1) + [(na_w_o, (0,))], ctx_rows=ctx_rows,
        tile=JOINT_STAGE_TILE, name="ffn_qkv")
    bias = _na_bias(na_rpb[0], seq // GRID_W)
    attn = _na_attention(q, kv, bias, ctx_rows)
    out, = _stage(xs, ffn(1, 1, w_in, w_out), pre=("na", attn, mod_rows(1, 1), w_o),
                  ctx_rows=ctx_rows, tile=LATENT_TOKEN_TILE, name="naout_ffn")
    return out
```

```python
import functools
import math

import jax
import jax.numpy as jnp
from jax import lax
from jax.experimental import pallas as pl
from jax.experimental.pallas import tpu as pltpu

F32 = jnp.float32
BF16 = jnp.bfloat16

GRID_W = 64
N_SUB = 3
MACARON_WEIGHT = 0.5
RMS_EPS = 1e-6
S5_GROUP = 16
S5_STATE = 64
S5_MIN_NEG_RE = -1e-4
NA_HEADS = 16
WIN_H = 8
WIN_W = 16

LANES = 128
SUBLANES = 8
MXU_DIM = 256
VMEM_LIMIT = 56 * 1024 * 1024

ADA_COLUMN_TILE = 2304
TOKEN_TILE = 256
JOINT_STAGE_TILE = 384
LATENT_TOKEN_TILE = 512
WEIGHT_CAST_CHUNKS = 16
BF16_SUBLANES = 16
S5_CHUNK = 8
S5_LANE_GROUPS = LANES // S5_GROUP
NA_Q_ROWS = 8
NA_K_ROWS = 16
NA_TILES_PER_STEP = 4
NA_SUB_Q_ROWS = 4
NA_SUB_K_ROWS = 12
LOG2E = math.log2(math.e)


def _dot(a, b):
    return jnp.dot(a, b, preferred_element_type=F32)


def _dot_nt(a, b):
    return lax.dot_general(a, b, (((1,), (1,)), ((), ())), preferred_element_type=F32)


def _dot_nt_f32(a, b):
    return lax.dot_general(a, b, (((1,), (1,)), ((), ())), preferred_element_type=F32,
                           precision=lax.Precision.HIGHEST)


def _split_bf16(x):
    hi = x.astype(BF16)
    return hi, (x - hi.astype(F32)).astype(BF16)


def _params(*sem):
    return pltpu.CompilerParams(dimension_semantics=sem, vmem_limit_bytes=VMEM_LIMIT)


def _resident(shape, index_map):
    return pl.BlockSpec(shape, index_map, pipeline_mode=pl.Buffered(1))


def _ada_kernel(ct_ref, w_ref, b_ref, o_ref, *, n_rows):
    ct = ct_ref[...]
    s = ct * jax.nn.sigmoid(ct)
    w = w_ref[0]
    rows = [jnp.sum(w * s[:, r:r + 1], axis=0, keepdims=True) + b_ref[0] for r in range(n_rows)]
    rows.append(jnp.zeros((SUBLANES - n_rows, w.shape[1]), F32))
    o_ref[0] = jnp.concatenate(rows, axis=0)


def _ada_modulation(cvec, n_rows, ada_w, ada_b):
    depth, d, n = ada_w.shape
    tn = ADA_COLUMN_TILE
    assert n % tn == 0
    return pl.pallas_call(
        functools.partial(_ada_kernel, n_rows=n_rows),
        grid=(depth, n // tn),
        in_specs=[pl.BlockSpec((d, SUBLANES), lambda l, j: (0, 0)),
                  pl.BlockSpec((1, d, tn), lambda l, j: (l, 0, j)),
                  pl.BlockSpec((1, 1, tn), lambda l, j: (l, 0, j))],
        out_specs=pl.BlockSpec((1, SUBLANES, tn), lambda l, j: (l, 0, j)),
        out_shape=jax.ShapeDtypeStruct((depth, SUBLANES, n), F32),
        compiler_params=_params("parallel", "parallel"),
        name="ada_modulation",
    )(cvec.T, ada_w, ada_b.reshape(depth, 1, n))


def _row_is_context(tm, first_row, ctx_rows):
    return (first_row + lax.broadcasted_iota(jnp.int32, (tm, 1), 0)) < ctx_rows


def _modulated_norm(x, g, mod, first_row, ctx_rows):
    tm = x.shape[0]
    y = x * lax.rsqrt(jnp.mean(x * x, axis=-1, keepdims=True) + RMS_EPS) * g
    if ctx_rows == 0:
        shift, scale, gate = mod[3:4], mod[4:5], mod[5:6]
    else:
        is_ctx = _row_is_context(tm, first_row, ctx_rows)
        shift = jnp.where(is_ctx, mod[0:1], mod[3:4])
        scale = jnp.where(is_ctx, mod[1:2], mod[4:5])
        gate = jnp.where(is_ctx, mod[2:3], mod[5:6])
    return y * (1.0 + scale) + shift, gate


def _gelu_tanh(y):
    return 0.5 * y * (1.0 + jnp.tanh(math.sqrt(2.0 / math.pi) * (y + 0.044715 * (y * y * y))))


def _lane_group_transpose(xs):
    n = len(xs)
    assert n * S5_GROUP == LANES
    group = lax.broadcasted_iota(jnp.int32, xs[0].shape, 1) // S5_GROUP
    rotated = []
    for k in range(n):
        v = xs[k]
        for g in range(1, n):
            v = jnp.where(group == g, xs[(g + k) % n], v)
        rotated.append(v if k == 0 else pltpu.roll(v, S5_GROUP * k, 1))
    out = []
    for j in range(n):
        b = rotated[(-j) % n]
        for i in range(1, n):
            b = jnp.where(group == i, rotated[(i - j) % n], b)
        out.append(b)
    return out


def _stage_kernel(*refs, pre, post, n_convert, ctx_rows, hidden, head_dim):
    it = iter(refs)
    x_ref = next(it)
    tm, d = x_ref.shape[1], x_ref.shape[2]
    first_row = pl.program_id(1) * tm
    x = x_ref[0]
    if pre == "join":
        head = next(it)[0]
        if ctx_rows < tm:
            head = jnp.concatenate([head, x[:tm - ctx_rows]], axis=0)
        x = jnp.where(pl.program_id(1) == 0, head, x)

    def gate_of(mod):
        if ctx_rows == 0:
            return mod[5:6]
        return jnp.where(_row_is_context(tm, first_row, ctx_rows), mod[2:3], mod[5:6])

    n_blocks = d // LANES
    chunk = LANES // S5_GROUP
    by_block = refs[-1]
    if pre == "s5":
        yg_ref, u_ref, dsk_ref, pmod_ref, wglu_ref = (next(it) for _ in range(5))
        z = None
        for c0 in range(0, d, MXU_DIM):
            cols = slice(c0, c0 + MXU_DIM)
            for o in range(c0 // LANES, (c0 + MXU_DIM) // LANES):
                per_group = [yg_ref[0, :, (o * chunk + g) * LANES:(o * chunk + g + 1) * LANES]
                             for g in range(chunk)]
                for t, y_t in enumerate(_lane_group_transpose(per_group)):
                    by_block[o, pl.ds(t, tm // chunk, stride=chunk), :] = y_t
            y = jnp.concatenate([by_block[o] for o in range(c0 // LANES, (c0 + MXU_DIM) // LANES)],
                                axis=1)
            y = y + u_ref[0, :, cols] * dsk_ref[:, cols]
            part = _dot(_gelu_tanh(y).astype(BF16), wglu_ref[cols, :])
            z = part if z is None else z + part
        x = x + gate_of(pmod_ref[0]) * (z[:, :d] * jax.nn.sigmoid(z[:, d:]))
    elif pre == "na":
        a_ref, pmod_ref, wo_ref = next(it), next(it), next(it)
        a = jnp.concatenate([a_ref[0, p] for p in range(a_ref.shape[1])], axis=-1)
        x = x + gate_of(pmod_ref[0]) * _dot(a, wo_ref[...])

    mod_ref, g_ref, win_ref, wout_ref = next(it), next(it), next(it), next(it)
    h, gate = _modulated_norm(x, g_ref[...], mod_ref[0], first_row, ctx_rows)
    hb = h.astype(BF16)
    gt = _dot(hb, win_ref[:, :hidden])
    up = _dot(hb, win_ref[:, hidden:])
    act = (gt * jax.nn.sigmoid(gt) * up).astype(BF16)
    x = x + (MACARON_WEIGHT * gate) * _dot(act, wout_ref[...])

    if post == "s5":
        qmod_ref, qg_ref, w_ref = next(it), next(it), next(it)
    elif post == "qkv":
        (qmod_ref, qg_ref, w_ref, reduce_ref, expand_ref,
         qgain_ref, kgain_ref) = (next(it) for _ in range(7))

    cast_in = [next(it) for _ in range(n_convert)]
    n_scratch = 1 if "s5" in (pre, post) else 0
    cast_out = refs[len(refs) - n_scratch - n_convert:len(refs) - n_scratch]

    @pl.when(pl.program_id(0) * pl.num_programs(1) + pl.program_id(1) < WEIGHT_CAST_CHUNKS)
    def _():
        for src, dst in zip(cast_in, cast_out):
            dst[...] = src[...].astype(BF16)

    o_ref = next(it)
    o_ref[0] = x
    if post is None:
        return
    h2, _ = _modulated_norm(x, qg_ref[...], qmod_ref[0], first_row, ctx_rows)
    if post == "s5":
        u_out, ug_out = next(it), next(it)
        h2b = h2.astype(BF16)
        for c0 in range(0, d, MXU_DIM):
            proj = _dot(h2b, w_ref[:, c0:c0 + MXU_DIM])
            u_out[0, :, c0:c0 + MXU_DIM] = proj
            for i, o in enumerate(range(c0 // LANES, (c0 + MXU_DIM) // LANES)):
                by_block[o] = proj[:, i * LANES:(i + 1) * LANES]
                per_token = [by_block[o, pl.ds(s, tm // chunk, stride=chunk), :]
                             for s in range(chunk)]
                for g, u_g in enumerate(_lane_group_transpose(per_token)):
                    lanes = slice((o * chunk + g) * LANES, (o * chunk + g + 1) * LANES)
                    ug_out[0, :, lanes] = u_g.astype(BF16)
        return
    proj = _dot(h2.astype(BF16), w_ref[...])

    def head_norm(z, gain):
        ms = _dot((z * z).astype(BF16), reduce_ref[...]) * (1.0 / head_dim)
        hi, lo = _split_bf16(lax.rsqrt(ms + RMS_EPS))
        return z * _dot(jnp.concatenate([hi, lo], axis=1), expand_ref[...]) * gain

    q = head_norm(proj[:, :d], qgain_ref[...]) * (head_dim ** -0.5 * LOG2E)
    k = head_norm(proj[:, d:2 * d], kgain_ref[...])
    v = proj[:, 2 * d:]
    q_ref, kv_ref = next(it), next(it)
    for hp in range(d // LANES):
        sl = slice(hp * LANES, (hp + 1) * LANES)
        q_ref[0, hp] = q[:, sl].astype(BF16)
        kv_ref[0, hp, :, :LANES] = k[:, sl].astype(BF16)
        kv_ref[0, hp, :, LANES:] = v[:, sl].astype(BF16)


def _stage(xs, ffn, *, pre=None, post=None, convert=(), ctx_rows, tile, name):
    b, s, d = xs.shape
    tm = tile
    mod, gain, w_in, w_out, which = ffn
    hidden = w_out.shape[-2]
    head_dim = d // NA_HEADS
    tok = pl.BlockSpec((1, tm, d), lambda i, t: (i, t, 0))
    modspec = pl.BlockSpec((1, 6, d), lambda i, t: (i, 0, 0))
    vec = pl.BlockSpec((1, d), lambda i, t: (0, 0))

    s5_chunk = LANES // S5_GROUP
    uses_s5 = "s5" in (None if pre is None else pre[0], None if post is None else post[0])
    assert not uses_s5 or tm % (s5_chunk * BF16_SUBLANES) == 0

    def chunk_rows(a):
        return pl.BlockSpec((1, tm // s5_chunk, a.shape[2]), lambda i, t: (i, t, 0))

    def weight(w, lead=()):
        return _resident((None,) * len(lead) + w.shape[len(lead):], lambda i, t: lead + (0, 0))

    def rows_from(offset):
        step = math.gcd(tm, ctx_rows)
        n_tile, n_ctx = tm // step, ctx_rows // step
        return pl.BlockSpec(tuple(pl.Element(n) for n in (1, tm, d)),
                            lambda i, t: (i, offset(t, n_tile, n_ctx) * step, 0))

    args = [xs]
    if pre is not None and pre[0] == "join":
        assert tm >= ctx_rows
        s += ctx_rows
        args.append(pre[1])
        in_specs = [rows_from(lambda t, n, c: jnp.maximum(t * n - c, 0)),
                    pl.BlockSpec((1, ctx_rows, d), lambda i, t: (i, 0, 0))]
    elif pre is not None and pre[0] == "na":
        in_specs = [rows_from(lambda t, n, c: t * n + c)]
        s, ctx_rows = s - ctx_rows, 0
    else:
        in_specs = [tok]
    if pre is not None and pre[0] == "join":
        pass
    elif pre is not None and pre[0] == "s5":
        _, yg, u, d_skip, pmod, w_glu = pre
        args += [yg, u, d_skip, pmod, w_glu]
        in_specs += [chunk_rows(yg), tok, vec, modspec, weight(w_glu)]
    elif pre is not None:
        _, attn, pmod, w_o = pre
        args += [attn, pmod, w_o]
        in_specs += [pl.BlockSpec((1, attn.shape[1], tm, LANES), lambda i, t: (i, 0, t, 0)),
                     modspec, weight(w_o)]
    args += [mod, gain, w_in, w_out]
    in_specs += [modspec, vec, weight(w_in, which), weight(w_out, which)]
    out_shape = [jax.ShapeDtypeStruct((b, s, d), F32)]
    out_specs = [tok]
    if post is not None and post[0] == "s5":
        _, qmod, qgain, w = post
        args += [qmod, qgain, w]
        in_specs += [modspec, vec, weight(w)]
        ug = jax.ShapeDtypeStruct((b, s // s5_chunk, s5_chunk * d), BF16)
        out_shape += [jax.ShapeDtypeStruct((b, s, d), F32), ug]
        out_specs += [tok, chunk_rows(ug)]
    elif post is not None:
        _, qmod, qgain, w, q_gain, k_gain = post
        hid = jnp.arange(d) // head_dim
        to_head = (hid[:, None] == jnp.arange(LANES)[None, :]).astype(BF16)
        from_head = jnp.concatenate([to_head.T, to_head.T], axis=0)
        args += [qmod, qgain, w, to_head, from_head,
                 jnp.tile(q_gain, NA_HEADS).reshape(1, d), jnp.tile(k_gain, NA_HEADS).reshape(1, d)]
        in_specs += [modspec, vec, weight(w), weight(to_head), weight(from_head), vec, vec]
        hp = d // LANES
        out_shape += [jax.ShapeDtypeStruct((b, hp, s, LANES), BF16),
                      jax.ShapeDtypeStruct((b, hp, s, 2 * LANES), BF16)]
        out_specs += [pl.BlockSpec((1, hp, tm, LANES), lambda i, t: (i, 0, t, 0)),
                      pl.BlockSpec((1, hp, tm, 2 * LANES), lambda i, t: (i, 0, t, 0))]
    n_t = s // tm
    for w, lead in convert:
        rows, cols = w.shape[-2:]
        chunk = rows // WEIGHT_CAST_CHUNKS
        assert chunk * WEIGHT_CAST_CHUNKS == rows and chunk % BF16_SUBLANES == 0
        assert b * n_t >= WEIGHT_CAST_CHUNKS

        def chunk_of(i, t):
            return jnp.minimum(i * n_t + t, WEIGHT_CAST_CHUNKS - 1)

        args.append(w)
        in_specs.append(pl.BlockSpec((None,) * len(lead) + (chunk, cols),
                                     lambda i, t, lead=lead: lead + (chunk_of(i, t), 0)))
        out_shape.append(jax.ShapeDtypeStruct((rows, cols), BF16))
        out_specs.append(pl.BlockSpec((chunk, cols), lambda i, t: (chunk_of(i, t), 0)))
    return pl.pallas_call(
        functools.partial(_stage_kernel, pre=None if pre is None else pre[0],
                          post=None if post is None else post[0], n_convert=len(convert),
                          ctx_rows=ctx_rows, hidden=hidden, head_dim=head_dim),
        grid=(b, n_t),
        in_specs=in_specs,
        out_specs=out_specs,
        out_shape=out_shape,
        scratch_shapes=[pltpu.VMEM((d // LANES, tm, LANES), F32)] if uses_s5 else [],
        compiler_params=_params("arbitrary", "arbitrary"),
        name=name,
    )(*args)


def _s5_prep_kernel(lr_ref, li_ref, ls_ref, btr_ref, bti_ref, cr_ref, ci_ref,
                    tpd_ref, wst_ref, prdt_ref, avec_ref, qs, *, chunk, seg_chunks):
    n = LANES
    ng = S5_LANE_GROUPS
    half = n // 2
    wide = ng * half
    lr = jnp.minimum(lr_ref[0], S5_MIN_NEG_RE)
    li = li_ref[0]
    dt = jnp.exp(ls_ref[0])
    mag = jnp.exp(lr * dt)
    ar = mag * jnp.cos(li * dt)
    ai = mag * jnp.sin(li * dt)
    den = lr * lr + li * li
    zr = ((ar - 1.0) * lr + ai * li) / den
    zi = (ai * lr - (ar - 1.0) * li) / den
    btr, bti = btr_ref[0], bti_ref[0]
    bbr = zr * btr - zi * bti
    bbi = zr * bti + zi * btr
    cr, ci = cr_ref[0], ci_ref[0]

    first_half = lax.broadcasted_iota(jnp.int32, (n, n), 1) < half

    powers = [(jnp.ones((n, n), F32), jnp.zeros((n, n), F32))]
    for _ in range(chunk):
        er, ei = powers[-1]
        powers.append((er * ar - ei * ai, er * ai + ei * ar))

    def input_map(k):
        er, ei = powers[k]
        return bbr * er - bbi * ei, bbr * ei + bbi * er

    def output_map(k):
        er, ei = powers[k]
        return cr * er - ci * ei, cr * ei + ci * er

    gs = S5_GROUP
    lane16 = lax.broadcasted_iota(jnp.int32, (gs, n), 1)
    zero16 = jnp.zeros((gs, n), BF16)

    def emit(reverse):
        for k in range(chunk):
            pr, pi = output_map(k)
            qk = jnp.where(first_half, pr, -pi)
            j = chunk - 1 - k if reverse else k
            for g in range(ng):
                qs[g * n + j * gs:g * n + (j + 1) * gs, :] = qk[g * gs:(g + 1) * gs]
        kt_all = _dot_nt_f32(jnp.where(first_half, bbr, bbi), qs[...])
        for s in range(chunk):
            wr, wi = input_map(s if reverse else chunk - 1 - s)
            pr, pi = output_map(chunk - s if reverse else s + 1)
            for g in range(ng):
                pair, g_in = divmod(g, 2)
                own = (lane16 // half) == g_in
                grp = slice(g * gs, (g + 1) * gs)
                rows = slice(g_in * n + s * gs, g_in * n + (s + 1) * gs)
                wst_ref[0, 0, pair, rows, 0:n] = jnp.where(own, wr[grp], 0.0).astype(BF16)
                wst_ref[0, 0, pair, rows, n:2 * n] = jnp.where(own, wi[grp], 0.0).astype(BF16)
                prdt_ref[0, 0, pair, rows, 0:n] = jnp.where(own, pr[grp], 0.0).astype(BF16)
                prdt_ref[0, 0, pair, rows, n:2 * n] = jnp.where(own, -pi[grp], 0.0).astype(BF16)
                kt = kt_all[grp, g * n:(g + 1) * n]
                if reverse:
                    shift = (chunk - 1 - s) * gs
                    blk = jnp.where(lane16 < (s + 1) * gs, pltpu.roll(kt, (n - shift) % n, 1), 0.0)
                else:
                    blk = jnp.where(lane16 >= s * gs, pltpu.roll(kt, s * gs, 1), 0.0)
                tpd_ref[0, 0, pair, rows, g_in * n:(g_in + 1) * n] = blk.astype(BF16)
                tpd_ref[0, 0, pair, rows, (1 - g_in) * n:(2 - g_in) * n] = zero16

    @pl.when(pl.program_id(0) == 0)
    def _():
        emit(False)

    @pl.when(pl.program_id(0) == 1)
    def _():
        emit(True)

    er, ei = powers[chunk]
    sr, si = jnp.ones((n, n), F32), jnp.zeros((n, n), F32)
    pr_, pi_ = er, ei
    e = seg_chunks
    while e:
        if e & 1:
            sr, si = sr * pr_ - si * pi_, sr * pi_ + si * pr_
        pr_, pi_ = pr_ * pr_ - pi_ * pi_, 2.0 * pr_ * pi_
        e >>= 1
    for r, v in enumerate((er, ei, sr, si)):
        for j in range(ng // 2):
            g0 = 2 * j * S5_GROUP
            g1 = g0 + S5_GROUP
            avec_ref[0, 0, r:r + 1, j * n:(j + 1) * n] = jnp.where(
                first_half[0:1], v[g0:g0 + 1], v[g1:g1 + 1])
    avec_ref[0, 0, 4:8, :] = jnp.zeros((4, wide), F32)


def _s5_prep(lam_re, lam_im, log_step, b_re, b_im, c_re, c_im, chunk, seg_chunks):
    nd, g, p = lam_re.shape
    gs = S5_GROUP
    assert p == S5_STATE and 2 * p == LANES and g % S5_LANE_GROUPS == 0
    n_blocks = g // S5_LANE_GROUPS
    tk = chunk * LANES
    wide = S5_LANE_GROUPS * p

    def rows(v):
        v = v.reshape(nd, g * gs, p)
        return jnp.concatenate([v, v], axis=-1)

    def per_group(v):
        return rows(jnp.broadcast_to(v[:, :, None, :], (nd, g, gs, p)))

    args = (per_group(lam_re), per_group(lam_im),
            per_group(jnp.broadcast_to(log_step[:, :, None], (nd, g, p))),
            rows(jnp.swapaxes(b_re, 2, 3)), rows(jnp.swapaxes(b_im, 2, 3)), rows(c_re), rows(c_im))
    sq = pl.BlockSpec((1, LANES, LANES), lambda d, i: (d, i, 0))
    n_pairs = S5_LANE_GROUPS // 2
    pw = 2 * LANES
    assert chunk * gs == LANES
    mats = pl.BlockSpec((1, 1, n_pairs, pw, pw), lambda d, i: (d, i, 0, 0, 0))
    return pl.pallas_call(
        functools.partial(_s5_prep_kernel, chunk=chunk, seg_chunks=seg_chunks),
        grid=(nd, n_blocks),
        in_specs=[sq] * 7,
        out_specs=[mats, mats, mats, pl.BlockSpec((1, 1, 8, wide), lambda d, i: (d, i, 0, 0))],
        out_shape=[jax.ShapeDtypeStruct((nd, n_blocks, n_pairs, pw, pw), BF16)] * 3
                  + [jax.ShapeDtypeStruct((nd, n_blocks, 8, wide), F32)],
        scratch_shapes=[pltpu.VMEM((S5_LANE_GROUPS * LANES, LANES), F32)],
        compiler_params=_params("parallel", "parallel"),
        name="s5_prep",
    )(*args)


def _s5_scan_kernel(ug_ref, tpd_ref, wst_ref, prdt_ref, avec_ref, yg_ref, sv, *, n_ctx):
    n = LANES
    pw = 2 * n
    n_c = ug_ref.shape[1]
    nb = sv.shape[0] // 2
    n_seg = SUBLANES
    seg = n_c // n_seg
    n_lat = n_c - n_ctx
    direction = pl.program_id(2)

    def run(reverse):
        def summarise(dst, src):
            for p in range(nb):
                s = _dot(ug_ref[0, src, p * pw:(p + 1) * pw], wst_ref[0, 0, p])
                sv[p, dst, :] = s[:, :n]
                sv[nb + p, dst, :] = s[:, n:]

        if reverse:
            summarise(slice(0, n_lat), slice(n_ctx, n_c))
            summarise(slice(n_lat, n_c), slice(0, n_ctx))
        else:
            summarise(slice(0, n_c), slice(0, n_c))

        intra = jnp.concatenate([_dot(ug_ref[0, :, p * pw:(p + 1) * pw], tpd_ref[0, 0, p])
                                 for p in range(nb)], axis=1)

        def lane_blocks(r):
            return [avec_ref[0, 0, r:r + 1, j * n:(j + 1) * n] for j in range(nb)]

        ar = [jnp.broadcast_to(v, (n_seg, n)) for v in lane_blocks(0)]
        ai = [jnp.broadcast_to(v, (n_seg, n)) for v in lane_blocks(1)]

        def rows_at(i):
            return pl.ds(seg - 1 - i if reverse else i, n_seg, stride=seg)

        def advance(h, rows):
            out_r, out_i = [], []
            for j in range(nb):
                hr, hi = h[j], h[nb + j]
                out_r.append(ar[j] * hr - ai[j] * hi + sv[j, rows, :])
                out_i.append(ar[j] * hi + ai[j] * hr + sv[nb + j, rows, :])
            return tuple(out_r + out_i)

        zero = jnp.zeros((n_seg, n), F32)
        fin = lax.fori_loop(0, seg, lambda i, h: advance(h, rows_at(i)), (zero,) * (2 * nb),
                            unroll=True)

        asr, asi = lane_blocks(2), lane_blocks(3)
        order = range(n_seg - 1, -1, -1) if reverse else range(n_seg)
        h0 = []
        for j in range(nb):
            rows_r, rows_i = [None] * n_seg, [None] * n_seg
            pr = pi = jnp.zeros((1, n), F32)
            prev = None
            for sgm in order:
                if prev is not None:
                    pr, pi = (asr[j] * pr - asi[j] * pi + fin[j][prev:prev + 1],
                              asr[j] * pi + asi[j] * pr + fin[nb + j][prev:prev + 1])
                rows_r[sgm], rows_i[sgm] = pr, pi
                prev = sgm
            h0.append((jnp.concatenate(rows_r, axis=0), jnp.concatenate(rows_i, axis=0)))
        h0 = tuple(v[0] for v in h0) + tuple(v[1] for v in h0)

        def step(i, h):
            rows = rows_at(i)
            nxt = advance(h, rows)
            for j in range(2 * nb):
                sv[j, rows, :] = h[j]
            return nxt

        lax.fori_loop(0, seg, step, h0, unroll=True)

        def entry_states(p, rows):
            return jnp.concatenate([sv[p, rows, :], sv[nb + p, rows, :]], axis=1).astype(BF16)

        from_states = []
        for p in range(nb):
            if reverse:
                h = jnp.concatenate([entry_states(p, slice(n_lat, n_c)),
                                     entry_states(p, slice(0, n_lat))], axis=0)
            else:
                h = entry_states(p, slice(0, n_c))
            from_states.append(_dot_nt(h, prdt_ref[0, 0, p]))
        return intra + jnp.concatenate(from_states, axis=1)

    @pl.when(direction == 0)
    def _():
        yg_ref[0] = run(False)

    @pl.when(direction == 1)
    def _():
        yg_ref[0] = yg_ref[0] + run(True)


def _s5_scan(ug, mats, chunk, ctx_rows):
    b, n_c, width = ug.shape
    tpd, wst, prdt, avec = mats
    nd, n_blocks, n_pairs, pw, _ = tpd.shape
    wide = avec.shape[-1]
    block = S5_LANE_GROUPS * LANES
    assert width == n_blocks * block

    def mat(*shape):
        return pl.BlockSpec((1, 1) + shape, lambda o, i, dr: (dr, o) + (0,) * len(shape))

    rows = pl.BlockSpec((1, n_c, block), lambda o, i, dr: (i, 0, o))
    return pl.pallas_call(
        functools.partial(_s5_scan_kernel, n_ctx=ctx_rows // chunk),
        grid=(n_blocks, b, nd),
        in_specs=[rows, mat(n_pairs, pw, pw), mat(n_pairs, pw, pw), mat(n_pairs, pw, pw),
                  mat(8, wide)],
        out_specs=rows,
        out_shape=jax.ShapeDtypeStruct((b, n_c, width), F32),
        scratch_shapes=[pltpu.VMEM((2 * n_pairs, n_c, LANES), F32)],
        compiler_params=_params("parallel", "parallel", "arbitrary"),
        name="s5_scan",
    )(ug, tpd, wst, prdt, avec)


def _window_start(r, rows):
    kh = min(WIN_H, rows)
    return min(max(r - kh // 2, 0), rows - kh)


def _na_tile_geometry(kind, rows):
    n_tiles = rows // NA_Q_ROWS
    tile = {0: 0, 1: 1, 2: n_tiles - 1}[kind]
    q0 = tile * NA_Q_ROWS
    k0 = min(max(q0 - (NA_K_ROWS - NA_Q_ROWS) // 2, 0), rows - NA_K_ROWS)
    return q0, k0


def _na_sub_window(kind, sub, rows):
    q0, k0 = _na_tile_geometry(kind, rows)
    kh = min(WIN_H, rows)
    starts = [_window_start(q0 + NA_SUB_Q_ROWS * sub + i, rows) - k0 for i in range(NA_SUB_Q_ROWS)]
    first = min(min(starts) // NA_SUB_Q_ROWS * NA_SUB_Q_ROWS, NA_K_ROWS - NA_SUB_K_ROWS)
    assert first >= 0 and max(starts) + kh <= first + NA_SUB_K_ROWS
    return first


def _na_bias_kernel(rpb_ref, o_ref, *, rows):
    h = pl.program_id(0)
    w = GRID_W
    kh = min(WIN_H, rows)
    ncol = 2 * WIN_W - 1
    nrow = 2 * WIN_H - 1
    cq = lax.broadcasted_iota(jnp.int32, (w, LANES), 0)
    lane = lax.broadcasted_iota(jnp.int32, (w, LANES), 1)
    ck = lane % w
    left = lane < w
    cs = jnp.clip(cq - WIN_W // 2, 0, w - WIN_W)
    col_ok = (ck >= cs) & (ck < cs + WIN_W)
    dc = jnp.clip(ck - cq + WIN_W - 1, 0, ncol - 1)
    neg = jnp.full((w, LANES), -jnp.inf, F32)

    def pair_table(d_left, d_right):
        t = jnp.zeros((w, LANES), F32)
        for j in range(ncol):
            vl = rpb_ref[(h * nrow + d_left) * ncol + j] if d_left is not None else 0.0
            vr = rpb_ref[(h * nrow + d_right) * ncol + j] if d_right is not None else 0.0
            t = jnp.where(dc == j, jnp.where(left, vl, vr), t)
        t = t * LOG2E
        ok = col_ok
        if d_left is None:
            ok = ok & jnp.logical_not(left)
        if d_right is None:
            ok = ok & left
        return jnp.where(ok, t, neg)

    cache = {}
    for kind in range(3):
        q0, k0 = _na_tile_geometry(kind, rows)
        for rq in range(NA_Q_ROWS):
            r = q0 + rq
            rs = _window_start(r, rows)
            first = k0 + _na_sub_window(kind, rq // NA_SUB_Q_ROWS, rows)
            for m in range(NA_SUB_K_ROWS // 2):
                ds = []
                for kr in (first + 2 * m, first + 2 * m + 1):
                    ds.append(kr - r + WIN_H - 1 if rs <= kr < rs + kh else None)
                key = tuple(ds)
                if key == (None, None):
                    blk = neg
                else:
                    if key not in cache:
                        cache[key] = pair_table(*key)
                    blk = cache[key]
                o_ref[kind, 0, rq * w:(rq + 1) * w, m * LANES:(m + 1) * LANES] = blk


def _na_bias(rpb, rows):
    nh = rpb.shape[0]
    nq, nk = NA_Q_ROWS * GRID_W, NA_SUB_K_ROWS * GRID_W
    return pl.pallas_call(
        functools.partial(_na_bias_kernel, rows=rows),
        grid=(nh,),
        in_specs=[pl.BlockSpec(memory_space=pltpu.SMEM)],
        out_specs=pl.BlockSpec((3, 1, nq, nk), lambda h: (0, h, 0, 0)),
        out_shape=jax.ShapeDtypeStruct((3, nh, nq, nk), F32),
        compiler_params=_params("parallel"),
        name="na_bias",
    )(rpb.reshape(-1))


def _na_attn_kernel(*refs, n_q, n_tiles):
    n_sub = n_tiles * n_q
    q_ref = refs[0]
    kv_refs = refs[1:1 + n_sub]
    kvc_ref = refs[1 + n_sub]
    bias_refs = refs[2 + n_sub:2 + n_sub + n_tiles]
    o_ref = refs[2 + n_sub + n_tiles]
    tb = kvc_ref.shape[2]
    half = LANES // 2

    def per_head(x, other):
        first_head = lax.broadcasted_iota(jnp.int32, x.shape, 1) < half
        fill = jnp.full_like(x, other)
        return [jnp.where(first_head, x, fill), jnp.where(first_head, fill, x)]

    kc = kvc_ref[0, 0, :, :LANES]
    vc_heads = per_head(kvc_ref[0, 0, :, LANES:], 1.0)
    units = [(sub, hh) for sub in range(n_sub) for hh in range(2)]
    keys, values, queries = {}, {}, {}
    for sub in range(n_sub):
        keys[sub] = kv_refs[sub][0, 0, :, :LANES]
        values[sub] = per_head(kv_refs[sub][0, 0, :, LANES:], 1.0)
        queries[sub] = per_head(q_ref[0, 0, sub * tb:(sub + 1) * tb, :], 0.0)

    def score(unit):
        sub, hh = unit
        qh = queries[sub][hh]
        in_tile = sub % n_q
        bias = bias_refs[sub // n_q][0, hh, in_tile * tb:(in_tile + 1) * tb, :]
        return _dot_nt(qh, keys[sub]) + bias, _dot_nt(qh, kc)

    ahead = 8
    pending = [score(u) for u in units[:ahead]]
    outs = {}
    for i, (sub, hh) in enumerate(units):
        s, sc = pending.pop(0)
        if i + ahead < len(units):
            pending.append(score(units[i + ahead]))
        m = jnp.maximum(jnp.max(s, axis=-1, keepdims=True), jnp.max(sc, axis=-1, keepdims=True))
        p = jnp.exp2(s - m).astype(BF16)
        pc = jnp.exp2(sc - m).astype(BF16)
        o = _dot(p, values[sub][hh]) + _dot(pc, vc_heads[hh])
        denom = o[:, half:half + 1] if hh == 0 else o[:, 0:1]
        outs[sub, hh] = o / denom
    first_head = lax.broadcasted_iota(jnp.int32, (tb, LANES), 1) < half
    for sub in range(n_sub):
        o_ref[0, 0, sub * tb:(sub + 1) * tb, :] = jnp.where(
            first_head, outs[sub, 0], outs[sub, 1]).astype(BF16)


def _na_attention(q, kv, bias, ctx_rows):
    b, hp, s, _ = q.shape
    seq = s - ctx_rows
    tb = TOKEN_TILE
    assert ctx_rows == tb and kv.shape[2] == s
    rows = seq // GRID_W
    n_tiles = rows // NA_Q_ROWS
    assert n_tiles >= 3
    assert NA_SUB_Q_ROWS * GRID_W == tb
    n_q = NA_Q_ROWS // NA_SUB_Q_ROWS
    n_k = NA_SUB_K_ROWS * GRID_W // tb
    slab = NA_K_ROWS * GRID_W // tb
    last_k0 = (rows - NA_K_ROWS) * GRID_W // tb

    per_step = NA_TILES_PER_STEP
    assert n_tiles % per_step == 0

    def pick(tile, by_kind):
        return jnp.where(tile == 0, by_kind[0], jnp.where(tile == n_tiles - 1, by_kind[2], by_kind[1]))

    def window_map(j, sub):
        offs = [_na_sub_window(kind, sub, rows) * GRID_W // tb for kind in range(3)]

        def f(p, bi, t):
            tile = per_step * t + j
            first = jnp.clip(n_q * tile - (slab - n_q) // 2, 0, last_k0)
            return (bi, p, (1 + first + pick(tile, offs)) * tb, 0)
        return f

    def bias_map(j):
        return lambda p, bi, t: (pick(per_step * t + j, (0, 1, 2)), p, 0, 0)

    nq = per_step * NA_Q_ROWS * GRID_W
    step_spec = pl.BlockSpec((1, 1, nq, LANES), lambda p, bi, t: (bi, p, t, 0))
    window_shape = tuple(pl.Element(n) for n in (1, 1, n_k * tb, 2 * LANES))
    kv_specs = [pl.BlockSpec(window_shape, window_map(j, sub))
                for j in range(per_step) for sub in range(n_q)]
    bias_specs = [pl.BlockSpec((1, 2) + bias.shape[2:], bias_map(j)) for j in range(per_step)]
    q_spec = pl.BlockSpec(tuple(pl.Element(n) for n in (1, 1, nq, LANES)),
                          lambda p, bi, t: (bi, p, (1 + t * (nq // tb)) * tb, 0))
    in_specs = ([q_spec] + kv_specs
                + [pl.BlockSpec((1, 1, tb, 2 * LANES), lambda p, bi, t: (bi, p, 0, 0))] + bias_specs)
    args = [q] + [kv] * (len(kv_specs) + 1) + [bias] * per_step
    return pl.pallas_call(
        functools.partial(_na_attn_kernel, n_q=n_q, n_tiles=per_step),
        grid=(hp, b, n_tiles // per_step),
        in_specs=in_specs,
        out_specs=step_spec,
        out_shape=jax.ShapeDtypeStruct((b, hp, seq, LANES), BF16),
        compiler_params=_params("parallel", "parallel", "parallel"),
        name="na_attention",
    )(*args)


def kernel(x, c, ctx, c_ctx, norm_g, ada_w, ada_b, ffn_w_in, ffn_w_out, ssm_w_in, ssm_lambda_re, ssm_lambda_im, ssm_log_step, ssm_b_re, ssm_b_im, ssm_c_re, ssm_c_im, ssm_d, ssm_w_glu, na_w_qkv, na_q_norm, na_k_norm, na_rpb, na_w_o):
    b, seq, d = x.shape
    ctx_rows = ctx.shape[1]
    depth = norm_g.shape[0]
    assert b + 1 <= 8 and ctx_rows == TOKEN_TILE and depth == 2

    cvec = jnp.zeros((8, d), F32).at[:b].set(c).at[b].set(c_ctx)
    m = _ada_modulation(cvec, b + 1, ada_w, ada_b).reshape(depth, 8, N_SUB, 3, d)

    def mod_rows(layer, sub):
        lat = m[layer, :b, sub]
        cx = jnp.broadcast_to(m[layer, b, sub], (b, 3, d))
        return jnp.concatenate([cx, lat], axis=1)

    def gain(layer, sub):
        return norm_g[layer, sub].reshape(1, d)

    def ffn(layer, which, w_in, w_out):
        sub = 2 * which
        return mod_rows(layer, sub), gain(layer, sub), w_in, w_out, ()

    def ffn_f32(layer, which):
        return [(ffn_w_in, (layer, which)), (ffn_w_out, (layer, which))]

    xs, u, ug, w_in, w_out, w_glu = _stage(
        x, ffn(0, 0, ffn_w_in[0, 0].astype(BF16), ffn_w_out[0, 0].astype(BF16)),
        pre=("join", ctx), post=("s5", mod_rows(0, 1), gain(0, 1), ssm_w_in[0].astype(BF16)),
        convert=ffn_f32(0, 1) + [(ssm_w_glu, (0,))], ctx_rows=ctx_rows,
        tile=JOINT_STAGE_TILE, name="ffn_s5in")
    n_chunks = (ctx_rows + seq) // S5_CHUNK
    assert n_chunks % SUBLANES == 0 and ctx_rows % S5_CHUNK == 0
    mats = _s5_prep(ssm_lambda_re[0], ssm_lambda_im[0], ssm_log_step[0], ssm_b_re[0], ssm_b_im[0],
                    ssm_c_re[0], ssm_c_im[0], S5_CHUNK, n_chunks // SUBLANES)
    yg = _s5_scan(ug, mats, S5_CHUNK, ctx_rows)
    xs, w_in, w_out, w_qkv = _stage(
        xs, ffn(0, 1, w_in, w_out),
        pre=("s5", yg, u, ssm_d[0].reshape(1, d), mod_rows(0, 1), w_glu),
        convert=ffn_f32(1, 0) + [(na_w_qkv, (0,))], ctx_rows=ctx_rows,
        tile=JOINT_STAGE_TILE, name="s5out_ffn")

    xs, q, kv, w_in, w_out, w_o = _stage(
        xs, ffn(1, 0, w_in, w_out),
        post=("qkv", mod_rows(1, 1), gain(1, 1), w_qkv, na_q_norm[0], na_k_norm[0]),
        convert=ffn_f32(1, 1) + [(na_w_o, (0,))], ctx_rows=ctx_rows,
        tile=JOINT_STAGE_TILE, name="ffn_qkv")
    bias = _na_bias(na_rpb[0], seq // GRID_W)
    attn = _na_attention(q, kv, bias, ctx_rows)
    out, = _stage(xs, ffn(1, 1, w_in, w_out), pre=("na", attn, mod_rows(1, 1), w_o),
                  ctx_rows=ctx_rows, tile=LATENT_TOKEN_TILE, name="naout_ffn")
    return out
```

```python
import functools
import math

import jax
import jax.numpy as jnp
from jax import lax
from jax.experimental import pallas as pl
from jax.experimental.pallas import tpu as pltpu

F32 = jnp.float32
BF16 = jnp.bfloat16

GRID_W = 64
N_SUB = 3
MACARON_WEIGHT = 0.5
RMS_EPS = 1e-6
S5_GROUP = 16
S5_STATE = 64
S5_MIN_NEG_RE = -1e-4
NA_HEADS = 16
WIN_H = 8
WIN_W = 16

LANES = 128
SUBLANES = 8
MXU_DIM = 256
VMEM_LIMIT = 56 * 1024 * 1024

ADA_COLUMN_TILE = 2304
TOKEN_TILE = 256
JOINT_STAGE_TILE = 384
LATENT_TOKEN_TILE = 512
WEIGHT_CAST_CHUNKS = 16
BF16_SUBLANES = 16
S5_CHUNK = 8
S5_LANE_GROUPS = LANES // S5_GROUP
NA_Q_ROWS = 8
NA_K_ROWS = 16
NA_TILES_PER_STEP = 4
NA_SUB_Q_ROWS = 4
NA_SUB_K_ROWS = 12
LOG2E = math.log2(math.e)


def _dot(a, b):
    return jnp.dot(a, b, preferred_element_type=F32)


def _dot_nt(a, b):
    return lax.dot_general(a, b, (((1,), (1,)), ((), ())), preferred_element_type=F32)


def _dot_nt_f32(a, b):
    return lax.dot_general(a, b, (((1,), (1,)), ((), ())), preferred_element_type=F32,
                           precision=lax.Precision.HIGHEST)


def _split_bf16(x):
    hi = x.astype(BF16)
    return hi, (x - hi.astype(F32)).astype(BF16)


def _params(*sem):
    return pltpu.CompilerParams(dimension_semantics=sem, vmem_limit_bytes=VMEM_LIMIT)


def _resident(shape, index_map):
    return pl.BlockSpec(shape, index_map, pipeline_mode=pl.Buffered(1))


def _ada_kernel(ct_ref, w_ref, b_ref, o_ref, *, n_rows):
    ct = ct_ref[...]
    s = ct * jax.nn.sigmoid(ct)
    w = w_ref[0]
    rows = [jnp.sum(w * s[:, r:r + 1], axis=0, keepdims=True) + b_ref[0] for r in range(n_rows)]
    rows.append(jnp.zeros((SUBLANES - n_rows, w.shape[1]), F32))
    o_ref[0] = jnp.concatenate(rows, axis=0)


def _ada_modulation(cvec, n_rows, ada_w, ada_b):
    depth, d, n = ada_w.shape
    tn = ADA_COLUMN_TILE
    assert n % tn == 0
    return pl.pallas_call(
        functools.partial(_ada_kernel, n_rows=n_rows),
        grid=(depth, n // tn),
        in_specs=[pl.BlockSpec((d, SUBLANES), lambda l, j: (0, 0)),
                  pl.BlockSpec((1, d, tn), lambda l, j: (l, 0, j)),
                  pl.BlockSpec((1, 1, tn), lambda l, j: (l, 0, j))],
        out_specs=pl.BlockSpec((1, SUBLANES, tn), lambda l, j: (l, 0, j)),
        out_shape=jax.ShapeDtypeStruct((depth, SUBLANES, n), F32),
        compiler_params=_params("parallel", "parallel"),
        name="ada_modulation",
    )(cvec.T, ada_w, ada_b.reshape(depth, 1, n))


def _row_is_context(tm, first_row, ctx_rows):
    return (first_row + lax.broadcasted_iota(jnp.int32, (tm, 1), 0)) < ctx_rows


def _modulated_norm(x, g, mod, first_row, ctx_rows):
    tm = x.shape[0]
    y = x * lax.rsqrt(jnp.mean(x * x, axis=-1, keepdims=True) + RMS_EPS) * g
    if ctx_rows == 0:
        shift, scale, gate = mod[3:4], mod[4:5], mod[5:6]
    else:
        is_ctx = _row_is_context(tm, first_row, ctx_rows)
        shift = jnp.where(is_ctx, mod[0:1], mod[3:4])
        scale = jnp.where(is_ctx, mod[1:2], mod[4:5])
        gate = jnp.where(is_ctx, mod[2:3], mod[5:6])
    return y * (1.0 + scale) + shift, gate


def _gelu_tanh(y):
    return 0.5 * y * (1.0 + jnp.tanh(math.sqrt(2.0 / math.pi) * (y + 0.044715 * (y * y * y))))


def _lane_group_transpose(xs):
    n = len(xs)
    assert n * S5_GROUP == LANES and n & (n - 1) == 0
    group = lax.broadcasted_iota(jnp.int32, xs[0].shape, 1) // S5_GROUP
    xs = list(xs)
    k = 1
    while k < n:
        bit_set = (group & k) != 0
        for i in range(n):
            if i & k:
                continue
            lo, hi = xs[i], xs[i | k]
            xs[i] = jnp.where(bit_set, pltpu.roll(hi, S5_GROUP * k, 1), lo)
            xs[i | k] = jnp.where(bit_set, hi, pltpu.roll(lo, LANES - S5_GROUP * k, 1))
        k *= 2
    return xs


def _stage_kernel(*refs, pre, post, n_convert, ctx_rows, hidden, head_dim):
    it = iter(refs)
    x_ref = next(it)
    tm, d = x_ref.shape[1], x_ref.shape[2]
    first_row = pl.program_id(1) * tm
    x = x_ref[0]
    if pre == "join":
        head = next(it)[0]
        if ctx_rows < tm:
            head = jnp.concatenate([head, x[:tm - ctx_rows]], axis=0)
        x = jnp.where(pl.program_id(1) == 0, head, x)

    def gate_of(mod):
        if ctx_rows == 0:
            return mod[5:6]
        return jnp.where(_row_is_context(tm, first_row, ctx_rows), mod[2:3], mod[5:6])

    n_blocks = d // LANES
    chunk = LANES // S5_GROUP
    by_block = refs[-1]
    if pre == "s5":
        yg_ref, u_ref, dsk_ref, pmod_ref, wglu_ref = (next(it) for _ in range(5))
        z = None
        for c0 in range(0, d, MXU_DIM):
            cols = slice(c0, c0 + MXU_DIM)
            blocks = range(c0 // LANES, (c0 + MXU_DIM) // LANES)
            for o in blocks:
                per_group = [yg_ref[0, :, (o * chunk + g) * LANES:(o * chunk + g + 1) * LANES]
                             for g in range(chunk)]
                for t, y_t in enumerate(_lane_group_transpose(per_group)):
                    by_block[o, pl.ds(t, tm // chunk, stride=chunk), :] = y_t
            y = jnp.concatenate([by_block[o] for o in blocks], axis=1)
            y = y + u_ref[0, :, cols] * dsk_ref[:, cols]
            part = _dot(_gelu_tanh(y).astype(BF16), wglu_ref[cols, :])
            z = part if z is None else z + part
        x = x + gate_of(pmod_ref[0]) * (z[:, :d] * jax.nn.sigmoid(z[:, d:]))
    elif pre == "na":
        a_ref, pmod_ref, wo_ref = next(it), next(it), next(it)
        a = jnp.concatenate([a_ref[0, p] for p in range(a_ref.shape[1])], axis=-1)
        x = x + gate_of(pmod_ref[0]) * _dot(a, wo_ref[...])

    mod_ref, g_ref, win_ref, wout_ref = next(it), next(it), next(it), next(it)
    h, gate = _modulated_norm(x, g_ref[...], mod_ref[0], first_row, ctx_rows)
    hb = h.astype(BF16)
    gt = _dot(hb, win_ref[:, :hidden])
    up = _dot(hb, win_ref[:, hidden:])
    act = (gt * jax.nn.sigmoid(gt) * up).astype(BF16)
    x = x + (MACARON_WEIGHT * gate) * _dot(act, wout_ref[...])

    if post == "s5":
        qmod_ref, qg_ref, w_ref = next(it), next(it), next(it)
    elif post == "qkv":
        (qmod_ref, qg_ref, w_ref, reduce_ref, expand_ref,
         qgain_ref, kgain_ref) = (next(it) for _ in range(7))

    cast_in = [next(it) for _ in range(n_convert)]
    n_scratch = 1 if "s5" in (pre, post) else 0
    cast_out = refs[len(refs) - n_scratch - n_convert:len(refs) - n_scratch]

    @pl.when(pl.program_id(0) * pl.num_programs(1) + pl.program_id(1) < WEIGHT_CAST_CHUNKS)
    def _():
        for src, dst in zip(cast_in, cast_out):
            dst[...] = src[...].astype(BF16)

    o_ref = next(it)
    o_ref[0] = x
    if post is None:
        return
    h2, _ = _modulated_norm(x, qg_ref[...], qmod_ref[0], first_row, ctx_rows)
    proj = _dot(h2.astype(BF16), w_ref[...])
    if post == "s5":
        u_out, ug_out = next(it), next(it)
        u_out[0] = proj
        for o in range(n_blocks):
            by_block[o] = proj[:, o * LANES:(o + 1) * LANES]
            per_token = [by_block[o, pl.ds(s, tm // chunk, stride=chunk), :] for s in range(chunk)]
            for g, u_g in enumerate(_lane_group_transpose(per_token)):
                lanes = slice((o * chunk + g) * LANES, (o * chunk + g + 1) * LANES)
                ug_out[0, :, lanes] = u_g.astype(BF16)
        return

    def head_norm(z, gain):
        ms = _dot((z * z).astype(BF16), reduce_ref[...]) * (1.0 / head_dim)
        hi, lo = _split_bf16(lax.rsqrt(ms + RMS_EPS))
        return z * _dot(jnp.concatenate([hi, lo], axis=1), expand_ref[...]) * gain

    q = head_norm(proj[:, :d], qgain_ref[...]) * (head_dim ** -0.5 * LOG2E)
    k = head_norm(proj[:, d:2 * d], kgain_ref[...])
    v = proj[:, 2 * d:]
    q_ref, kv_ref = next(it), next(it)
    for hp in range(d // LANES):
        sl = slice(hp * LANES, (hp + 1) * LANES)
        q_ref[0, hp] = q[:, sl].astype(BF16)
        kv_ref[0, hp, :, :LANES] = k[:, sl].astype(BF16)
        kv_ref[0, hp, :, LANES:] = v[:, sl].astype(BF16)


def _stage(xs, ffn, *, pre=None, post=None, convert=(), ctx_rows, tile, name):
    b, s, d = xs.shape
    tm = tile
    mod, gain, w_in, w_out, which = ffn
    hidden = w_out.shape[-2]
    head_dim = d // NA_HEADS
    tok = pl.BlockSpec((1, tm, d), lambda i, t: (i, t, 0))
    modspec = pl.BlockSpec((1, 6, d), lambda i, t: (i, 0, 0))
    vec = pl.BlockSpec((1, d), lambda i, t: (0, 0))

    s5_chunk = LANES // S5_GROUP
    uses_s5 = "s5" in (None if pre is None else pre[0], None if post is None else post[0])
    assert not uses_s5 or tm % (s5_chunk * BF16_SUBLANES) == 0

    def chunk_rows(a):
        return pl.BlockSpec((1, tm // s5_chunk, a.shape[2]), lambda i, t: (i, t, 0))

    def weight(w, lead=()):
        return _resident((None,) * len(lead) + w.shape[len(lead):], lambda i, t: lead + (0, 0))

    def rows_from(offset):
        step = math.gcd(tm, ctx_rows)
        n_tile, n_ctx = tm // step, ctx_rows // step
        return pl.BlockSpec(tuple(pl.Element(n) for n in (1, tm, d)),
                            lambda i, t: (i, offset(t, n_tile, n_ctx) * step, 0))

    args = [xs]
    if pre is not None and pre[0] == "join":
        assert tm >= ctx_rows
        s += ctx_rows
        args.append(pre[1])
        in_specs = [rows_from(lambda t, n, c: jnp.maximum(t * n - c, 0)),
                    pl.BlockSpec((1, ctx_rows, d), lambda i, t: (i, 0, 0))]
    elif pre is not None and pre[0] == "na":
        in_specs = [rows_from(lambda t, n, c: t * n + c)]
        s, ctx_rows = s - ctx_rows, 0
    else:
        in_specs = [tok]
    if pre is not None and pre[0] == "join":
        pass
    elif pre is not None and pre[0] == "s5":
        _, yg, u, d_skip, pmod, w_glu = pre
        args += [yg, u, d_skip, pmod, w_glu]
        in_specs += [chunk_rows(yg), tok, vec, modspec, weight(w_glu)]
    elif pre is not None:
        _, attn, pmod, w_o = pre
        args += [attn, pmod, w_o]
        in_specs += [pl.BlockSpec((1, attn.shape[1], tm, LANES), lambda i, t: (i, 0, t, 0)),
                     modspec, weight(w_o)]
    args += [mod, gain, w_in, w_out]
    in_specs += [modspec, vec, weight(w_in, which), weight(w_out, which)]
    out_shape = [jax.ShapeDtypeStruct((b, s, d), F32)]
    out_specs = [tok]
    if post is not None and post[0] == "s5":
        _, qmod, qgain, w = post
        args += [qmod, qgain, w]
        in_specs += [modspec, vec, weight(w)]
        ug = jax.ShapeDtypeStruct((b, s // s5_chunk, s5_chunk * d), BF16)
        out_shape += [jax.ShapeDtypeStruct((b, s, d), F32), ug]
        out_specs += [tok, chunk_rows(ug)]
    elif post is not None:
        _, qmod, qgain, w, q_gain, k_gain = post
        hid = jnp.arange(d) // head_dim
        to_head = (hid[:, None] == jnp.arange(LANES)[None, :]).astype(BF16)
        from_head = jnp.concatenate([to_head.T, to_head.T], axis=0)
        args += [qmod, qgain, w, to_head, from_head,
                 jnp.tile(q_gain, NA_HEADS).reshape(1, d), jnp.tile(k_gain, NA_HEADS).reshape(1, d)]
        in_specs += [modspec, vec, weight(w), weight(to_head), weight(from_head), vec, vec]
        hp = d // LANES
        out_shape += [jax.ShapeDtypeStruct((b, hp, s, LANES), BF16),
                      jax.ShapeDtypeStruct((b, hp, s, 2 * LANES), BF16)]
        out_specs += [pl.BlockSpec((1, hp, tm, LANES), lambda i, t: (i, 0, t, 0)),
                      pl.BlockSpec((1, hp, tm, 2 * LANES), lambda i, t: (i, 0, t, 0))]
    n_t = s // tm
    for w, lead in convert:
        rows, cols = w.shape[-2:]
        chunk = rows // WEIGHT_CAST_CHUNKS
        assert chunk * WEIGHT_CAST_CHUNKS == rows and chunk % BF16_SUBLANES == 0
        assert b * n_t >= WEIGHT_CAST_CHUNKS

        def chunk_of(i, t):
            return jnp.minimum(i * n_t + t, WEIGHT_CAST_CHUNKS - 1)

        args.append(w)
        in_specs.append(pl.BlockSpec((None,) * len(lead) + (chunk, cols),
                                     lambda i, t, lead=lead: lead + (chunk_of(i, t), 0)))
        out_shape.append(jax.ShapeDtypeStruct((rows, cols), BF16))
        out_specs.append(pl.BlockSpec((chunk, cols), lambda i, t: (chunk_of(i, t), 0)))
    return pl.pallas_call(
        functools.partial(_stage_kernel, pre=None if pre is None else pre[0],
                          post=None if post is None else post[0], n_convert=len(convert),
                          ctx_rows=ctx_rows, hidden=hidden, head_dim=head_dim),
        grid=(b, n_t),
        in_specs=in_specs,
        out_specs=out_specs,
        out_shape=out_shape,
        scratch_shapes=[pltpu.VMEM((d // LANES, tm, LANES), F32)] if uses_s5 else [],
        compiler_params=_params("arbitrary", "arbitrary"),
        name=name,
    )(*args)


def _s5_prep_kernel(lr_ref, li_ref, ls_ref, btr_ref, bti_ref, cr_ref, ci_ref,
                    tpd_ref, wst_ref, prdt_ref, avec_ref, qs, *, chunk, seg_chunks):
    n = LANES
    ng = S5_LANE_GROUPS
    half = n // 2
    wide = ng * half
    lr = jnp.minimum(lr_ref[0], S5_MIN_NEG_RE)
    li = li_ref[0]
    dt = jnp.exp(ls_ref[0])
    mag = jnp.exp(lr * dt)
    ar = mag * jnp.cos(li * dt)
    ai = mag * jnp.sin(li * dt)
    den = lr * lr + li * li
    zr = ((ar - 1.0) * lr + ai * li) / den
    zi = (ai * lr - (ar - 1.0) * li) / den
    btr, bti = btr_ref[0], bti_ref[0]
    bbr = zr * btr - zi * bti
    bbi = zr * bti + zi * btr
    cr, ci = cr_ref[0], ci_ref[0]

    first_half = lax.broadcasted_iota(jnp.int32, (n, n), 1) < half

    powers = [(jnp.ones((n, n), F32), jnp.zeros((n, n), F32))]
    for _ in range(chunk):
        er, ei = powers[-1]
        powers.append((er * ar - ei * ai, er * ai + ei * ar))

    def input_map(k):
        er, ei = powers[k]
        return bbr * er - bbi * ei, bbr * ei + bbi * er

    def output_map(k):
        er, ei = powers[k]
        return cr * er - ci * ei, cr * ei + ci * er

    gs = S5_GROUP
    lane16 = lax.broadcasted_iota(jnp.int32, (gs, n), 1)
    zero16 = jnp.zeros((gs, n), BF16)

    def emit(reverse):
        for k in range(chunk):
            pr, pi = output_map(k)
            qk = jnp.where(first_half, pr, -pi)
            j = chunk - 1 - k if reverse else k
            for g in range(ng):
                qs[g * n + j * gs:g * n + (j + 1) * gs, :] = qk[g * gs:(g + 1) * gs]
        kt_all = _dot_nt_f32(jnp.where(first_half, bbr, bbi), qs[...])
        for s in range(chunk):
            wr, wi = input_map(s if reverse else chunk - 1 - s)
            pr, pi = output_map(chunk - s if reverse else s + 1)
            for g in range(ng):
                pair, g_in = divmod(g, 2)
                own = (lane16 // half) == g_in
                grp = slice(g * gs, (g + 1) * gs)
                rows = slice(g_in * n + s * gs, g_in * n + (s + 1) * gs)
                wst_ref[0, 0, pair, rows, 0:n] = jnp.where(own, wr[grp], 0.0).astype(BF16)
                wst_ref[0, 0, pair, rows, n:2 * n] = jnp.where(own, wi[grp], 0.0).astype(BF16)
                prdt_ref[0, 0, pair, rows, 0:n] = jnp.where(own, pr[grp], 0.0).astype(BF16)
                prdt_ref[0, 0, pair, rows, n:2 * n] = jnp.where(own, -pi[grp], 0.0).astype(BF16)
                kt = kt_all[grp, g * n:(g + 1) * n]
                if reverse:
                    shift = (chunk - 1 - s) * gs
                    blk = jnp.where(lane16 < (s + 1) * gs, pltpu.roll(kt, (n - shift) % n, 1), 0.0)
                else:
                    blk = jnp.where(lane16 >= s * gs, pltpu.roll(kt, s * gs, 1), 0.0)
                tpd_ref[0, 0, pair, rows, g_in * n:(g_in + 1) * n] = blk.astype(BF16)
                tpd_ref[0, 0, pair, rows, (1 - g_in) * n:(2 - g_in) * n] = zero16

    @pl.when(pl.program_id(0) == 0)
    def _():
        emit(False)

    @pl.when(pl.program_id(0) == 1)
    def _():
        emit(True)

    er, ei = powers[chunk]
    sr, si = jnp.ones((n, n), F32), jnp.zeros((n, n), F32)
    pr_, pi_ = er, ei
    e = seg_chunks
    while e:
        if e & 1:
            sr, si = sr * pr_ - si * pi_, sr * pi_ + si * pr_
        pr_, pi_ = pr_ * pr_ - pi_ * pi_, 2.0 * pr_ * pi_
        e >>= 1
    for r, v in enumerate((er, ei, sr, si)):
        for j in range(ng // 2):
            g0 = 2 * j * S5_GROUP
            g1 = g0 + S5_GROUP
            avec_ref[0, 0, r:r + 1, j * n:(j + 1) * n] = jnp.where(
                first_half[0:1], v[g0:g0 + 1], v[g1:g1 + 1])
    avec_ref[0, 0, 4:8, :] = jnp.zeros((4, wide), F32)


def _s5_prep(lam_re, lam_im, log_step, b_re, b_im, c_re, c_im, chunk, seg_chunks):
    nd, g, p = lam_re.shape
    gs = S5_GROUP
    assert p == S5_STATE and 2 * p == LANES and g % S5_LANE_GROUPS == 0
    n_blocks = g // S5_LANE_GROUPS
    tk = chunk * LANES
    wide = S5_LANE_GROUPS * p

    def rows(v):
        v = v.reshape(nd, g * gs, p)
        return jnp.concatenate([v, v], axis=-1)

    def per_group(v):
        return rows(jnp.broadcast_to(v[:, :, None, :], (nd, g, gs, p)))

    args = (per_group(lam_re), per_group(lam_im),
            per_group(jnp.broadcast_to(log_step[:, :, None], (nd, g, p))),
            rows(jnp.swapaxes(b_re, 2, 3)), rows(jnp.swapaxes(b_im, 2, 3)), rows(c_re), rows(c_im))
    sq = pl.BlockSpec((1, LANES, LANES), lambda d, i: (d, i, 0))
    n_pairs = S5_LANE_GROUPS // 2
    pw = 2 * LANES
    assert chunk * gs == LANES
    mats = pl.BlockSpec((1, 1, n_pairs, pw, pw), lambda d, i: (d, i, 0, 0, 0))
    return pl.pallas_call(
        functools.partial(_s5_prep_kernel, chunk=chunk, seg_chunks=seg_chunks),
        grid=(nd, n_blocks),
        in_specs=[sq] * 7,
        out_specs=[mats, mats, mats, pl.BlockSpec((1, 1, 8, wide), lambda d, i: (d, i, 0, 0))],
        out_shape=[jax.ShapeDtypeStruct((nd, n_blocks, n_pairs, pw, pw), BF16)] * 3
                  + [jax.ShapeDtypeStruct((nd, n_blocks, 8, wide), F32)],
        scratch_shapes=[pltpu.VMEM((S5_LANE_GROUPS * LANES, LANES), F32)],
        compiler_params=_params("parallel", "parallel"),
        name="s5_prep",
    )(*args)


def _s5_scan_kernel(ug_ref, tpd_ref, wst_ref, prdt_ref, avec_ref, yg_ref, sv, *, n_ctx):
    n = LANES
    pw = 2 * n
    n_c = ug_ref.shape[1]
    nb = sv.shape[0] // 2
    n_seg = SUBLANES
    seg = n_c // n_seg
    n_lat = n_c - n_ctx
    direction = pl.program_id(2)

    def run(reverse):
        def summarise(dst, src):
            for p in range(nb):
                s = _dot(ug_ref[0, src, p * pw:(p + 1) * pw], wst_ref[0, 0, p])
                sv[p, dst, :] = s[:, :n]
                sv[nb + p, dst, :] = s[:, n:]

        if reverse:
            summarise(slice(0, n_lat), slice(n_ctx, n_c))
            summarise(slice(n_lat, n_c), slice(0, n_ctx))
        else:
            summarise(slice(0, n_c), slice(0, n_c))

        intra = jnp.concatenate([_dot(ug_ref[0, :, p * pw:(p + 1) * pw], tpd_ref[0, 0, p])
                                 for p in range(nb)], axis=1)

        def lane_blocks(r):
            return [avec_ref[0, 0, r:r + 1, j * n:(j + 1) * n] for j in range(nb)]

        ar = [jnp.broadcast_to(v, (n_seg, n)) for v in lane_blocks(0)]
        ai = [jnp.broadcast_to(v, (n_seg, n)) for v in lane_blocks(1)]

        def rows_at(i):
            return pl.ds(seg - 1 - i if reverse else i, n_seg, stride=seg)

        def advance(h, rows):
            out_r, out_i = [], []
            for j in range(nb):
                hr, hi = h[j], h[nb + j]
                out_r.append(ar[j] * hr - ai[j] * hi + sv[j, rows, :])
                out_i.append(ar[j] * hi + ai[j] * hr + sv[nb + j, rows, :])
            return tuple(out_r + out_i)

        zero = jnp.zeros((n_seg, n), F32)
        fin = lax.fori_loop(0, seg, lambda i, h: advance(h, rows_at(i)), (zero,) * (2 * nb),
                            unroll=True)

        asr, asi = lane_blocks(2), lane_blocks(3)
        order = range(n_seg - 1, -1, -1) if reverse else range(n_seg)
        h0 = []
        for j in range(nb):
            rows_r, rows_i = [None] * n_seg, [None] * n_seg
            pr = pi = jnp.zeros((1, n), F32)
            prev = None
            for sgm in order:
                if prev is not None:
                    pr, pi = (asr[j] * pr - asi[j] * pi + fin[j][prev:prev + 1],
                              asr[j] * pi + asi[j] * pr + fin[nb + j][prev:prev + 1])
                rows_r[sgm], rows_i[sgm] = pr, pi
                prev = sgm
            h0.append((jnp.concatenate(rows_r, axis=0), jnp.concatenate(rows_i, axis=0)))
        h0 = tuple(v[0] for v in h0) + tuple(v[1] for v in h0)

        def step(i, h):
            rows = rows_at(i)
            nxt = advance(h, rows)
            for j in range(2 * nb):
                sv[j, rows, :] = h[j]
            return nxt

        lax.fori_loop(0, seg, step, h0, unroll=True)

        def entry_states(p, rows):
            return jnp.concatenate([sv[p, rows, :], sv[nb + p, rows, :]], axis=1).astype(BF16)

        from_states = []
        for p in range(nb):
            if reverse:
                h = jnp.concatenate([entry_states(p, slice(n_lat, n_c)),
                                     entry_states(p, slice(0, n_lat))], axis=0)
            else:
                h = entry_states(p, slice(0, n_c))
            from_states.append(_dot_nt(h, prdt_ref[0, 0, p]))
        return intra + jnp.concatenate(from_states, axis=1)

    @pl.when(direction == 0)
    def _():
        yg_ref[0] = run(False)

    @pl.when(direction == 1)
    def _():
        yg_ref[0] = yg_ref[0] + run(True)


def _s5_scan(ug, mats, chunk, ctx_rows):
    b, n_c, width = ug.shape
    tpd, wst, prdt, avec = mats
    nd, n_blocks, n_pairs, pw, _ = tpd.shape
    wide = avec.shape[-1]
    block = S5_LANE_GROUPS * LANES
    assert width == n_blocks * block

    def mat(*shape):
        return pl.BlockSpec((1, 1) + shape, lambda o, i, dr: (dr, o) + (0,) * len(shape))

    rows = pl.BlockSpec((1, n_c, block), lambda o, i, dr: (i, 0, o))
    return pl.pallas_call(
        functools.partial(_s5_scan_kernel, n_ctx=ctx_rows // chunk),
        grid=(n_blocks, b, nd),
        in_specs=[rows, mat(n_pairs, pw, pw), mat(n_pairs, pw, pw), mat(n_pairs, pw, pw),
                  mat(8, wide)],
        out_specs=rows,
        out_shape=jax.ShapeDtypeStruct((b, n_c, width), F32),
        scratch_shapes=[pltpu.VMEM((2 * n_pairs, n_c, LANES), F32)],
        compiler_params=_params("parallel", "parallel", "arbitrary"),
        name="s5_scan",
    )(ug, tpd, wst, prdt, avec)


def _window_start(r, rows):
    kh = min(WIN_H, rows)
    return min(max(r - kh // 2, 0), rows - kh)


def _na_tile_geometry(kind, rows):
    n_tiles = rows // NA_Q_ROWS
    tile = {0: 0, 1: 1, 2: n_tiles - 1}[kind]
    q0 = tile * NA_Q_ROWS
    k0 = min(max(q0 - (NA_K_ROWS - NA_Q_ROWS) // 2, 0), rows - NA_K_ROWS)
    return q0, k0


def _na_sub_window(kind, sub, rows):
    q0, k0 = _na_tile_geometry(kind, rows)
    kh = min(WIN_H, rows)
    starts = [_window_start(q0 + NA_SUB_Q_ROWS * sub + i, rows) - k0 for i in range(NA_SUB_Q_ROWS)]
    first = min(min(starts) // NA_SUB_Q_ROWS * NA_SUB_Q_ROWS, NA_K_ROWS - NA_SUB_K_ROWS)
    assert first >= 0 and max(starts) + kh <= first + NA_SUB_K_ROWS
    return first


def _na_bias_kernel(rpb_ref, o_ref, *, rows):
    h = pl.program_id(0)
    w = GRID_W
    kh = min(WIN_H, rows)
    ncol = 2 * WIN_W - 1
    nrow = 2 * WIN_H - 1
    cq = lax.broadcasted_iota(jnp.int32, (w, LANES), 0)
    lane = lax.broadcasted_iota(jnp.int32, (w, LANES), 1)
    ck = lane % w
    left = lane < w
    cs = jnp.clip(cq - WIN_W // 2, 0, w - WIN_W)
    col_ok = (ck >= cs) & (ck < cs + WIN_W)
    dc = jnp.clip(ck - cq + WIN_W - 1, 0, ncol - 1)
    neg = jnp.full((w, LANES), -jnp.inf, F32)

    def pair_table(d_left, d_right):
        t = jnp.zeros((w, LANES), F32)
        for j in range(ncol):
            vl = rpb_ref[(h * nrow + d_left) * ncol + j] if d_left is not None else 0.0
            vr = rpb_ref[(h * nrow + d_right) * ncol + j] if d_right is not None else 0.0
            t = jnp.where(dc == j, jnp.where(left, vl, vr), t)
        t = t * LOG2E
        ok = col_ok
        if d_left is None:
            ok = ok & jnp.logical_not(left)
        if d_right is None:
            ok = ok & left
        return jnp.where(ok, t, neg)

    cache = {}
    for kind in range(3):
        q0, k0 = _na_tile_geometry(kind, rows)
        for rq in range(NA_Q_ROWS):
            r = q0 + rq
            rs = _window_start(r, rows)
            first = k0 + _na_sub_window(kind, rq // NA_SUB_Q_ROWS, rows)
            for m in range(NA_SUB_K_ROWS // 2):
                ds = []
                for kr in (first + 2 * m, first + 2 * m + 1):
                    ds.append(kr - r + WIN_H - 1 if rs <= kr < rs + kh else None)
                key = tuple(ds)
                if key == (None, None):
                    blk = neg
                else:
                    if key not in cache:
                        cache[key] = pair_table(*key)
                    blk = cache[key]
                o_ref[kind, 0, rq * w:(rq + 1) * w, m * LANES:(m + 1) * LANES] = blk


def _na_bias(rpb, rows):
    nh = rpb.shape[0]
    nq, nk = NA_Q_ROWS * GRID_W, NA_SUB_K_ROWS * GRID_W
    return pl.pallas_call(
        functools.partial(_na_bias_kernel, rows=rows),
        grid=(nh,),
        in_specs=[pl.BlockSpec(memory_space=pltpu.SMEM)],
        out_specs=pl.BlockSpec((3, 1, nq, nk), lambda h: (0, h, 0, 0)),
        out_shape=jax.ShapeDtypeStruct((3, nh, nq, nk), F32),
        compiler_params=_params("parallel"),
        name="na_bias",
    )(rpb.reshape(-1))


def _na_attn_kernel(*refs, n_q, n_tiles):
    n_sub = n_tiles * n_q
    q_ref = refs[0]
    kv_refs = refs[1:1 + n_sub]
    kvc_ref = refs[1 + n_sub]
    bias_refs = refs[2 + n_sub:2 + n_sub + n_tiles]
    o_ref = refs[2 + n_sub + n_tiles]
    tb = kvc_ref.shape[2]
    half = LANES // 2

    def per_head(x, other):
        first_head = lax.broadcasted_iota(jnp.int32, x.shape, 1) < half
        fill = jnp.full_like(x, other)
        return [jnp.where(first_head, x, fill), jnp.where(first_head, fill, x)]

    kc = kvc_ref[0, 0, :, :LANES]
    vc_heads = per_head(kvc_ref[0, 0, :, LANES:], 1.0)
    units = [(sub, hh) for sub in range(n_sub) for hh in range(2)]
    keys, values, queries = {}, {}, {}
    for sub in range(n_sub):
        keys[sub] = kv_refs[sub][0, 0, :, :LANES]
        values[sub] = per_head(kv_refs[sub][0, 0, :, LANES:], 1.0)
        queries[sub] = per_head(q_ref[0, 0, sub * tb:(sub + 1) * tb, :], 0.0)

    def score(unit):
        sub, hh = unit
        qh = queries[sub][hh]
        in_tile = sub % n_q
        bias = bias_refs[sub // n_q][0, hh, in_tile * tb:(in_tile + 1) * tb, :]
        return _dot_nt(qh, keys[sub]) + bias, _dot_nt(qh, kc)

    ahead = 8
    pending = [score(u) for u in units[:ahead]]
    outs = {}
    for i, (sub, hh) in enumerate(units):
        s, sc = pending.pop(0)
        if i + ahead < len(units):
            pending.append(score(units[i + ahead]))
        m = jnp.maximum(jnp.max(s, axis=-1, keepdims=True), jnp.max(sc, axis=-1, keepdims=True))
        p = jnp.exp2(s - m).astype(BF16)
        pc = jnp.exp2(sc - m).astype(BF16)
        o = _dot(p, values[sub][hh]) + _dot(pc, vc_heads[hh])
        denom = o[:, half:half + 1] if hh == 0 else o[:, 0:1]
        outs[sub, hh] = o / denom
    first_head = lax.broadcasted_iota(jnp.int32, (tb, LANES), 1) < half
    for sub in range(n_sub):
        o_ref[0, 0, sub * tb:(sub + 1) * tb, :] = jnp.where(
            first_head, outs[sub, 0], outs[sub, 1]).astype(BF16)


def _na_attention(q, kv, bias, ctx_rows):
    b, hp, s, _ = q.shape
    seq = s - ctx_rows
    tb = TOKEN_TILE
    assert ctx_rows == tb and kv.shape[2] == s
    rows = seq // GRID_W
    n_tiles = rows // NA_Q_ROWS
    assert n_tiles >= 3
    assert NA_SUB_Q_ROWS * GRID_W == tb
    n_q = NA_Q_ROWS // NA_SUB_Q_ROWS
    n_k = NA_SUB_K_ROWS * GRID_W // tb
    slab = NA_K_ROWS * GRID_W // tb
    last_k0 = (rows - NA_K_ROWS) * GRID_W // tb

    per_step = NA_TILES_PER_STEP
    assert n_tiles % per_step == 0

    def pick(tile, by_kind):
        return jnp.where(tile == 0, by_kind[0], jnp.where(tile == n_tiles - 1, by_kind[2], by_kind[1]))

    def window_map(j, sub):
        offs = [_na_sub_window(kind, sub, rows) * GRID_W // tb for kind in range(3)]

        def f(p, bi, t):
            tile = per_step * t + j
            first = jnp.clip(n_q * tile - (slab - n_q) // 2, 0, last_k0)
            return (bi, p, (1 + first + pick(tile, offs)) * tb, 0)
        return f

    def bias_map(j):
        return lambda p, bi, t: (pick(per_step * t + j, (0, 1, 2)), p, 0, 0)

    nq = per_step * NA_Q_ROWS * GRID_W
    step_spec = pl.BlockSpec((1, 1, nq, LANES), lambda p, bi, t: (bi, p, t, 0))
    window_shape = tuple(pl.Element(n) for n in (1, 1, n_k * tb, 2 * LANES))
    kv_specs = [pl.BlockSpec(window_shape, window_map(j, sub))
                for j in range(per_step) for sub in range(n_q)]
    bias_specs = [pl.BlockSpec((1, 2) + bias.shape[2:], bias_map(j)) for j in range(per_step)]
    q_spec = pl.BlockSpec(tuple(pl.Element(n) for n in (1, 1, nq, LANES)),
                          lambda p, bi, t: (bi, p, (1 + t * (nq // tb)) * tb, 0))
    in_specs = ([q_spec] + kv_specs
                + [pl.BlockSpec((1, 1, tb, 2 * LANES), lambda p, bi, t: (bi, p, 0, 0))] + bias_specs)
    args = [q] + [kv] * (len(kv_specs) + 1) + [bias] * per_step
    return pl.pallas_call(
        functools.partial(_na_attn_kernel, n_q=n_q, n_tiles=per_step),
        grid=(hp, b, n_tiles // per_step),
        in_specs=in_specs,
        out_specs=step_spec,
        out_shape=jax.ShapeDtypeStruct((b, hp, seq, LANES), BF16),
        compiler_params=_params("parallel", "parallel", "parallel"),
        name="na_attention",
    )(*args)


def kernel(x, c, ctx, c_ctx, norm_g, ada_w, ada_b, ffn_w_in, ffn_w_out, ssm_w_in, ssm_lambda_re, ssm_lambda_im, ssm_log_step, ssm_b_re, ssm_b_im, ssm_c_re, ssm_c_im, ssm_d, ssm_w_glu, na_w_qkv, na_q_norm, na_k_norm, na_rpb, na_w_o):
    b, seq, d = x.shape
    ctx_rows = ctx.shape[1]
    depth = norm_g.shape[0]
    assert b + 1 <= 8 and ctx_rows == TOKEN_TILE and depth == 2

    cvec = jnp.zeros((8, d), F32).at[:b].set(c).at[b].set(c_ctx)
    m = _ada_modulation(cvec, b + 1, ada_w, ada_b).reshape(depth, 8, N_SUB, 3, d)

    def mod_rows(layer, sub):
        lat = m[layer, :b, sub]
        cx = jnp.broadcast_to(m[layer, b, sub], (b, 3, d))
        return jnp.concatenate([cx, lat], axis=1)

    def gain(layer, sub):
        return norm_g[layer, sub].reshape(1, d)

    def ffn(layer, which, w_in, w_out):
        sub = 2 * which
        return mod_rows(layer, sub), gain(layer, sub), w_in, w_out, ()

    def ffn_f32(layer, which):
        return [(ffn_w_in, (layer, which)), (ffn_w_out, (layer, which))]

    xs, u, ug, w_in, w_out, w_glu = _stage(
        x, ffn(0, 0, ffn_w_in[0, 0].astype(BF16), ffn_w_out[0, 0].astype(BF16)),
        pre=("join", ctx), post=("s5", mod_rows(0, 1), gain(0, 1), ssm_w_in[0].astype(BF16)),
        convert=ffn_f32(0, 1) + [(ssm_w_glu, (0,))], ctx_rows=ctx_rows,
        tile=JOINT_STAGE_TILE, name="ffn_s5in")
    n_chunks = (ctx_rows + seq) // S5_CHUNK
    assert n_chunks % SUBLANES == 0 and ctx_rows % S5_CHUNK == 0
    mats = _s5_prep(ssm_lambda_re[0], ssm_lambda_im[0], ssm_log_step[0], ssm_b_re[0], ssm_b_im[0],
                    ssm_c_re[0], ssm_c_im[0], S5_CHUNK, n_chunks // SUBLANES)
    yg = _s5_scan(ug, mats, S5_CHUNK, ctx_rows)
    xs, w_in, w_out, w_qkv = _stage(
        xs, ffn(0, 1, w_in, w_out),
        pre=("s5", yg, u, ssm_d[0].reshape(1, d), mod_rows(0, 1), w_glu),
        convert=ffn_f32(1, 0) + [(na_w_qkv, (0,))], ctx_rows=ctx_rows,
        tile=JOINT_STAGE_TILE, name="s5out_ffn")

    xs, q, kv, w_in, w_out, w_o = _stage(
        xs, ffn(1, 0, w_in, w_out),
        post=("qkv", mod_rows(1, 1), gain(1, 1), w_qkv, na_q_norm[0], na_k_norm[0]),
        convert=ffn_f32(1, 1) + [(na_w_o, (0,))], ctx_rows=ctx_rows,
        tile=JOINT_STAGE_TILE, name="ffn_qkv")
    bias = _na_bias(na_rpb[0], seq // GRID_W)
    attn = _na_attention(q, kv, bias, ctx_rows)
    out, = _stage(xs, ffn(1, 1, w_in, w_out), pre=("na", attn, mod_rows(1, 1), w_o),
                  ctx_rows=ctx_rows, tile=LATENT_TOKEN_TILE, name="naout_ffn")
    return out
```

```python
import functools
import math

import jax
import jax.numpy as jnp
from jax import lax
from jax.experimental import pallas as pl
from jax.experimental.pallas import tpu as pltpu

F32 = jnp.float32
BF16 = jnp.bfloat16

GRID_W = 64
N_SUB = 3
MACARON_WEIGHT = 0.5
RMS_EPS = 1e-6
S5_GROUP = 16
S5_STATE = 64
S5_MIN_NEG_RE = -1e-4
NA_HEADS = 16
WIN_H = 8
WIN_W = 16

LANES = 128
SUBLANES = 8
MXU_DIM = 256
VMEM_LIMIT = 56 * 1024 * 1024

ADA_COLUMN_TILE = 2304
TOKEN_TILE = 256
JOINT_STAGE_TILE = 384
LATENT_TOKEN_TILE = 512
WEIGHT_CAST_CHUNKS = 16
BF16_SUBLANES = 16
S5_CHUNK = 8
S5_LANE_GROUPS = LANES // S5_GROUP
NA_Q_ROWS = 8
NA_K_ROWS = 16
NA_TILES_PER_STEP = 4
NA_SUB_Q_ROWS = 4
NA_SUB_K_ROWS = 12
LOG2E = math.log2(math.e)


def _dot(a, b):
    return jnp.dot(a, b, preferred_element_type=F32)


def _dot_nt(a, b):
    return lax.dot_general(a, b, (((1,), (1,)), ((), ())), preferred_element_type=F32)


def _dot_nt_f32(a, b):
    return lax.dot_general(a, b, (((1,), (1,)), ((), ())), preferred_element_type=F32,
                           precision=lax.Precision.HIGHEST)


def _split_bf16(x):
    hi = x.astype(BF16)
    return hi, (x - hi.astype(F32)).astype(BF16)


def _params(*sem):
    return pltpu.CompilerParams(dimension_semantics=sem, vmem_limit_bytes=VMEM_LIMIT)


def _resident(shape, index_map):
    return pl.BlockSpec(shape, index_map, pipeline_mode=pl.Buffered(1))


def _ada_kernel(ct_ref, w_ref, b_ref, o_ref, *, n_rows):
    ct = ct_ref[...]
    s = ct * jax.nn.sigmoid(ct)
    w = w_ref[0]
    rows = [jnp.sum(w * s[:, r:r + 1], axis=0, keepdims=True) + b_ref[0] for r in range(n_rows)]
    rows.append(jnp.zeros((SUBLANES - n_rows, w.shape[1]), F32))
    o_ref[0] = jnp.concatenate(rows, axis=0)


def _ada_modulation(cvec, n_rows, ada_w, ada_b):
    depth, d, n = ada_w.shape
    tn = ADA_COLUMN_TILE
    assert n % tn == 0
    return pl.pallas_call(
        functools.partial(_ada_kernel, n_rows=n_rows),
        grid=(depth, n // tn),
        in_specs=[pl.BlockSpec((d, SUBLANES), lambda l, j: (0, 0)),
                  pl.BlockSpec((1, d, tn), lambda l, j: (l, 0, j)),
                  pl.BlockSpec((1, 1, tn), lambda l, j: (l, 0, j))],
        out_specs=pl.BlockSpec((1, SUBLANES, tn), lambda l, j: (l, 0, j)),
        out_shape=jax.ShapeDtypeStruct((depth, SUBLANES, n), F32),
        compiler_params=_params("parallel", "parallel"),
        name="ada_modulation",
    )(cvec.T, ada_w, ada_b.reshape(depth, 1, n))


def _row_is_context(tm, first_row, ctx_rows):
    return (first_row + lax.broadcasted_iota(jnp.int32, (tm, 1), 0)) < ctx_rows


def _modulated_norm(x, g, mod, first_row, ctx_rows):
    tm = x.shape[0]
    y = x * lax.rsqrt(jnp.mean(x * x, axis=-1, keepdims=True) + RMS_EPS) * g
    if ctx_rows == 0:
        shift, scale, gate = mod[3:4], mod[4:5], mod[5:6]
    else:
        is_ctx = _row_is_context(tm, first_row, ctx_rows)
        shift = jnp.where(is_ctx, mod[0:1], mod[3:4])
        scale = jnp.where(is_ctx, mod[1:2], mod[4:5])
        gate = jnp.where(is_ctx, mod[2:3], mod[5:6])
    return y * (1.0 + scale) + shift, gate


def _gelu_tanh(y):
    return 0.5 * y * (1.0 + jnp.tanh(math.sqrt(2.0 / math.pi) * (y + 0.044715 * (y * y * y))))


def _lane_group_transpose(xs):
    n = len(xs)
    assert n * S5_GROUP == LANES and n & (n - 1) == 0
    group = lax.broadcasted_iota(jnp.int32, xs[0].shape, 1) // S5_GROUP
    xs = list(xs)
    k = 1
    while k < n:
        bit_set = (group & k) != 0
        for i in range(n):
            if i & k:
                continue
            lo, hi = xs[i], xs[i | k]
            xs[i] = jnp.where(bit_set, pltpu.roll(hi, S5_GROUP * k, 1), lo)
            xs[i | k] = jnp.where(bit_set, hi, pltpu.roll(lo, LANES - S5_GROUP * k, 1))
        k *= 2
    return xs


def _stage_kernel(*refs, pre, post, n_convert, ctx_rows, hidden, head_dim):
    it = iter(refs)
    x_ref = next(it)
    tm, d = x_ref.shape[1], x_ref.shape[2]
    first_row = pl.program_id(1) * tm
    x = x_ref[0]
    if pre == "join":
        head = next(it)[0]
        if ctx_rows < tm:
            head = jnp.concatenate([head, x[:tm - ctx_rows]], axis=0)
        x = jnp.where(pl.program_id(1) == 0, head, x)

    def gate_of(mod):
        if ctx_rows == 0:
            return mod[5:6]
        return jnp.where(_row_is_context(tm, first_row, ctx_rows), mod[2:3], mod[5:6])

    n_blocks = d // LANES
    chunk = LANES // S5_GROUP
    by_block = refs[-1]
    if pre == "s5":
        yg_ref, u_ref, dsk_ref, pmod_ref, wglu_ref = (next(it) for _ in range(5))
        z = None
        for c0 in range(0, d, MXU_DIM):
            cols = slice(c0, c0 + MXU_DIM)
            blocks = range(c0 // LANES, (c0 + MXU_DIM) // LANES)
            for o in blocks:
                per_group = [yg_ref[0, :, (o * chunk + g) * LANES:(o * chunk + g + 1) * LANES]
                             for g in range(chunk)]
                for t, y_t in enumerate(_lane_group_transpose(per_group)):
                    by_block[o, pl.ds(t, tm // chunk, stride=chunk), :] = y_t
            y = jnp.concatenate([by_block[o] for o in blocks], axis=1)
            y = y + u_ref[0, :, cols] * dsk_ref[:, cols]
            part = _dot(_gelu_tanh(y).astype(BF16), wglu_ref[cols, :])
            z = part if z is None else z + part
        x = x + gate_of(pmod_ref[0]) * (z[:, :d] * jax.nn.sigmoid(z[:, d:]))
    elif pre == "na":
        a_ref, pmod_ref, wo_ref = next(it), next(it), next(it)
        a = jnp.concatenate([a_ref[0, p] for p in range(a_ref.shape[1])], axis=-1)
        x = x + gate_of(pmod_ref[0]) * _dot(a, wo_ref[...])

    mod_ref, g_ref, win_ref, wout_ref = next(it), next(it), next(it), next(it)
    h, gate = _modulated_norm(x, g_ref[...], mod_ref[0], first_row, ctx_rows)
    hb = h.astype(BF16)
    gt = _dot(hb, win_ref[:, :hidden])
    up = _dot(hb, win_ref[:, hidden:])
    act = (gt * jax.nn.sigmoid(gt) * up).astype(BF16)
    x = x + (MACARON_WEIGHT * gate) * _dot(act, wout_ref[...])

    if post == "s5":
        qmod_ref, qg_ref, w_ref = next(it), next(it), next(it)
    elif post == "qkv":
        (qmod_ref, qg_ref, w_ref, reduce_ref, expand_ref,
         qgain_ref, kgain_ref) = (next(it) for _ in range(7))

    cast_in = [next(it) for _ in range(n_convert)]
    n_scratch = 1 if "s5" in (pre, post) else 0
    cast_out = refs[len(refs) - n_scratch - n_convert:len(refs) - n_scratch]

    @pl.when(pl.program_id(0) * pl.num_programs(1) + pl.program_id(1) < WEIGHT_CAST_CHUNKS)
    def _():
        for src, dst in zip(cast_in, cast_out):
            dst[...] = src[...].astype(BF16)

    o_ref = next(it)
    o_ref[0] = x
    if post is None:
        return
    h2, _ = _modulated_norm(x, qg_ref[...], qmod_ref[0], first_row, ctx_rows)
    proj = _dot(h2.astype(BF16), w_ref[...])
    if post == "s5":
        u_out, ug_out = next(it), next(it)
        u_out[0] = proj
        for o in range(n_blocks):
            by_block[o] = proj[:, o * LANES:(o + 1) * LANES]
            per_token = [by_block[o, pl.ds(s, tm // chunk, stride=chunk), :] for s in range(chunk)]
            for g, u_g in enumerate(_lane_group_transpose(per_token)):
                lanes = slice((o * chunk + g) * LANES, (o * chunk + g + 1) * LANES)
                ug_out[0, :, lanes] = u_g.astype(BF16)
        return

    def head_norm(z, gain):
        ms = _dot((z * z).astype(BF16), reduce_ref[...]) * (1.0 / head_dim)
        hi, lo = _split_bf16(lax.rsqrt(ms + RMS_EPS))
        return z * _dot(jnp.concatenate([hi, lo], axis=1), expand_ref[...]) * gain

    q = head_norm(proj[:, :d], qgain_ref[...]) * (head_dim ** -0.5 * LOG2E)
    k = head_norm(proj[:, d:2 * d], kgain_ref[...])
    v = proj[:, 2 * d:]
    q_ref, kv_ref = next(it), next(it)
    for hp in range(d // LANES):
        sl = slice(hp * LANES, (hp + 1) * LANES)
        q_ref[0, hp] = q[:, sl].astype(BF16)
        kv_ref[0, hp, :, :LANES] = k[:, sl].astype(BF16)
        kv_ref[0, hp, :, LANES:] = v[:, sl].astype(BF16)


def _stage(xs, ffn, *, pre=None, post=None, convert=(), ctx_rows, tile, name):
    b, s, d = xs.shape
    tm = tile
    mod, gain, w_in, w_out, which = ffn
    hidden = w_out.shape[-2]
    head_dim = d // NA_HEADS
    tok = pl.BlockSpec((1, tm, d), lambda i, t: (i, t, 0))
    modspec = pl.BlockSpec((1, 6, d), lambda i, t: (i, 0, 0))
    vec = pl.BlockSpec((1, d), lambda i, t: (0, 0))

    s5_chunk = LANES // S5_GROUP
    uses_s5 = "s5" in (None if pre is None else pre[0], None if post is None else post[0])
    assert not uses_s5 or tm % (s5_chunk * BF16_SUBLANES) == 0

    def chunk_rows(a):
        return pl.BlockSpec((1, tm // s5_chunk, a.shape[2]), lambda i, t: (i, t, 0))

    def weight(w, lead=()):
        return _resident((None,) * len(lead) + w.shape[len(lead):], lambda i, t: lead + (0, 0))

    def rows_from(offset):
        step = math.gcd(tm, ctx_rows)
        n_tile, n_ctx = tm // step, ctx_rows // step
        return pl.BlockSpec(tuple(pl.Element(n) for n in (1, tm, d)),
                            lambda i, t: (i, offset(t, n_tile, n_ctx) * step, 0))

    args = [xs]
    if pre is not None and pre[0] == "join":
        assert tm >= ctx_rows
        s += ctx_rows
        args.append(pre[1])
        in_specs = [rows_from(lambda t, n, c: jnp.maximum(t * n - c, 0)),
                    pl.BlockSpec((1, ctx_rows, d), lambda i, t: (i, 0, 0))]
    elif pre is not None and pre[0] == "na":
        in_specs = [rows_from(lambda t, n, c: t * n + c)]
        s, ctx_rows = s - ctx_rows, 0
    else:
        in_specs = [tok]
    if pre is not None and pre[0] == "s5":
        _, yg, u, d_skip, pmod, w_glu = pre
        args += [yg, u, d_skip, pmod, w_glu]
        in_specs += [chunk_rows(yg), tok, vec, modspec, weight(w_glu)]
    elif pre is not None and pre[0] == "na":
        _, attn, pmod, w_o = pre
        args += [attn, pmod, w_o]
        in_specs += [pl.BlockSpec((1, attn.shape[1], tm, LANES), lambda i, t: (i, 0, t, 0)),
                     modspec, weight(w_o)]
    args += [mod, gain, w_in, w_out]
    in_specs += [modspec, vec, weight(w_in, which), weight(w_out, which)]
    out_shape = [jax.ShapeDtypeStruct((b, s, d), F32)]
    out_specs = [tok]
    if post is not None and post[0] == "s5":
        _, qmod, qgain, w = post
        args += [qmod, qgain, w]
        in_specs += [modspec, vec, weight(w)]
        ug = jax.ShapeDtypeStruct((b, s // s5_chunk, s5_chunk * d), BF16)
        out_shape += [jax.ShapeDtypeStruct((b, s, d), F32), ug]
        out_specs += [tok, chunk_rows(ug)]
    elif post is not None:
        _, qmod, qgain, w, q_gain, k_gain = post
        hid = jnp.arange(d) // head_dim
        to_head = (hid[:, None] == jnp.arange(LANES)[None, :]).astype(BF16)
        from_head = jnp.concatenate([to_head.T, to_head.T], axis=0)
        args += [qmod, qgain, w, to_head, from_head,
                 jnp.tile(q_gain, NA_HEADS).reshape(1, d), jnp.tile(k_gain, NA_HEADS).reshape(1, d)]
        in_specs += [modspec, vec, weight(w), weight(to_head), weight(from_head), vec, vec]
        hp = d // LANES
        out_shape += [jax.ShapeDtypeStruct((b, hp, s, LANES), BF16),
                      jax.ShapeDtypeStruct((b, hp, s, 2 * LANES), BF16)]
        out_specs += [pl.BlockSpec((1, hp, tm, LANES), lambda i, t: (i, 0, t, 0)),
                      pl.BlockSpec((1, hp, tm, 2 * LANES), lambda i, t: (i, 0, t, 0))]
    n_t = s // tm
    for w, lead in convert:
        rows, cols = w.shape[-2:]
        chunk = rows // WEIGHT_CAST_CHUNKS
        assert chunk * WEIGHT_CAST_CHUNKS == rows and chunk % BF16_SUBLANES == 0
        assert b * n_t >= WEIGHT_CAST_CHUNKS

        def chunk_of(i, t):
            return jnp.minimum(i * n_t + t, WEIGHT_CAST_CHUNKS - 1)

        args.append(w)
        in_specs.append(pl.BlockSpec((None,) * len(lead) + (chunk, cols),
                                     lambda i, t, lead=lead: lead + (chunk_of(i, t), 0)))
        out_shape.append(jax.ShapeDtypeStruct((rows, cols), BF16))
        out_specs.append(pl.BlockSpec((chunk, cols), lambda i, t: (chunk_of(i, t), 0)))
    return pl.pallas_call(
        functools.partial(_stage_kernel, pre=None if pre is None else pre[0],
                          post=None if post is None else post[0], n_convert=len(convert),
                          ctx_rows=ctx_rows, hidden=hidden, head_dim=head_dim),
        grid=(b, n_t),
        in_specs=in_specs,
        out_specs=out_specs,
        out_shape=out_shape,
        scratch_shapes=[pltpu.VMEM((d // LANES, tm, LANES), F32)] if uses_s5 else [],
        compiler_params=_params("arbitrary", "arbitrary"),
        name=name,
    )(*args)


def _s5_prep_kernel(lr_ref, li_ref, ls_ref, btr_ref, bti_ref, cr_ref, ci_ref,
                    tpd_ref, wst_ref, prdt_ref, avec_ref, qs, *, chunk, seg_chunks):
    n = LANES
    ng = S5_LANE_GROUPS
    half = n // 2
    wide = ng * half
    lr = jnp.minimum(lr_ref[0], S5_MIN_NEG_RE)
    li = li_ref[0]
    dt = jnp.exp(ls_ref[0])
    mag = jnp.exp(lr * dt)
    ar = mag * jnp.cos(li * dt)
    ai = mag * jnp.sin(li * dt)
    den = lr * lr + li * li
    zr = ((ar - 1.0) * lr + ai * li) / den
    zi = (ai * lr - (ar - 1.0) * li) / den
    btr, bti = btr_ref[0], bti_ref[0]
    bbr = zr * btr - zi * bti
    bbi = zr * bti + zi * btr
    cr, ci = cr_ref[0], ci_ref[0]

    first_half = lax.broadcasted_iota(jnp.int32, (n, n), 1) < half

    powers = [(jnp.ones((n, n), F32), jnp.zeros((n, n), F32))]
    for _ in range(chunk):
        er, ei = powers[-1]
        powers.append((er * ar - ei * ai, er * ai + ei * ar))

    def input_map(k):
        er, ei = powers[k]
        return bbr * er - bbi * ei, bbr * ei + bbi * er

    def output_map(k):
        er, ei = powers[k]
        return cr * er - ci * ei, cr * ei + ci * er

    gs = S5_GROUP
    lane16 = lax.broadcasted_iota(jnp.int32, (gs, n), 1)
    zero16 = jnp.zeros((gs, n), BF16)

    def emit(reverse):
        for k in range(chunk):
            pr, pi = output_map(k)
            qk = jnp.where(first_half, pr, -pi)
            j = chunk - 1 - k if reverse else k
            for g in range(ng):
                qs[g * n + j * gs:g * n + (j + 1) * gs, :] = qk[g * gs:(g + 1) * gs]
        kt_all = _dot_nt_f32(jnp.where(first_half, bbr, bbi), qs[...])
        for s in range(chunk):
            wr, wi = input_map(s if reverse else chunk - 1 - s)
            pr, pi = output_map(chunk - s if reverse else s + 1)
            for g in range(ng):
                pair, g_in = divmod(g, 2)
                own = (lane16 // half) == g_in
                grp = slice(g * gs, (g + 1) * gs)
                rows = slice(g_in * n + s * gs, g_in * n + (s + 1) * gs)
                wst_ref[0, 0, pair, rows, 0:n] = jnp.where(own, wr[grp], 0.0).astype(BF16)
                wst_ref[0, 0, pair, rows, n:2 * n] = jnp.where(own, wi[grp], 0.0).astype(BF16)
                prdt_ref[0, 0, pair, rows, 0:n] = jnp.where(own, pr[grp], 0.0).astype(BF16)
                prdt_ref[0, 0, pair, rows, n:2 * n] = jnp.where(own, -pi[grp], 0.0).astype(BF16)
                kt = kt_all[grp, g * n:(g + 1) * n]
                if reverse:
                    shift = (chunk - 1 - s) * gs
                    blk = jnp.where(lane16 < (s + 1) * gs, pltpu.roll(kt, (n - shift) % n, 1), 0.0)
                else:
                    blk = jnp.where(lane16 >= s * gs, pltpu.roll(kt, s * gs, 1), 0.0)
                tpd_ref[0, 0, pair, rows, g_in * n:(g_in + 1) * n] = blk.astype(BF16)
                tpd_ref[0, 0, pair, rows, (1 - g_in) * n:(2 - g_in) * n] = zero16

    @pl.when(pl.program_id(0) == 0)
    def _():
        emit(False)

    @pl.when(pl.program_id(0) == 1)
    def _():
        emit(True)

    er, ei = powers[chunk]
    sr, si = jnp.ones((n, n), F32), jnp.zeros((n, n), F32)
    pr_, pi_ = er, ei
    e = seg_chunks
    while e:
        if e & 1:
            sr, si = sr * pr_ - si * pi_, sr * pi_ + si * pr_
        pr_, pi_ = pr_ * pr_ - pi_ * pi_, 2.0 * pr_ * pi_
        e >>= 1
    for r, v in enumerate((er, ei, sr, si)):
        for j in range(ng // 2):
            g0 = 2 * j * S5_GROUP
            g1 = g0 + S5_GROUP
            avec_ref[0, 0, r:r + 1, j * n:(j + 1) * n] = jnp.where(
                first_half[0:1], v[g0:g0 + 1], v[g1:g1 + 1])
    avec_ref[0, 0, 4:SUBLANES, :] = jnp.zeros((SUBLANES - 4, wide), F32)


def _s5_prep(lam_re, lam_im, log_step, b_re, b_im, c_re, c_im, chunk, seg_chunks):
    nd, g, p = lam_re.shape
    gs = S5_GROUP
    assert p == S5_STATE and 2 * p == LANES and g % S5_LANE_GROUPS == 0
    n_blocks = g // S5_LANE_GROUPS
    wide = S5_LANE_GROUPS * p

    def rows(v):
        v = v.reshape(nd, g * gs, p)
        return jnp.concatenate([v, v], axis=-1)

    def per_group(v):
        return rows(jnp.broadcast_to(v[:, :, None, :], (nd, g, gs, p)))

    args = (per_group(lam_re), per_group(lam_im),
            per_group(jnp.broadcast_to(log_step[:, :, None], (nd, g, p))),
            rows(jnp.swapaxes(b_re, 2, 3)), rows(jnp.swapaxes(b_im, 2, 3)), rows(c_re), rows(c_im))
    sq = pl.BlockSpec((1, LANES, LANES), lambda d, i: (d, i, 0))
    n_pairs = S5_LANE_GROUPS // 2
    pw = 2 * LANES
    assert chunk * gs == LANES
    mats = pl.BlockSpec((1, 1, n_pairs, pw, pw), lambda d, i: (d, i, 0, 0, 0))
    return pl.pallas_call(
        functools.partial(_s5_prep_kernel, chunk=chunk, seg_chunks=seg_chunks),
        grid=(nd, n_blocks),
        in_specs=[sq] * 7,
        out_specs=[mats, mats, mats,
                   pl.BlockSpec((1, 1, SUBLANES, wide), lambda d, i: (d, i, 0, 0))],
        out_shape=[jax.ShapeDtypeStruct((nd, n_blocks, n_pairs, pw, pw), BF16)] * 3
                  + [jax.ShapeDtypeStruct((nd, n_blocks, SUBLANES, wide), F32)],
        scratch_shapes=[pltpu.VMEM((S5_LANE_GROUPS * LANES, LANES), F32)],
        compiler_params=_params("parallel", "parallel"),
        name="s5_prep",
    )(*args)


def _s5_scan_kernel(ug_ref, tpd_ref, wst_ref, prdt_ref, avec_ref, yg_ref, sv, *, n_ctx):
    n = LANES
    pw = 2 * n
    n_c = ug_ref.shape[1]
    nb = sv.shape[0] // 2
    n_seg = SUBLANES
    seg = n_c // n_seg
    n_lat = n_c - n_ctx
    direction = pl.program_id(2)

    def run(reverse):
        def summarise(dst, src):
            for p in range(nb):
                s = _dot(ug_ref[0, src, p * pw:(p + 1) * pw], wst_ref[0, 0, p])
                sv[p, dst, :] = s[:, :n]
                sv[nb + p, dst, :] = s[:, n:]

        if reverse:
            summarise(slice(0, n_lat), slice(n_ctx, n_c))
            summarise(slice(n_lat, n_c), slice(0, n_ctx))
        else:
            summarise(slice(0, n_c), slice(0, n_c))

        intra = jnp.concatenate([_dot(ug_ref[0, :, p * pw:(p + 1) * pw], tpd_ref[0, 0, p])
                                 for p in range(nb)], axis=1)

        def lane_blocks(r):
            return [avec_ref[0, 0, r:r + 1, j * n:(j + 1) * n] for j in range(nb)]

        ar = [jnp.broadcast_to(v, (n_seg, n)) for v in lane_blocks(0)]
        ai = [jnp.broadcast_to(v, (n_seg, n)) for v in lane_blocks(1)]

        def rows_at(i):
            return pl.ds(seg - 1 - i if reverse else i, n_seg, stride=seg)

        def advance(h, rows):
            out_r, out_i = [], []
            for j in range(nb):
                hr, hi = h[j], h[nb + j]
                out_r.append(ar[j] * hr - ai[j] * hi + sv[j, rows, :])
                out_i.append(ar[j] * hi + ai[j] * hr + sv[nb + j, rows, :])
            return tuple(out_r + out_i)

        zero = jnp.zeros((n_seg, n), F32)
        fin = lax.fori_loop(0, seg, lambda i, h: advance(h, rows_at(i)), (zero,) * (2 * nb),
                            unroll=True)

        asr, asi = lane_blocks(2), lane_blocks(3)
        order = range(n_seg - 1, -1, -1) if reverse else range(n_seg)
        h0 = []
        for j in range(nb):
            rows_r, rows_i = [None] * n_seg, [None] * n_seg
            pr = pi = jnp.zeros((1, n), F32)
            prev = None
            for sgm in order:
                if prev is not None:
                    pr, pi = (asr[j] * pr - asi[j] * pi + fin[j][prev:prev + 1],
                              asr[j] * pi + asi[j] * pr + fin[nb + j][prev:prev + 1])
                rows_r[sgm], rows_i[sgm] = pr, pi
                prev = sgm
            h0.append((jnp.concatenate(rows_r, axis=0), jnp.concatenate(rows_i, axis=0)))
        h0 = tuple(v[0] for v in h0) + tuple(v[1] for v in h0)

        def step(i, h):
            rows = rows_at(i)
            nxt = advance(h, rows)
            for j in range(2 * nb):
                sv[j, rows, :] = h[j]
            return nxt

        lax.fori_loop(0, seg, step, h0, unroll=True)

        def entry_states(p, rows):
            return jnp.concatenate([sv[p, rows, :], sv[nb + p, rows, :]], axis=1).astype(BF16)

        from_states = []
        for p in range(nb):
            if reverse:
                h = jnp.concatenate([entry_states(p, slice(n_lat, n_c)),
                                     entry_states(p, slice(0, n_lat))], axis=0)
            else:
                h = entry_states(p, slice(0, n_c))
            from_states.append(_dot_nt(h, prdt_ref[0, 0, p]))
        return intra + jnp.concatenate(from_states, axis=1)

    @pl.when(direction == 0)
    def _():
        yg_ref[0] = run(False)

    @pl.when(direction == 1)
    def _():
        yg_ref[0] = yg_ref[0] + run(True)


def _s5_scan(ug, mats, chunk, ctx_rows):
    b, n_c, width = ug.shape
    tpd, wst, prdt, avec = mats
    nd, n_blocks, n_pairs, pw, _ = tpd.shape
    wide = avec.shape[-1]
    block = S5_LANE_GROUPS * LANES
    assert width == n_blocks * block

    def mat(*shape):
        return pl.BlockSpec((1, 1) + shape, lambda o, i, dr: (dr, o) + (0,) * len(shape))

    rows = pl.BlockSpec((1, n_c, block), lambda o, i, dr: (i, 0, o))
    return pl.pallas_call(
        functools.partial(_s5_scan_kernel, n_ctx=ctx_rows // chunk),
        grid=(n_blocks, b, nd),
        in_specs=[rows, mat(n_pairs, pw, pw), mat(n_pairs, pw, pw), mat(n_pairs, pw, pw),
                  mat(SUBLANES, wide)],
        out_specs=rows,
        out_shape=jax.ShapeDtypeStruct((b, n_c, width), F32),
        scratch_shapes=[pltpu.VMEM((2 * n_pairs, n_c, LANES), F32)],
        compiler_params=_params("parallel", "parallel", "arbitrary"),
        name="s5_scan",
    )(ug, tpd, wst, prdt, avec)


def _window_start(r, rows):
    kh = min(WIN_H, rows)
    return min(max(r - kh // 2, 0), rows - kh)


def _na_tile_geometry(kind, rows):
    n_tiles = rows // NA_Q_ROWS
    tile = {0: 0, 1: 1, 2: n_tiles - 1}[kind]
    q0 = tile * NA_Q_ROWS
    k0 = min(max(q0 - (NA_K_ROWS - NA_Q_ROWS) // 2, 0), rows - NA_K_ROWS)
    return q0, k0


def _na_sub_window(kind, sub, rows):
    q0, k0 = _na_tile_geometry(kind, rows)
    kh = min(WIN_H, rows)
    starts = [_window_start(q0 + NA_SUB_Q_ROWS * sub + i, rows) - k0 for i in range(NA_SUB_Q_ROWS)]
    first = min(min(starts) // NA_SUB_Q_ROWS * NA_SUB_Q_ROWS, NA_K_ROWS - NA_SUB_K_ROWS)
    assert first >= 0 and max(starts) + kh <= first + NA_SUB_K_ROWS
    return first


def _na_bias_kernel(rpb_ref, o_ref, *, rows):
    h = pl.program_id(0)
    w = GRID_W
    kh = min(WIN_H, rows)
    ncol = 2 * WIN_W - 1
    nrow = 2 * WIN_H - 1
    cq = lax.broadcasted_iota(jnp.int32, (w, LANES), 0)
    lane = lax.broadcasted_iota(jnp.int32, (w, LANES), 1)
    ck = lane % w
    left = lane < w
    cs = jnp.clip(cq - WIN_W // 2, 0, w - WIN_W)
    col_ok = (ck >= cs) & (ck < cs + WIN_W)
    dc = jnp.clip(ck - cq + WIN_W - 1, 0, ncol - 1)
    neg = jnp.full((w, LANES), -jnp.inf, F32)

    def pair_table(d_left, d_right):
        t = jnp.zeros((w, LANES), F32)
        for j in range(ncol):
            vl = rpb_ref[(h * nrow + d_left) * ncol + j] if d_left is not None else 0.0
            vr = rpb_ref[(h * nrow + d_right) * ncol + j] if d_right is not None else 0.0
            t = jnp.where(dc == j, jnp.where(left, vl, vr), t)
        t = t * LOG2E
        ok = col_ok
        if d_left is None:
            ok = ok & jnp.logical_not(left)
        if d_right is None:
            ok = ok & left
        return jnp.where(ok, t, neg)

    cache = {}
    for kind in range(3):
        q0, k0 = _na_tile_geometry(kind, rows)
        for rq in range(NA_Q_ROWS):
            r = q0 + rq
            rs = _window_start(r, rows)
            first = k0 + _na_sub_window(kind, rq // NA_SUB_Q_ROWS, rows)
            for m in range(NA_SUB_K_ROWS // 2):
                ds = []
                for kr in (first + 2 * m, first + 2 * m + 1):
                    ds.append(kr - r + WIN_H - 1 if rs <= kr < rs + kh else None)
                key = tuple(ds)
                if key == (None, None):
                    blk = neg
                else:
                    if key not in cache:
                        cache[key] = pair_table(*key)
                    blk = cache[key]
                o_ref[kind, 0, rq * w:(rq + 1) * w, m * LANES:(m + 1) * LANES] = blk


def _na_bias(rpb, rows):
    nh = rpb.shape[0]
    nq, nk = NA_Q_ROWS * GRID_W, NA_SUB_K_ROWS * GRID_W
    return pl.pallas_call(
        functools.partial(_na_bias_kernel, rows=rows),
        grid=(nh,),
        in_specs=[pl.BlockSpec(memory_space=pltpu.SMEM)],
        out_specs=pl.BlockSpec((3, 1, nq, nk), lambda h: (0, h, 0, 0)),
        out_shape=jax.ShapeDtypeStruct((3, nh, nq, nk), F32),
        compiler_params=_params("parallel"),
        name="na_bias",
    )(rpb.reshape(-1))


def _na_attn_kernel(*refs, n_q, n_tiles):
    n_sub = n_tiles * n_q
    q_ref = refs[0]
    kv_refs = refs[1:1 + n_sub]
    kvc_ref = refs[1 + n_sub]
    bias_refs = refs[2 + n_sub:2 + n_sub + n_tiles]
    o_ref = refs[2 + n_sub + n_tiles]
    tb = kvc_ref.shape[2]
    half = LANES // 2

    def per_head(x, other):
        first_head = lax.broadcasted_iota(jnp.int32, x.shape, 1) < half
        fill = jnp.full_like(x, other)
        return [jnp.where(first_head, x, fill), jnp.where(first_head, fill, x)]

    kc = kvc_ref[0, 0, :, :LANES]
    vc_heads = per_head(kvc_ref[0, 0, :, LANES:], 1.0)
    units = [(sub, hh) for sub in range(n_sub) for hh in range(2)]
    keys, values, queries = {}, {}, {}
    for sub in range(n_sub):
        keys[sub] = kv_refs[sub][0, 0, :, :LANES]
        values[sub] = per_head(kv_refs[sub][0, 0, :, LANES:], 1.0)
        queries[sub] = per_head(q_ref[0, 0, sub * tb:(sub + 1) * tb, :], 0.0)

    def score(unit):
        sub, hh = unit
        qh = queries[sub][hh]
        in_tile = sub % n_q
        bias = bias_refs[sub // n_q][0, hh, in_tile * tb:(in_tile + 1) * tb, :]
        return _dot_nt(qh, keys[sub]) + bias, _dot_nt(qh, kc)

    ahead = 8
    pending = [score(u) for u in units[:ahead]]
    outs = {}
    for i, (sub, hh) in enumerate(units):
        s, sc = pending.pop(0)
        if i + ahead < len(units):
            pending.append(score(units[i + ahead]))
        m = jnp.maximum(jnp.max(s, axis=-1, keepdims=True), jnp.max(sc, axis=-1, keepdims=True))
        p = jnp.exp2(s - m).astype(BF16)
        pc = jnp.exp2(sc - m).astype(BF16)
        o = _dot(p, values[sub][hh]) + _dot(pc, vc_heads[hh])
        denom = o[:, half:half + 1] if hh == 0 else o[:, 0:1]
        outs[sub, hh] = o / denom
    first_head = lax.broadcasted_iota(jnp.int32, (tb, LANES), 1) < half
    for sub in range(n_sub):
        o_ref[0, 0, sub * tb:(sub + 1) * tb, :] = jnp.where(
            first_head, outs[sub, 0], outs[sub, 1]).astype(BF16)


def _na_attention(q, kv, bias, ctx_rows):
    b, hp, s, _ = q.shape
    seq = s - ctx_rows
    tb = TOKEN_TILE
    assert ctx_rows == tb and kv.shape[2] == s
    rows = seq // GRID_W
    n_tiles = rows // NA_Q_ROWS
    assert n_tiles >= 3
    assert NA_SUB_Q_ROWS * GRID_W == tb
    n_q = NA_Q_ROWS // NA_SUB_Q_ROWS
    n_k = NA_SUB_K_ROWS * GRID_W // tb
    slab = NA_K_ROWS * GRID_W // tb
    last_k0 = (rows - NA_K_ROWS) * GRID_W // tb

    per_step = NA_TILES_PER_STEP
    assert n_tiles % per_step == 0

    def pick(tile, by_kind):
        return jnp.where(tile == 0, by_kind[0], jnp.where(tile == n_tiles - 1, by_kind[2], by_kind[1]))

    def window_map(j, sub):
        offs = [_na_sub_window(kind, sub, rows) * GRID_W // tb for kind in range(3)]

        def f(p, bi, t):
            tile = per_step * t + j
            first = jnp.clip(n_q * tile - (slab - n_q) // 2, 0, last_k0)
            return (bi, p, (1 + first + pick(tile, offs)) * tb, 0)
        return f

    def bias_map(j):
        return lambda p, bi, t: (pick(per_step * t + j, (0, 1, 2)), p, 0, 0)

    nq = per_step * NA_Q_ROWS * GRID_W
    step_spec = pl.BlockSpec((1, 1, nq, LANES), lambda p, bi, t: (bi, p, t, 0))
    window_shape = tuple(pl.Element(n) for n in (1, 1, n_k * tb, 2 * LANES))
    kv_specs = [pl.BlockSpec(window_shape, window_map(j, sub))
                for j in range(per_step) for sub in range(n_q)]
    bias_specs = [pl.BlockSpec((1, 2) + bias.shape[2:], bias_map(j)) for j in range(per_step)]
    q_spec = pl.BlockSpec(tuple(pl.Element(n) for n in (1, 1, nq, LANES)),
                          lambda p, bi, t: (bi, p, (1 + t * (nq // tb)) * tb, 0))
    in_specs = ([q_spec] + kv_specs
                + [pl.BlockSpec((1, 1, tb, 2 * LANES), lambda p, bi, t: (bi, p, 0, 0))] + bias_specs)
    args = [q] + [kv] * (len(kv_specs) + 1) + [bias] * per_step
    return pl.pallas_call(
        functools.partial(_na_attn_kernel, n_q=n_q, n_tiles=per_step),
        grid=(hp, b, n_tiles // per_step),
        in_specs=in_specs,
        out_specs=step_spec,
        out_shape=jax.ShapeDtypeStruct((b, hp, seq, LANES), BF16),
        compiler_params=_params("parallel", "parallel", "parallel"),
        name="na_attention",
    )(*args)


def kernel(x, c, ctx, c_ctx, norm_g, ada_w, ada_b, ffn_w_in, ffn_w_out, ssm_w_in, ssm_lambda_re, ssm_lambda_im, ssm_log_step, ssm_b_re, ssm_b_im, ssm_c_re, ssm_c_im, ssm_d, ssm_w_glu, na_w_qkv, na_q_norm, na_k_norm, na_rpb, na_w_o):
    b, seq, d = x.shape
    ctx_rows = ctx.shape[1]
    depth = norm_g.shape[0]
    assert b + 1 <= SUBLANES and ctx_rows == TOKEN_TILE and depth == 2

    cvec = jnp.zeros((SUBLANES, d), F32).at[:b].set(c).at[b].set(c_ctx)
    m = _ada_modulation(cvec, b + 1, ada_w, ada_b).reshape(depth, SUBLANES, N_SUB, 3, d)

    def mod_rows(layer, sub):
        lat = m[layer, :b, sub]
        cx = jnp.broadcast_to(m[layer, b, sub], (b, 3, d))
        return jnp.concatenate([cx, lat], axis=1)

    def gain(layer, sub):
        return norm_g[layer, sub].reshape(1, d)

    def ffn(layer, which, w_in, w_out):
        sub = 2 * which
        return mod_rows(layer, sub), gain(layer, sub), w_in, w_out, ()

    def ffn_f32(layer, which):
        return [(ffn_w_in, (layer, which)), (ffn_w_out, (layer, which))]

    xs, u, ug, w_in, w_out, w_glu = _stage(
        x, ffn(0, 0, ffn_w_in[0, 0].astype(BF16), ffn_w_out[0, 0].astype(BF16)),
        pre=("join", ctx), post=("s5", mod_rows(0, 1), gain(0, 1), ssm_w_in[0].astype(BF16)),
        convert=ffn_f32(0, 1) + [(ssm_w_glu, (0,))], ctx_rows=ctx_rows,
        tile=JOINT_STAGE_TILE, name="ffn_s5in")
    n_chunks = (ctx_rows + seq) // S5_CHUNK
    assert n_chunks % SUBLANES == 0 and ctx_rows % S5_CHUNK == 0
    mats = _s5_prep(ssm_lambda_re[0], ssm_lambda_im[0], ssm_log_step[0], ssm_b_re[0], ssm_b_im[0],
                    ssm_c_re[0], ssm_c_im[0], S5_CHUNK, n_chunks // SUBLANES)
    yg = _s5_scan(ug, mats, S5_CHUNK, ctx_rows)
    xs, w_in, w_out, w_qkv = _stage(
        xs, ffn(0, 1, w_in, w_out),
        pre=("s5", yg, u, ssm_d[0].reshape(1, d), mod_rows(0, 1), w_glu),
        convert=ffn_f32(1, 0) + [(na_w_qkv, (0,))], ctx_rows=ctx_rows,
        tile=JOINT_STAGE_TILE, name="s5out_ffn")

    xs, q, kv, w_in, w_out, w_o = _stage(
        xs, ffn(1, 0, w_in, w_out),
        post=("qkv", mod_rows(1, 1), gain(1, 1), w_qkv, na_q_norm[0], na_k_norm[0]),
        convert=ffn_f32(1, 1) + [(na_w_o, (0,))], ctx_rows=ctx_rows,
        tile=JOINT_STAGE_TILE, name="ffn_qkv")
    bias = _na_bias(na_rpb[0], seq // GRID_W)
    attn = _na_attention(q, kv, bias, ctx_rows)
    out, = _stage(xs, ffn(1, 1, w_in, w_out), pre=("na", attn, mod_rows(1, 1), w_o),
                  ctx_rows=ctx_rows, tile=LATENT_TOKEN_TILE, name="naout_ffn")
    return out
```
